```python
import jax, jax.numpy as jnp
from jax import lax
import numpy as np

D_MODEL = 1024
BATCH = 4
SEQ = 8192
DEPTH = 1

GRID_W = 64
CTX_LEN = 256
W_CONV = 1024
CONV_K = 31
W_LRU = 1024
LRU_HEADS = 8
LRU_HEAD_DIM = W_LRU // LRU_HEADS
LRU_CONV_K = 4
LRU_C = 8.0
N_GROUPS = 4
EXPERTS_PER_GROUP = 4
N_EXPERTS = N_GROUPS * EXPERTS_PER_GROUP
TOP_K_IN_GROUP = 2
D_EXPERT = 512
EPS = 1e-6
SPLITS = (W_CONV, 2 * W_CONV, 2 * W_CONV + W_LRU, 2 * W_CONV + 2 * W_LRU, 2 * W_CONV + 2 * W_LRU + D_MODEL)
REC0 = 2 * W_CONV + W_LRU
REC1 = 2 * W_CONV + 2 * W_LRU
IN_COLS = 2 * W_CONV + 2 * W_LRU + 2 * D_MODEL

kernel_name = "hybrid_conformer_rglru_hmoe_prefix_block"


def rms_norm(x, g):
    xf = x.astype(jnp.float32)
    y = xf * lax.rsqrt(jnp.mean(xf * xf, axis=-1, keepdims=True) + EPS)
    return (y * g.astype(jnp.float32)).astype(x.dtype)


def layer_norm(x, g, b):
    xf = x.astype(jnp.float32)
    mu = jnp.mean(xf, axis=-1, keepdims=True)
    xc = xf - mu
    y = xc * lax.rsqrt(jnp.mean(xc * xc, axis=-1, keepdims=True) + EPS)
    return (y * g.astype(jnp.float32) + b.astype(jnp.float32)).astype(x.dtype)


def ada_split(cond, w, b):
    m = jax.nn.silu(cond) @ w + b
    return jnp.split(m[..., None, :], 6, axis=-1)


def modulate(xn, shift, scale):
    return xn * (1 + scale) + shift


def depthwise_conv(x, w, b, pad_lo, pad_hi):
    y = lax.conv_general_dilated(
        x, w[:, None, :].astype(x.dtype), window_strides=(1,),
        padding=[(pad_lo, pad_hi)], dimension_numbers=("NWC", "WIO", "NWC"),
        feature_group_count=x.shape[-1])
    return y + b


def conformer_conv(u, v, w_dw, b_dw, ln_g, ln_b, w_pa):
    z = u * jax.nn.sigmoid(v)
    z = depthwise_conv(z, w_dw, b_dw, CONV_K // 2, CONV_K // 2)
    z = jax.nn.silu(layer_norm(z, ln_g, ln_b))
    return z @ w_pa


def rglru_coeffs(xc, w_r, b_r, w_i, b_i, lam):
    bsz, n = xc.shape[0], xc.shape[1]
    xh = xc.reshape(bsz, n, LRU_HEADS, LRU_HEAD_DIM)
    r = jax.nn.sigmoid(jnp.einsum("blhi,hij->blhj", xh, w_r) + b_r).reshape(bsz, n, W_LRU)
    i = jax.nn.sigmoid(jnp.einsum("blhi,hij->blhj", xh, w_i) + b_i).reshape(bsz, n, W_LRU)
    log_a = -LRU_C * r.astype(jnp.float32) * jax.nn.softplus(-lam.astype(jnp.float32))
    a = jnp.exp(log_a)
    b = jnp.sqrt(-jnp.expm1(2.0 * log_a)) * (i * xc).astype(jnp.float32)
    return a, b


def linear_scan(a, b, h0):
    b = b.at[:, 0].add(a[:, 0] * h0)

    def combine(left, right):
        a_l, b_l = left
        a_r, b_r = right
        return a_l * a_r, a_r * b_l + b_r

    _, h = lax.associative_scan(combine, (a, b), axis=1)
    return h


def rglru_bidir(xc, pf, pb, h0f, h0b):
    af, bf = rglru_coeffs(xc, *pf)
    ab, bb = rglru_coeffs(jnp.flip(xc, axis=1), *pb)
    return linear_scan(af, bf, h0f), linear_scan(ab, bb, h0b)


def split_proj(p):
    return jnp.split(p, SPLITS, axis=-1)


def merge_branches(br_a, br_b, g_a, g_b, w_o):
    return (jax.nn.sigmoid(g_a) * br_a + jax.nn.sigmoid(g_b) * br_b) @ w_o


def hier_moe(h, w_grp, b_grp, w_er, b_er, w_gate, w_up, w_down):
    n_tok = h.shape[0]
    g_logits = (h @ w_grp + b_grp).astype(jnp.float32)
    g_prob = jax.nn.softmax(g_logits, axis=-1)
    g_sel = jnp.argmax(g_logits, axis=-1)
    p_grp = jnp.max(g_prob, axis=-1, keepdims=True)
    e_logits = (h @ w_er + b_er).astype(jnp.float32).reshape(n_tok, N_GROUPS, EXPERTS_PER_GROUP)
    e_sel = jnp.einsum("nge,ng->ne", e_logits, jax.nn.one_hot(g_sel, N_GROUPS, dtype=jnp.float32))
    top_v, top_i = lax.top_k(e_sel, TOP_K_IN_GROUP)
    wts = p_grp * jax.nn.softmax(top_v, axis=-1)
    expert_id = g_sel[:, None] * EXPERTS_PER_GROUP + top_i
    combine = jnp.sum(jax.nn.one_hot(expert_id, N_EXPERTS, dtype=jnp.float32) * wts[..., None], axis=1)
    combine = combine.astype(h.dtype)
    out = jnp.zeros_like(h)
    for g in range(N_GROUPS):
        sl = slice(g * EXPERTS_PER_GROUP, (g + 1) * EXPERTS_PER_GROUP)
        hg = jnp.einsum("nd,edf->nef", h, w_gate[sl])
        hu = jnp.einsum("nd,edf->nef", h, w_up[sl])
        act = jax.nn.silu(hg) * hu * combine[:, sl, None]
        out = out + jnp.einsum("nef,efd->nd", act, w_down[sl])
    return out


def setup_inputs(seed: int = 0) -> dict:
    key = jax.random.key(seed)
    ks = jax.random.split(key, 40)
    f32 = jnp.float32
    D, L = D_MODEL, DEPTH
    H, hd = LRU_HEADS, LRU_HEAD_DIM

    def nrm(k, shape, scale):
        return jax.random.normal(k, shape, f32) * scale

    def lam_init(k):
        a_c = jax.random.uniform(k, (L, W_LRU), f32, minval=0.9, maxval=0.999)
        a = a_c ** (1.0 / LRU_C)
        return jnp.log(a) - jnp.log1p(-a)

    return {
        "x": nrm(ks[0], (BATCH, SEQ, D), 1.0),
        "c": nrm(ks[1], (BATCH, D), 1.0),
        "ctx": nrm(ks[2], (BATCH, CTX_LEN, D), 1.0),
        "c_ctx": nrm(ks[3], (D,), 1.0),
        "w_ada": nrm(ks[4], (L, D, 6 * D), 0.5 * D ** -0.5),
        "b_ada": nrm(ks[5], (L, 6 * D), 0.01),
        "g_mix": 1.0 + nrm(ks[6], (L, D), 0.01),
        "w_in": nrm(ks[7], (L, D, IN_COLS), D ** -0.5),
        "b_in": nrm(ks[8], (L, IN_COLS), 0.01),
        "conv_w": nrm(ks[9], (L, CONV_K, W_CONV), CONV_K ** -0.5),
        "conv_b": nrm(ks[10], (L, W_CONV), 0.01),
        "ln_g": 1.0 + nrm(ks[11], (L, W_CONV), 0.01),
        "ln_b": nrm(ks[12], (L, W_CONV), 0.01),
        "w_pa": nrm(ks[13], (L, W_CONV, D), W_CONV ** -0.5),
        "lru_conv_w": nrm(ks[14], (L, LRU_CONV_K, W_LRU), LRU_CONV_K ** -0.5),
        "lru_conv_b": nrm(ks[15], (L, W_LRU), 0.01),
        "w_r_f": nrm(ks[16], (L, H, hd, hd), hd ** -0.5),
        "b_r_f": nrm(ks[17], (L, H, hd), 0.01),
        "w_i_f": nrm(ks[18], (L, H, hd, hd), hd ** -0.5),
        "b_i_f": nrm(ks[19], (L, H, hd), 0.01),
        "lam_f": lam_init(ks[20]),
        "w_r_b": nrm(ks[21], (L, H, hd, hd), hd ** -0.5),
        "b_r_b": nrm(ks[22], (L, H, hd), 0.01),
        "w_i_b": nrm(ks[23], (L, H, hd, hd), hd ** -0.5),
        "b_i_b": nrm(ks[24], (L, H, hd), 0.01),
        "lam_b": lam_init(ks[25]),
        "w_pb": nrm(ks[26], (L, W_LRU, D), W_LRU ** -0.5),
        "w_o": nrm(ks[27], (L, D, D), D ** -0.5),
        "g_ffn": 1.0 + nrm(ks[28], (L, D), 0.01),
        "w_grp": nrm(ks[29], (L, D, N_GROUPS), D ** -0.5),
        "b_grp": nrm(ks[30], (L, N_GROUPS), 0.01),
        "w_er": nrm(ks[31], (L, D, N_EXPERTS), D ** -0.5),
        "b_er": nrm(ks[32], (L, N_EXPERTS), 0.01),
        "w_gate": nrm(ks[33], (L, N_EXPERTS, D, D_EXPERT), D ** -0.5),
        "w_up": nrm(ks[34], (L, N_EXPERTS, D, D_EXPERT), D ** -0.5),
        "w_down": nrm(ks[35], (L, N_EXPERTS, D_EXPERT, D), D_EXPERT ** -0.5),
        "g_final": 1.0 + nrm(ks[36], (D,), 0.01),
    }


def reference(x, c, ctx, c_ctx, w_ada, b_ada, g_mix, w_in, b_in, conv_w, conv_b, ln_g, ln_b, w_pa,
              lru_conv_w, lru_conv_b, w_r_f, b_r_f, w_i_f, b_i_f, lam_f, w_r_b, b_r_b, w_i_b, b_i_b, lam_b,
              w_pb, w_o, g_ffn, w_grp, b_grp, w_er, b_er, w_gate, w_up, w_down, g_final):
    bsz, n_lat, d = x.shape
    rows = n_lat // GRID_W
    n_ctx = ctx.shape[1]
    zero_state = jnp.zeros((bsz, W_LRU), jnp.float32)
    for l in range(DEPTH):
        last = l == DEPTH - 1
        sh1, sc1, gt1, sh2, sc2, gt2 = ada_split(c, w_ada[l], b_ada[l])
        csh1, csc1, cgt1, csh2, csc2, cgt2 = ada_split(c_ctx, w_ada[l], b_ada[l])
        pf = (w_r_f[l], b_r_f[l], w_i_f[l], b_i_f[l], lam_f[l])
        pb = (w_r_b[l], b_r_b[l], w_i_b[l], b_i_b[l], lam_b[l])

        hc = modulate(rms_norm(ctx, g_mix[l]), csh1, csc1)
        if last:
            xr_c = hc @ w_in[l][:, REC0:REC1] + b_in[l][REC0:REC1]
        else:
            u_c, v_c, yg_c, xr_c, ga_c, gb_c = split_proj(hc @ w_in[l] + b_in[l])
        xc_c = depthwise_conv(xr_c, lru_conv_w[l], lru_conv_b[l], LRU_CONV_K // 2, LRU_CONV_K - 1 - LRU_CONV_K // 2)
        hf_c, hb_c = rglru_bidir(xc_c, pf, pb, zero_state, zero_state)
        h0f, h0b = hf_c[:, -1], hb_c[:, -1]
        if not last:
            a_c = conformer_conv(u_c, v_c, conv_w[l], conv_b[l], ln_g[l], ln_b[l], w_pa[l])
            y_c = (hf_c + jnp.flip(hb_c, axis=1)).astype(ctx.dtype)
            b_c = (jax.nn.gelu(yg_c) * y_c) @ w_pb[l]
            ctx_new = ctx + cgt1 * merge_branches(a_c, b_c, ga_c, gb_c, w_o[l])
            hcf = modulate(rms_norm(ctx_new, g_ffn[l]), csh2, csc2)
            ctx_new = ctx_new + cgt2 * hier_moe(hcf.reshape(-1, d), w_grp[l], b_grp[l], w_er[l], b_er[l],
                                                w_gate[l], w_up[l], w_down[l]).reshape(bsz, n_ctx, d)

        hx = modulate(rms_norm(x, g_mix[l]), sh1, sc1)
        u, v, yg, xr, ga, gb = split_proj(hx @ w_in[l] + b_in[l])
        a_lat = conformer_conv(u.reshape(bsz * rows, GRID_W, W_CONV), v.reshape(bsz * rows, GRID_W, W_CONV),
                               conv_w[l], conv_b[l], ln_g[l], ln_b[l], w_pa[l]).reshape(bsz, n_lat, d)
        xc = depthwise_conv(xr, lru_conv_w[l], lru_conv_b[l], LRU_CONV_K // 2, LRU_CONV_K - 1 - LRU_CONV_K // 2)
        hf, hb = rglru_bidir(xc, pf, pb, h0f, h0b)
        y_rec = (hf + jnp.flip(hb, axis=1)).astype(x.dtype)
        b_lat = (jax.nn.gelu(yg) * y_rec) @ w_pb[l]
        x = x + gt1 * merge_branches(a_lat, b_lat, ga, gb, w_o[l])
        hm = modulate(rms_norm(x, g_ffn[l]), sh2, sc2)
        x = x + gt2 * hier_moe(hm.reshape(-1, d), w_grp[l], b_grp[l], w_er[l], b_er[l],
                               w_gate[l], w_up[l], w_down[l]).reshape(bsz, n_lat, d)
        if not last:
            ctx = ctx_new
    return rms_norm(x, g_final)
```

```python
import functools

import jax
import jax.numpy as jnp
from jax import lax
from jax.experimental import pallas as pl
from jax.experimental.pallas import tpu as pltpu

F32 = jnp.float32
BF16 = jnp.bfloat16
I32 = jnp.int32
HIGHEST = lax.Precision.HIGHEST

EPS = 1e-6
GRID_W = 64
CONV_K = 31
CONV_PAD = 16
LRU_CONV_K = 4
LRU_C = 8.0
LRU_HEADS = 8
N_GROUPS = 4
EXPERTS_PER_GROUP = 4
N_PAIRS = 6
N_BUCKETS = N_GROUPS * N_PAIRS
LANES = 128
SUBLANES = 8
VMEM_LIMIT = 56 * 1024 * 1024

T_TOK = 256
T_SCAN = 256
T_MOE = 256


def _sigmoid(x):
    return 0.5 * (jnp.tanh(0.5 * x) + 1.0)


def _silu(x):
    return x * _sigmoid(x)


def _gelu_tanh(x):
    return 0.5 * x * (1.0 + jnp.tanh(0.7978845608028654 * (x + 0.044715 * (x * x * x))))


def _rms_norm(x, g):
    return x * lax.rsqrt(jnp.mean(x * x, axis=-1, keepdims=True) + EPS) * g


def _bdot(a, b):
    return jnp.dot(a, b, preferred_element_type=F32)


def _cparams(sem):
    return pltpu.CompilerParams(dimension_semantics=sem, vmem_limit_bytes=VMEM_LIMIT)


def _ada_kernel(c_ref, w_ref, b_ref, o_ref):
    s = _silu(c_ref[...])
    o_ref[...] = jnp.dot(s, w_ref[...], precision=HIGHEST, preferred_element_type=F32) + b_ref[...]


def _ada(c8, w, b):
    d, n = w.shape
    blk = 1024
    return pl.pallas_call(
        _ada_kernel,
        grid=(n // blk,),
        in_specs=[pl.BlockSpec((8, d), lambda j: (0, 0)),
                  pl.BlockSpec((d, blk), lambda j: (0, j)),
                  pl.BlockSpec((1, blk), lambda j: (0, j))],
        out_specs=pl.BlockSpec((8, blk), lambda j: (0, j)),
        out_shape=jax.ShapeDtypeStruct((8, n), F32),
        compiler_params=_cparams(("arbitrary",)),
        name="ada",
    )(c8, w, b)


def _ctxproj_kernel(x_ref, mod_ref, g_ref, w_ref, b_ref, o_ref, *, row):
    d = x_ref.shape[-1]
    xn = _rms_norm(x_ref[0], g_ref[...])
    sh = mod_ref[row:row + 1, 0:d]
    sc = mod_ref[row:row + 1, d:2 * d]
    hx = (xn * (1.0 + sc) + sh).astype(BF16)
    o_ref[0] = _bdot(hx, w_ref[...]) + b_ref[...]


def _ctxproj(ctx, mod, g, w, b, row):
    bsz, n, d = ctx.shape
    return pl.pallas_call(
        functools.partial(_ctxproj_kernel, row=row),
        grid=(bsz,),
        in_specs=[pl.BlockSpec((1, n, d), lambda i: (i, 0, 0)),
                  pl.BlockSpec(mod.shape, lambda i: (0, 0)),
                  pl.BlockSpec((1, d), lambda i: (0, 0)),
                  pl.BlockSpec(w.shape, lambda i: (0, 0)),
                  pl.BlockSpec((1, w.shape[1]), lambda i: (0, 0))],
        out_specs=pl.BlockSpec((1, n, w.shape[1]), lambda i: (i, 0, 0)),
        out_shape=jax.ShapeDtypeStruct((bsz, n, w.shape[1]), F32),
        compiler_params=_cparams(("arbitrary",)),
        name="ctxproj",
    )(ctx, mod, g, w, b)


def _scan_kernel(xr_ref, prev_ref, next_ref, h0_ref, cw_ref, cb_ref, wg_ref, bg_ref, lam_ref, *rest,
                 reverse, emit_h, n_chunks):
    if emit_h:
        h_ref, hfin_ref, xp_ref, a_ref, b_ref, hst_ref = rest
    else:
        hfin_ref, xp_ref, a_ref, b_ref, hst_ref = rest
    bsz, tc, w = xr_ref.shape
    nblk = w // LANES
    i = pl.program_id(0)
    chunk = (n_chunks - 1 - i) if reverse else i

    @pl.when(i == 0)
    def _():
        hst_ref[...] = h0_ref[...]

    has_prev = chunk > 0
    has_next = chunk < n_chunks - 1
    z = -lam_ref[...]
    softplus = jnp.maximum(z, 0.0) + jnp.log1p(jnp.exp(-jnp.abs(z)))
    nls = -LRU_C * softplus

    for b in range(bsz):
        xp_ref[0:8, :] = jnp.where(has_prev, prev_ref[b], 0.0)
        xp_ref[8:8 + tc, :] = xr_ref[b]
        xp_ref[8 + tc:16 + tc, :] = jnp.where(has_next, next_ref[b], 0.0)
        xc = cb_ref[...] + cw_ref[0:1, :] * xp_ref[6:6 + tc, :]
        for k in range(1, LRU_CONV_K):
            xc = xc + cw_ref[k:k + 1, :] * xp_ref[6 + k:6 + k + tc, :]
        xcb = xc.astype(BF16)
        for p in range(LRU_HEADS // 2):
            cs = slice(2 * p * LANES, 2 * (p + 1) * LANES)
            g = _bdot(xcb[:, cs], wg_ref[p]) + bg_ref[p]
            r = _sigmoid(g[:, 0:2 * LANES])
            ig = _sigmoid(g[:, 2 * LANES:4 * LANES])
            log_a = nls[:, cs] * r
            a = jnp.exp(log_a)
            bb = jnp.sqrt(jnp.tanh(-log_a) * (1.0 + a * a)) * (ig * xc[:, cs])
            for jj in range(2):
                j = 2 * p + jj
                a_ref[b, pl.ds(j, tc, stride=nblk), :] = a[:, jj * LANES:(jj + 1) * LANES]
                b_ref[b, pl.ds(j, tc, stride=nblk), :] = bb[:, jj * LANES:(jj + 1) * LANES]

    def step(s, hs):
        t = (tc - 1 - s) if reverse else s
        row = pl.multiple_of(t * nblk, nblk)
        out = []
        for b in range(bsz):
            h = a_ref[b, pl.ds(row, nblk), :] * hs[b] + b_ref[b, pl.ds(row, nblk), :]
            b_ref[b, pl.ds(row, nblk), :] = h
            out.append(h)
        return tuple(out)

    hs = lax.fori_loop(0, tc, step, tuple(hst_ref[b] for b in range(bsz)), unroll=8)
    for b in range(bsz):
        hst_ref[b] = hs[b]
        hfin_ref[b] = hs[b]
    if emit_h:
        for b in range(bsz):
            for j in range(nblk):
                h_ref[b, :, j * LANES:(j + 1) * LANES] = b_ref[b, pl.ds(j, tc, stride=nblk), :]


def _scan(xr, h0, cw, cb, wg, bg, lam, *, reverse, emit_h):
    bsz, s, w = xr.shape
    tc = min(T_SCAN, s)
    assert s % tc == 0 and w == SUBLANES * LANES
    n_chunks = s // tc
    hb = tc // 8

    def cidx(i):
        return (n_chunks - 1 - i) if reverse else i

    in_specs = [
        pl.BlockSpec((bsz, tc, w), lambda i: (0, cidx(i), 0)),
        pl.BlockSpec((bsz, 8, w), lambda i: (0, jnp.maximum(cidx(i) * hb - 1, 0), 0)),
        pl.BlockSpec((bsz, 8, w), lambda i: (0, jnp.minimum((cidx(i) + 1) * hb, n_chunks * hb - 1), 0)),
        pl.BlockSpec((bsz, SUBLANES, LANES), lambda i: (0, 0, 0)),
        pl.BlockSpec(cw.shape, lambda i: (0, 0)),
        pl.BlockSpec(cb.shape, lambda i: (0, 0)),
        pl.BlockSpec(wg.shape, lambda i: (0, 0, 0)),
        pl.BlockSpec(bg.shape, lambda i: (0, 0, 0)),
        pl.BlockSpec(lam.shape, lambda i: (0, 0)),
    ]
    out_specs = [pl.BlockSpec((bsz, SUBLANES, LANES), lambda i: (0, 0, 0))]
    out_shape = [jax.ShapeDtypeStruct((bsz, SUBLANES, LANES), F32)]
    if emit_h:
        out_specs = [pl.BlockSpec((bsz, tc, w), lambda i: (0, cidx(i), 0))] + out_specs
        out_shape = [jax.ShapeDtypeStruct((bsz, s, w), F32)] + out_shape
    res = pl.pallas_call(
        functools.partial(_scan_kernel, reverse=reverse, emit_h=emit_h, n_chunks=n_chunks),
        grid=(n_chunks,),
        in_specs=in_specs,
        out_specs=out_specs,
        out_shape=out_shape,
        scratch_shapes=[pltpu.VMEM((tc + 16, w), F32),
                        pltpu.VMEM((bsz, tc * SUBLANES, LANES), F32),
                        pltpu.VMEM((bsz, tc * SUBLANES, LANES), F32),
                        pltpu.VMEM((bsz, SUBLANES, LANES), F32)],
        compiler_params=_cparams(("arbitrary",)),
        name="scan_bwd" if reverse else "scan_fwd",
    )(xr, xr, xr, h0, cw, cb, wg, bg, lam)
    return res if emit_h else (None, res[0])


def _k1_kernel(x_ref, mod_ref, gmix_ref, win_ref, bin_ref, cw_ref, cb_ref, lng_ref, lnb_ref, wpa_ref,
               ma_ref, gyg_ref, sgb_ref, xr_ref, zp_ref, co_ref):
    t, d = x_ref.shape[1], x_ref.shape[2]
    wc = cw_ref.shape[1]
    b = pl.program_id(0)
    xn = _rms_norm(x_ref[0], gmix_ref[...])
    sh1 = mod_ref[pl.ds(b, 1), 0:d]
    sc1 = mod_ref[pl.ds(b, 1), d:2 * d]
    hx = (xn * (1.0 + sc1) + sh1).astype(BF16)

    def proj(lo, hi):
        return _bdot(hx, win_ref[:, lo:hi]) + bin_ref[:, lo:hi]

    u = proj(0, wc)
    v = proj(wc, 2 * wc)
    zglu = u * _sigmoid(v)
    n_rows = t // GRID_W
    lo = CONV_PAD
    hi = CONV_PAD + GRID_W
    for g in range(n_rows):
        zp_ref[g, 0:lo, :] = jnp.zeros((lo, wc), F32)
        zp_ref[g, lo:hi, :] = zglu[g * GRID_W:(g + 1) * GRID_W, :]
        zp_ref[g, hi:hi + CONV_PAD, :] = jnp.zeros((CONV_PAD, wc), F32)

    def conv_row(g, carry):
        r0 = pl.multiple_of(g * GRID_W, GRID_W)
        for cblk in range(wc // LANES):
            cs = slice(cblk * LANES, (cblk + 1) * LANES)
            acc = jnp.broadcast_to(cb_ref[:, cs], (GRID_W, LANES))
            for k in range(CONV_K):
                off = lo - CONV_K // 2 + k
                acc = acc + cw_ref[k:k + 1, cs] * zp_ref[g, off:off + GRID_W, cs]
            co_ref[pl.ds(r0, GRID_W), cs] = acc
        return carry

    lax.fori_loop(0, n_rows, conv_row, 0)
    c = co_ref[...]
    mu = jnp.mean(c, axis=-1, keepdims=True)
    cc = c - mu
    y = cc * lax.rsqrt(jnp.mean(cc * cc, axis=-1, keepdims=True) + EPS) * lng_ref[...] + lnb_ref[...]
    a_lat = _bdot(_silu(y).astype(BF16), wpa_ref[...])

    o = 2 * wc
    gyg_ref[0] = _gelu_tanh(proj(o, o + d)).astype(BF16)
    xr_ref[0] = proj(o + d, o + 2 * d)
    ma_ref[0] = (_sigmoid(proj(o + 2 * d, o + 3 * d)) * a_lat).astype(BF16)
    sgb_ref[0] = _sigmoid(proj(o + 3 * d, o + 4 * d)).astype(BF16)


def _k1(x, mod, gmix, win, bin_, cw, cb, lng, lnb, wpa):
    bsz, s, d = x.shape
    t = T_TOK
    assert s % t == 0 and t % GRID_W == 0
    wc = cw.shape[1]

    def full(a):
        return pl.BlockSpec(a.shape, lambda b, i: (0,) * a.ndim)

    tok = lambda: pl.BlockSpec((1, t, d), lambda b, i: (b, i, 0))
    return pl.pallas_call(
        _k1_kernel,
        grid=(bsz, s // t),
        in_specs=[tok(), full(mod), full(gmix), full(win), full(bin_), full(cw), full(cb), full(lng), full(lnb),
                  full(wpa)],
        out_specs=[tok(), tok(), tok(), tok()],
        out_shape=[jax.ShapeDtypeStruct((bsz, s, d), BF16),
                   jax.ShapeDtypeStruct((bsz, s, d), BF16),
                   jax.ShapeDtypeStruct((bsz, s, d), BF16),
                   jax.ShapeDtypeStruct((bsz, s, d), F32)],
        scratch_shapes=[pltpu.VMEM((t // GRID_W, GRID_W + 2 * CONV_PAD, wc), F32),
                        pltpu.VMEM((t, wc), F32)],
        compiler_params=_cparams(("arbitrary", "arbitrary")),
        name="k1",
    )(x, mod, gmix, win, bin_, cw, cb, lng, lnb, wpa)


def _post_kernel(x_ref, hf_ref, hb_ref, gyg_ref, ma_ref, sgb_ref, mod_ref, wpb_ref, wo_ref, gffn_ref,
                 wrt_ref, brt_ref, x1_ref, hm_ref, rti_ref, rtf_ref):
    d = x_ref.shape[2]
    b = pl.program_id(0)
    y_rec = hf_ref[0] + hb_ref[0]
    b_lat = _bdot((gyg_ref[0].astype(F32) * y_rec).astype(BF16), wpb_ref[...])
    merged = ma_ref[0].astype(F32) + sgb_ref[0].astype(F32) * b_lat
    res = _bdot(merged.astype(BF16), wo_ref[...])
    gt1 = mod_ref[pl.ds(b, 1), 2 * d:3 * d]
    x1 = x_ref[0] + gt1 * res
    x1_ref[0] = x1
    sh2 = mod_ref[pl.ds(b, 1), 3 * d:4 * d]
    sc2 = mod_ref[pl.ds(b, 1), 4 * d:5 * d]
    hm = _rms_norm(x1, gffn_ref[...]) * (1.0 + sc2) + sh2
    hm_ref[0] = hm.astype(BF16)

    lg = lax.dot_general(wrt_ref[...], hm, (((1,), (1,)), ((), ())), precision=HIGHEST,
                         preferred_element_type=F32) + brt_ref[...]
    gl = [lg[k:k + 1, :] for k in range(N_GROUPS)]
    gmax = jnp.maximum(jnp.maximum(gl[0], gl[1]), jnp.maximum(gl[2], gl[3]))
    gsel = jnp.where(gl[0] == gmax, 0, jnp.where(gl[1] == gmax, 1, jnp.where(gl[2] == gmax, 2, 3)))
    p_grp = 1.0 / (jnp.exp(gl[0] - gmax) + jnp.exp(gl[1] - gmax) + jnp.exp(gl[2] - gmax) + jnp.exp(gl[3] - gmax))
    el = []
    for j in range(EXPERTS_PER_GROUP):
        rows = [lg[N_GROUPS + g * EXPERTS_PER_GROUP + j:N_GROUPS + g * EXPERTS_PER_GROUP + j + 1, :]
                for g in range(N_GROUPS)]
        el.append(jnp.where(gsel == 0, rows[0], jnp.where(gsel == 1, rows[1], jnp.where(gsel == 2, rows[2], rows[3]))))
    m1 = jnp.maximum(jnp.maximum(el[0], el[1]), jnp.maximum(el[2], el[3]))
    i1 = jnp.where(el[0] == m1, 0, jnp.where(el[1] == m1, 1, jnp.where(el[2] == m1, 2, 3)))
    neg = jnp.float32(-jnp.inf)
    rest = [jnp.where(i1 == j, neg, el[j]) for j in range(EXPERTS_PER_GROUP)]
    m2 = jnp.maximum(jnp.maximum(rest[0], rest[1]), jnp.maximum(rest[2], rest[3]))
    i2 = jnp.where((rest[0] == m2) & (i1 != 0), 0,
                   jnp.where((rest[1] == m2) & (i1 != 1), 1, jnp.where((rest[2] == m2) & (i1 != 2), 2, 3)))
    e21 = jnp.exp(m2 - m1)
    den = 1.0 / (1.0 + e21)
    w1 = p_grp * den
    w2 = p_grp * (e21 * den)
    first_lo = i1 < i2
    lo = jnp.where(first_lo, i1, i2)
    hi = jnp.where(first_lo, i2, i1)
    w_lo = jnp.where(first_lo, w1, w2)
    w_hi = jnp.where(first_lo, w2, w1)
    pair = jnp.where(lo == 0, hi - 1, jnp.where(lo == 1, hi + 1, 5))
    bucket = gsel * N_PAIRS + pair
    t = bucket.shape[1]
    rti_ref[...] = jnp.concatenate([bucket.astype(I32), jnp.zeros((7, t), I32)], axis=0)
    rtf_ref[...] = jnp.concatenate([w_lo, w_hi, jnp.zeros((6, t), F32)], axis=0)


def _post(x, hf, hb, gyg, ma, sgb, mod, wpb, wo, gffn, wrt, brt):
    bsz, s, d = x.shape
    t = T_TOK
    nt = s // t
    n = bsz * s

    def full(a):
        return pl.BlockSpec(a.shape, lambda b, i: (0,) * a.ndim)

    tok = lambda: pl.BlockSpec((1, t, d), lambda b, i: (b, i, 0))
    rt = lambda: pl.BlockSpec((8, t), lambda b, i: (0, b * nt + i))
    return pl.pallas_call(
        _post_kernel,
        grid=(bsz, nt),
        in_specs=[tok(), tok(), tok(), tok(), tok(), tok(), full(mod), full(wpb), full(wo), full(gffn), full(wrt),
                  full(brt)],
        out_specs=[tok(), tok(), rt(), rt()],
        out_shape=[jax.ShapeDtypeStruct((bsz, s, d), F32),
                   jax.ShapeDtypeStruct((bsz, s, d), BF16),
                   jax.ShapeDtypeStruct((8, n), I32),
                   jax.ShapeDtypeStruct((8, n), F32)],
        compiler_params=_cparams(("arbitrary", "arbitrary")),
        name="post",
    )(x, hf, hb, gyg, ma, sgb, mod, wpb, wo, gffn, wrt, brt)


def _sort_kernel(bk_ref, slot_ref, meta_ref, *, tm):
    bk = bk_ref[...]
    r = bk.shape[0]
    nt_lanes = meta_ref.shape[1]
    shift = tm.bit_length() - 1
    upper = (lax.broadcasted_iota(I32, (LANES, LANES), 0) < lax.broadcasted_iota(I32, (LANES, LANES), 1))
    upper = upper.astype(F32).astype(BF16)
    lower = (lax.broadcasted_iota(I32, (r, r), 1) < lax.broadcasted_iota(I32, (r, r), 0))
    lower = lower.astype(F32).astype(BF16)
    ones = jnp.ones((LANES, LANES), BF16)
    tile_pos = lax.broadcasted_iota(I32, (1, nt_lanes), 1) * tm
    start = jnp.zeros((1, LANES), I32)
    start_t = jnp.zeros((1, nt_lanes), I32)
    slot = jnp.zeros((r, LANES), I32)
    tile_bucket = jnp.zeros((1, nt_lanes), I32)
    for k in range(N_BUCKETS):
        m = bk == k
        mb = m.astype(F32).astype(BF16)
        pre = _bdot(mb, upper)
        tot = _bdot(mb, ones)
        rowpre = _bdot(lower, tot.astype(BF16))
        rank = (pre + rowpre).astype(I32)
        cnt = (rowpre[r - 1:r, :] + tot[r - 1:r, :]).astype(I32)
        slot = jnp.where(m, start + rank, slot)
        padded = ((cnt + (tm - 1)) >> shift) << shift
        start = start + padded
        start_t = start_t + jnp.concatenate([padded] * (nt_lanes // LANES), axis=1)
        tile_bucket = tile_bucket + (tile_pos >= start_t).astype(I32)
    slot_ref[...] = slot
    valid = (tile_pos < start_t).astype(I32)
    tb = jnp.minimum(tile_bucket, N_BUCKETS - 1)
    g = (tb * 43) >> 8
    pair = tb - N_PAIRS * g
    ge3 = (pair >= 3).astype(I32)
    ge5 = (pair >= 5).astype(I32)
    lo = ge3 + ge5
    hi = pair + 1 - 2 * ge3 - ge5
    e_lo = g * EXPERTS_PER_GROUP + lo
    e_hi = g * EXPERTS_PER_GROUP + hi
    meta_ref[...] = jnp.concatenate([e_lo, e_hi, valid, jnp.zeros((5, nt_lanes), I32)], axis=0)


def _sort(bucket2d, tm, nt_lanes):
    r = bucket2d.shape[0]
    return pl.pallas_call(
        functools.partial(_sort_kernel, tm=tm),
        grid=(1,),
        in_specs=[pl.BlockSpec((r, LANES), lambda i: (0, 0))],
        out_specs=[pl.BlockSpec((r, LANES), lambda i: (0, 0)),
                   pl.BlockSpec((8, nt_lanes), lambda i: (0, 0))],
        out_shape=[jax.ShapeDtypeStruct((r, LANES), I32),
                   jax.ShapeDtypeStruct((8, nt_lanes), I32)],
        compiler_params=_cparams(("arbitrary",)),
        name="bucket_sort",
    )(bucket2d)


def _moe_kernel(elo_ref, ehi_ref, valid_ref, h_ref, ws_ref, wg_lo, wg_hi, wu_lo, wu_hi, wd_lo, wd_hi, o_ref):
    i = pl.program_id(0)

    @pl.when(valid_ref[i] != 0)
    def _():
        h = h_ref[...]
        act_lo = _silu(_bdot(h, wg_lo[0])) * _bdot(h, wu_lo[0]) * ws_ref[:, 0:1]
        act_hi = _silu(_bdot(h, wg_hi[0])) * _bdot(h, wu_hi[0]) * ws_ref[:, 1:2]
        o_ref[...] = _bdot(act_lo.astype(BF16), wd_lo[0]) + _bdot(act_hi.astype(BF16), wd_hi[0])

    @pl.when(valid_ref[i] == 0)
    def _():
        o_ref[...] = jnp.zeros(o_ref.shape, F32)


def _moe(e_lo, e_hi, valid, hs, ws, wgate, wup, wdown):
    npad, d = hs.shape
    tm = T_MOE
    n_tiles = npad // tm
    ne, _, f = wgate.shape
    lo3 = lambda i, elo, ehi, va: (elo[i], 0, 0)
    hi3 = lambda i, elo, ehi, va: (ehi[i], 0, 0)
    row = lambda i, elo, ehi, va: (i, 0)
    grid_spec = pltpu.PrefetchScalarGridSpec(
        num_scalar_prefetch=3,
        grid=(n_tiles,),
        in_specs=[pl.BlockSpec((tm, d), row),
                  pl.BlockSpec((tm, 2), row),
                  pl.BlockSpec((1, d, f), lo3), pl.BlockSpec((1, d, f), hi3),
                  pl.BlockSpec((1, d, f), lo3), pl.BlockSpec((1, d, f), hi3),
                  pl.BlockSpec((1, f, d), lo3), pl.BlockSpec((1, f, d), hi3)],
        out_specs=pl.BlockSpec((tm, d), row),
    )
    return pl.pallas_call(
        _moe_kernel,
        grid_spec=grid_spec,
        out_shape=jax.ShapeDtypeStruct((npad, d), F32),
        compiler_params=_cparams(("arbitrary",)),
        name="moe",
    )(e_lo, e_hi, valid, hs, ws, wgate, wgate, wup, wup, wdown, wdown)


def _final_kernel(x1_ref, mo_ref, mod_ref, g_ref, o_ref):
    d = x1_ref.shape[2]
    b = pl.program_id(0)
    gt2 = mod_ref[pl.ds(b, 1), 5 * d:6 * d]
    o_ref[0] = _rms_norm(x1_ref[0] + gt2 * mo_ref[0], g_ref[...])


def _final(x1, mo, mod, g):
    bsz, s, d = x1.shape
    t = T_TOK
    tok = lambda: pl.BlockSpec((1, t, d), lambda b, i: (b, i, 0))
    return pl.pallas_call(
        _final_kernel,
        grid=(bsz, s // t),
        in_specs=[tok(), tok(), pl.BlockSpec(mod.shape, lambda b, i: (0, 0)), pl.BlockSpec((1, d), lambda b, i: (0, 0))],
        out_specs=tok(),
        out_shape=jax.ShapeDtypeStruct((bsz, s, d), F32),
        compiler_params=_cparams(("arbitrary", "arbitrary")),
        name="final",
    )(x1, mo, mod, g)


def _pair_blockdiag(w):
    h, hd, _ = w.shape
    w2 = w.reshape(h // 2, 2, hd, hd)
    z = jnp.zeros((h // 2, hd, hd), w.dtype)
    top = jnp.concatenate([w2[:, 0], z], axis=2)
    bot = jnp.concatenate([z, w2[:, 1]], axis=2)
    return jnp.concatenate([top, bot], axis=1)


def _gate_params(w_r, b_r, w_i, b_i):
    wg = jnp.concatenate([_pair_blockdiag(w_r), _pair_blockdiag(w_i)], axis=2).astype(BF16)
    h, hd = b_r.shape
    bg = jnp.concatenate([b_r.reshape(h // 2, 1, 2 * hd), b_i.reshape(h // 2, 1, 2 * hd)], axis=2)
    return wg, bg


def kernel(x, c, ctx, c_ctx, w_ada, b_ada, g_mix, w_in, b_in, conv_w, conv_b, ln_g, ln_b, w_pa, lru_conv_w, lru_conv_b, w_r_f, b_r_f, w_i_f, b_i_f, lam_f, w_r_b, b_r_b, w_i_b, b_i_b, lam_b, w_pb, w_o, g_ffn, w_grp, b_grp, w_er, b_er, w_gate, w_up, w_down, g_final):
    bsz, s, d = x.shape
    depth = w_ada.shape[0]
    assert depth == 1 and bsz + 1 <= 8
    l = 0
    n = bsz * s
    wc = conv_w.shape[2]
    wl = lru_conv_w.shape[2]
    rec0 = 2 * wc + wl
    ctx_row = bsz

    c8 = jnp.concatenate([c, c_ctx[None, :], jnp.zeros((8 - bsz - 1, d), F32)], axis=0)
    mod = _ada(c8, w_ada[l], b_ada[l][None, :])

    win = w_in[l].astype(BF16)
    bin_ = b_in[l][None, :]
    gmix = g_mix[l][None, :]
    lcw = lru_conv_w[l]
    lcb = lru_conv_b[l][None, :]
    wg_f, bg_f = _gate_params(w_r_f[l], b_r_f[l], w_i_f[l], b_i_f[l])
    wg_b, bg_b = _gate_params(w_r_b[l], b_r_b[l], w_i_b[l], b_i_b[l])
    lamf = lam_f[l][None, :]
    lamb = lam_b[l][None, :]

    xr_c = _ctxproj(ctx, mod, gmix, win[:, rec0:rec0 + wl], bin_[:, rec0:rec0 + wl], ctx_row)
    zero_state = jnp.zeros((bsz, SUBLANES, LANES), F32)
    _, h0f = _scan(xr_c, zero_state, lcw, lcb, wg_f, bg_f, lamf, reverse=False, emit_h=False)
    _, h0b = _scan(xr_c, zero_state, lcw, lcb, wg_b, bg_b, lamb, reverse=True, emit_h=False)

    ma, gyg, sgb, xr = _k1(x, mod, gmix, win, bin_, conv_w[l], conv_b[l][None, :], ln_g[l][None, :],
                           ln_b[l][None, :], w_pa[l].astype(BF16))
    hf, _ = _scan(xr, h0f, lcw, lcb, wg_f, bg_f, lamf, reverse=False, emit_h=True)
    hb, _ = _scan(xr, h0b, lcw, lcb, wg_b, bg_b, lamb, reverse=True, emit_h=True)

    n_rt = 32
    wrt = jnp.concatenate([w_grp[l].T, w_er[l].T, jnp.zeros((n_rt - N_GROUPS * (1 + EXPERTS_PER_GROUP), d), F32)], axis=0)
    brt = jnp.concatenate([b_grp[l], b_er[l], jnp.zeros((n_rt - N_GROUPS * (1 + EXPERTS_PER_GROUP),), F32)])
    brt = jnp.broadcast_to(brt[:, None], (n_rt, T_TOK))
    x1, hm, rti, rtf = _post(x, hf, hb, gyg, ma, sgb, mod, w_pb[l].astype(BF16), w_o[l].astype(BF16),
                             g_ffn[l][None, :], wrt, brt)

    tm = T_MOE
    n_tiles = n // tm + N_BUCKETS
    npad = n_tiles * tm
    nt_lanes = ((n_tiles + LANES - 1) // LANES) * LANES
    slot2d, meta = _sort(rti[0].reshape(n // LANES, LANES), tm, nt_lanes)
    slot = slot2d.reshape(n)
    tok_of_slot = jnp.zeros((npad,), I32).at[slot].set(jnp.arange(n, dtype=I32))
    row_valid = jnp.zeros((npad,), F32).at[slot].set(1.0)
    hs = jnp.take(hm.reshape(n, d), tok_of_slot, axis=0)
    ws = jnp.take(rtf[0:2].T, tok_of_slot, axis=0) * row_valid[:, None]
    mo_sorted = _moe(meta[0, :n_tiles], meta[1, :n_tiles], meta[2, :n_tiles], hs, ws,
                     w_gate[l].astype(BF16), w_up[l].astype(BF16), w_down[l].astype(BF16))
    mo = jnp.take(mo_sorted, slot, axis=0).reshape(bsz, s, d)

    return _final(x1, mo, mod, g_final[None, :])
```

```python
import functools

import jax
import jax.numpy as jnp
from jax import lax
from jax.experimental import pallas as pl
from jax.experimental.pallas import tpu as pltpu
from jax.experimental.pallas import tpu_sc as plsc

F32 = jnp.float32
BF16 = jnp.bfloat16
I32 = jnp.int32
U32 = jnp.uint32
HIGHEST = lax.Precision.HIGHEST

EPS = 1e-6
GRID_W = 64
CONV_K = 31
CONV_PAD = 16
LRU_CONV_K = 4
LRU_C = 8.0
LRU_HEADS = 8
N_GROUPS = 4
EXPERTS_PER_GROUP = 4
N_PAIRS = 6
N_BUCKETS = N_GROUPS * N_PAIRS
LANES = 128
SUBLANES = 8
VMEM_LIMIT = 56 * 1024 * 1024

T_TOK = 256
T_SCAN = 256
T_MOE = 256


def _sigmoid(x):
    return 0.5 * (jnp.tanh(0.5 * x) + 1.0)


def _silu(x):
    return x * _sigmoid(x)


def _gelu_tanh(x):
    return 0.5 * x * (1.0 + jnp.tanh(0.7978845608028654 * (x + 0.044715 * (x * x * x))))


def _rms_norm(x, g):
    return x * lax.rsqrt(jnp.mean(x * x, axis=-1, keepdims=True) + EPS) * g


def _bdot(a, b):
    return jnp.dot(a, b, preferred_element_type=F32)


def _pack_bf16_pair(lo, hi):
    lo_bits = lax.bitcast_convert_type(lo.astype(BF16).astype(F32), U32)
    hi_bits = lax.bitcast_convert_type(hi.astype(BF16).astype(F32), U32)
    return (hi_bits & jnp.uint32(0xFFFF0000)) | (lo_bits >> 16)


def _unpack_bf16_pair(u):
    lo = lax.bitcast_convert_type(u << 16, F32)
    hi = lax.bitcast_convert_type(u & jnp.uint32(0xFFFF0000), F32)
    return lo, hi


def _cparams(sem):
    return pltpu.CompilerParams(dimension_semantics=sem, vmem_limit_bytes=VMEM_LIMIT)


def _ada_kernel(c_ref, w_ref, b_ref, o_ref):
    s = _silu(c_ref[...])
    o_ref[...] = jnp.dot(s, w_ref[...], precision=HIGHEST, preferred_element_type=F32) + b_ref[...]


def _ada(c8, w, b):
    d, n = w.shape
    blk = 1024
    return pl.pallas_call(
        _ada_kernel,
        grid=(n // blk,),
        in_specs=[pl.BlockSpec((8, d), lambda j: (0, 0)),
                  pl.BlockSpec((d, blk), lambda j: (0, j)),
                  pl.BlockSpec((1, blk), lambda j: (0, j))],
        out_specs=pl.BlockSpec((8, blk), lambda j: (0, j)),
        out_shape=jax.ShapeDtypeStruct((8, n), F32),
        compiler_params=_cparams(("arbitrary",)),
        name="ada",
    )(c8, w, b)


def _ctxproj_kernel(x_ref, mod_ref, g_ref, w_ref, b_ref, o_ref, *, row):
    d = x_ref.shape[-1]
    xn = _rms_norm(x_ref[0], g_ref[...])
    sh = mod_ref[row:row + 1, 0:d]
    sc = mod_ref[row:row + 1, d:2 * d]
    hx = (xn * (1.0 + sc) + sh).astype(BF16)
    o_ref[0] = _bdot(hx, w_ref[...]) + b_ref[...]


def _ctxproj(ctx, mod, g, w, b, row):
    bsz, n, d = ctx.shape
    return pl.pallas_call(
        functools.partial(_ctxproj_kernel, row=row),
        grid=(bsz,),
        in_specs=[pl.BlockSpec((1, n, d), lambda i: (i, 0, 0)),
                  pl.BlockSpec(mod.shape, lambda i: (0, 0)),
                  pl.BlockSpec((1, d), lambda i: (0, 0)),
                  pl.BlockSpec(w.shape, lambda i: (0, 0)),
                  pl.BlockSpec((1, w.shape[1]), lambda i: (0, 0))],
        out_specs=pl.BlockSpec((1, n, w.shape[1]), lambda i: (i, 0, 0)),
        out_shape=jax.ShapeDtypeStruct((bsz, n, w.shape[1]), F32),
        compiler_params=_cparams(("arbitrary",)),
        name="ctxproj",
    )(ctx, mod, g, w, b)


def _scan_kernel(xr_ref, prev_ref, next_ref, h0_ref, cw_ref, cb_ref, wg_ref, bg_ref, lam_ref, *rest,
                 reverse, emit_h, n_chunks):
    if emit_h:
        h_ref, hfin_ref, xp_ref, a_ref, b_ref, hst_ref = rest
    else:
        hfin_ref, xp_ref, a_ref, b_ref, hst_ref = rest
    bsz, tc, w = xr_ref.shape
    nblk = w // LANES
    i = pl.program_id(0)
    chunk = (n_chunks - 1 - i) if reverse else i

    @pl.when(i == 0)
    def _():
        hst_ref[...] = h0_ref[...]

    has_prev = chunk > 0
    has_next = chunk < n_chunks - 1
    z = -lam_ref[...]
    softplus = jnp.maximum(z, 0.0) + jnp.log1p(jnp.exp(-jnp.abs(z)))
    nls = -LRU_C * softplus

    for b in range(bsz):
        xp_ref[0:8, :] = jnp.where(has_prev, prev_ref[b], 0.0)
        xp_ref[8:8 + tc, :] = xr_ref[b]
        xp_ref[8 + tc:16 + tc, :] = jnp.where(has_next, next_ref[b], 0.0)
        xc = cb_ref[...] + cw_ref[0:1, :] * xp_ref[6:6 + tc, :]
        for k in range(1, LRU_CONV_K):
            xc = xc + cw_ref[k:k + 1, :] * xp_ref[6 + k:6 + k + tc, :]
        xcb = xc.astype(BF16)
        for p in range(LRU_HEADS // 2):
            cs = slice(2 * p * LANES, 2 * (p + 1) * LANES)
            g = _bdot(xcb[:, cs], wg_ref[p]) + bg_ref[p]
            r = _sigmoid(g[:, 0:2 * LANES])
            ig = _sigmoid(g[:, 2 * LANES:4 * LANES])
            log_a = nls[:, cs] * r
            a = jnp.exp(log_a)
            bb = jnp.sqrt(jnp.tanh(-log_a) * (1.0 + a * a)) * (ig * xc[:, cs])
            for jj in range(2):
                j = 2 * p + jj
                a_ref[b, pl.ds(j, tc, stride=nblk), :] = a[:, jj * LANES:(jj + 1) * LANES]
                b_ref[b, pl.ds(j, tc, stride=nblk), :] = bb[:, jj * LANES:(jj + 1) * LANES]

    def step(s, hs):
        t = (tc - 1 - s) if reverse else s
        row = pl.multiple_of(t * nblk, nblk)
        out = []
        for b in range(bsz):
            h = a_ref[b, pl.ds(row, nblk), :] * hs[b] + b_ref[b, pl.ds(row, nblk), :]
            b_ref[b, pl.ds(row, nblk), :] = h
            out.append(h)
        return tuple(out)

    hs = lax.fori_loop(0, tc, step, tuple(hst_ref[b] for b in range(bsz)), unroll=8)
    for b in range(bsz):
        hst_ref[b] = hs[b]
        hfin_ref[b] = hs[b]
    if emit_h:
        for b in range(bsz):
            for j in range(nblk):
                h_ref[b, :, j * LANES:(j + 1) * LANES] = b_ref[b, pl.ds(j, tc, stride=nblk), :]


def _scan(xr, h0, cw, cb, wg, bg, lam, *, reverse, emit_h):
    bsz, s, w = xr.shape
    tc = min(T_SCAN, s)
    assert s % tc == 0 and w == SUBLANES * LANES
    n_chunks = s // tc
    hb = tc // 8

    def cidx(i):
        return (n_chunks - 1 - i) if reverse else i

    in_specs = [
        pl.BlockSpec((bsz, tc, w), lambda i: (0, cidx(i), 0)),
        pl.BlockSpec((bsz, 8, w), lambda i: (0, jnp.maximum(cidx(i) * hb - 1, 0), 0)),
        pl.BlockSpec((bsz, 8, w), lambda i: (0, jnp.minimum((cidx(i) + 1) * hb, n_chunks * hb - 1), 0)),
        pl.BlockSpec((bsz, SUBLANES, LANES), lambda i: (0, 0, 0)),
        pl.BlockSpec(cw.shape, lambda i: (0, 0)),
        pl.BlockSpec(cb.shape, lambda i: (0, 0)),
        pl.BlockSpec(wg.shape, lambda i: (0, 0, 0)),
        pl.BlockSpec(bg.shape, lambda i: (0, 0, 0)),
        pl.BlockSpec(lam.shape, lambda i: (0, 0)),
    ]
    out_specs = [pl.BlockSpec((bsz, SUBLANES, LANES), lambda i: (0, 0, 0))]
    out_shape = [jax.ShapeDtypeStruct((bsz, SUBLANES, LANES), F32)]
    if emit_h:
        out_specs = [pl.BlockSpec((bsz, tc, w), lambda i: (0, cidx(i), 0))] + out_specs
        out_shape = [jax.ShapeDtypeStruct((bsz, s, w), F32)] + out_shape
    res = pl.pallas_call(
        functools.partial(_scan_kernel, reverse=reverse, emit_h=emit_h, n_chunks=n_chunks),
        grid=(n_chunks,),
        in_specs=in_specs,
        out_specs=out_specs,
        out_shape=out_shape,
        scratch_shapes=[pltpu.VMEM((tc + 16, w), F32),
                        pltpu.VMEM((bsz, tc * SUBLANES, LANES), F32),
                        pltpu.VMEM((bsz, tc * SUBLANES, LANES), F32),
                        pltpu.VMEM((bsz, SUBLANES, LANES), F32)],
        compiler_params=_cparams(("arbitrary",)),
        name="scan_bwd" if reverse else "scan_fwd",
    )(xr, xr, xr, h0, cw, cb, wg, bg, lam)
    return res if emit_h else (None, res[0])


def _k1_kernel(x_ref, mod_ref, gmix_ref, win_ref, bin_ref, cw_ref, cb_ref, lng_ref, lnb_ref, wpa_ref,
               ma_ref, gyg_ref, sgb_ref, xr_ref, zp_ref, co_ref):
    t, d = x_ref.shape[1], x_ref.shape[2]
    wc = cw_ref.shape[1]
    b = pl.program_id(0)
    xn = _rms_norm(x_ref[0], gmix_ref[...])
    sh1 = mod_ref[pl.ds(b, 1), 0:d]
    sc1 = mod_ref[pl.ds(b, 1), d:2 * d]
    hx = (xn * (1.0 + sc1) + sh1).astype(BF16)

    def proj(lo, hi):
        return _bdot(hx, win_ref[:, lo:hi]) + bin_ref[:, lo:hi]

    u = proj(0, wc)
    v = proj(wc, 2 * wc)
    zglu = u * _sigmoid(v)
    n_rows = t // GRID_W
    lo = CONV_PAD
    hi = CONV_PAD + GRID_W
    for g in range(n_rows):
        zp_ref[g, 0:lo, :] = jnp.zeros((lo, wc), F32)
        zp_ref[g, lo:hi, :] = zglu[g * GRID_W:(g + 1) * GRID_W, :]
        zp_ref[g, hi:hi + CONV_PAD, :] = jnp.zeros((CONV_PAD, wc), F32)

    def conv_row(g, carry):
        r0 = pl.multiple_of(g * GRID_W, GRID_W)
        for cblk in range(wc // LANES):
            cs = slice(cblk * LANES, (cblk + 1) * LANES)
            acc = jnp.broadcast_to(cb_ref[:, cs], (GRID_W, LANES))
            for k in range(CONV_K):
                off = lo - CONV_K // 2 + k
                acc = acc + cw_ref[k:k + 1, cs] * zp_ref[g, off:off + GRID_W, cs]
            co_ref[pl.ds(r0, GRID_W), cs] = acc
        return carry

    lax.fori_loop(0, n_rows, conv_row, 0)
    c = co_ref[...]
    mu = jnp.mean(c, axis=-1, keepdims=True)
    cc = c - mu
    y = cc * lax.rsqrt(jnp.mean(cc * cc, axis=-1, keepdims=True) + EPS) * lng_ref[...] + lnb_ref[...]
    a_lat = _bdot(_silu(y).astype(BF16), wpa_ref[...])

    o = 2 * wc
    gyg_ref[0] = _gelu_tanh(proj(o, o + d)).astype(BF16)
    xr_ref[0] = proj(o + d, o + 2 * d)
    ma_ref[0] = (_sigmoid(proj(o + 2 * d, o + 3 * d)) * a_lat).astype(BF16)
    sgb_ref[0] = _sigmoid(proj(o + 3 * d, o + 4 * d)).astype(BF16)


def _k1(x, mod, gmix, win, bin_, cw, cb, lng, lnb, wpa):
    bsz, s, d = x.shape
    t = T_TOK
    assert s % t == 0 and t % GRID_W == 0
    wc = cw.shape[1]

    def full(a):
        return pl.BlockSpec(a.shape, lambda b, i: (0,) * a.ndim)

    tok = lambda: pl.BlockSpec((1, t, d), lambda b, i: (b, i, 0))
    return pl.pallas_call(
        _k1_kernel,
        grid=(bsz, s // t),
        in_specs=[tok(), full(mod), full(gmix), full(win), full(bin_), full(cw), full(cb), full(lng), full(lnb),
                  full(wpa)],
        out_specs=[tok(), tok(), tok(), tok()],
        out_shape=[jax.ShapeDtypeStruct((bsz, s, d), BF16),
                   jax.ShapeDtypeStruct((bsz, s, d), BF16),
                   jax.ShapeDtypeStruct((bsz, s, d), BF16),
                   jax.ShapeDtypeStruct((bsz, s, d), F32)],
        scratch_shapes=[pltpu.VMEM((t // GRID_W, GRID_W + 2 * CONV_PAD, wc), F32),
                        pltpu.VMEM((t, wc), F32)],
        compiler_params=_cparams(("arbitrary", "arbitrary")),
        name="k1",
    )(x, mod, gmix, win, bin_, cw, cb, lng, lnb, wpa)


def _post_kernel(x_ref, hf_ref, hb_ref, gyg_ref, ma_ref, sgb_ref, mod_ref, wpb_ref, wo_ref, gffn_ref,
                 wrt_ref, brt_ref, x1_ref, hx_ref, rti_ref):
    d = x_ref.shape[2]
    b = pl.program_id(0)
    y_rec = hf_ref[0] + hb_ref[0]
    b_lat = _bdot((gyg_ref[0].astype(F32) * y_rec).astype(BF16), wpb_ref[...])
    merged = ma_ref[0].astype(F32) + sgb_ref[0].astype(F32) * b_lat
    res = _bdot(merged.astype(BF16), wo_ref[...])
    gt1 = mod_ref[pl.ds(b, 1), 2 * d:3 * d]
    x1 = x_ref[0] + gt1 * res
    x1_ref[0] = x1
    sh2 = mod_ref[pl.ds(b, 1), 3 * d:4 * d]
    sc2 = mod_ref[pl.ds(b, 1), 4 * d:5 * d]
    hm = _rms_norm(x1, gffn_ref[...]) * (1.0 + sc2) + sh2
    half = d // 2
    hx_ref[0, :, 0:half] = _pack_bf16_pair(hm[:, 0:half], hm[:, half:d])

    lg = lax.dot_general(wrt_ref[...], hm, (((1,), (1,)), ((), ())), precision=HIGHEST,
                         preferred_element_type=F32) + brt_ref[...]
    gl = [lg[k:k + 1, :] for k in range(N_GROUPS)]
    gmax = jnp.maximum(jnp.maximum(gl[0], gl[1]), jnp.maximum(gl[2], gl[3]))
    gsel = jnp.where(gl[0] == gmax, 0, jnp.where(gl[1] == gmax, 1, jnp.where(gl[2] == gmax, 2, 3)))
    p_grp = 1.0 / (jnp.exp(gl[0] - gmax) + jnp.exp(gl[1] - gmax) + jnp.exp(gl[2] - gmax) + jnp.exp(gl[3] - gmax))
    el = []
    for j in range(EXPERTS_PER_GROUP):
        rows = [lg[N_GROUPS + g * EXPERTS_PER_GROUP + j:N_GROUPS + g * EXPERTS_PER_GROUP + j + 1, :]
                for g in range(N_GROUPS)]
        el.append(jnp.where(gsel == 0, rows[0], jnp.where(gsel == 1, rows[1], jnp.where(gsel == 2, rows[2], rows[3]))))
    m1 = jnp.maximum(jnp.maximum(el[0], el[1]), jnp.maximum(el[2], el[3]))
    i1 = jnp.where(el[0] == m1, 0, jnp.where(el[1] == m1, 1, jnp.where(el[2] == m1, 2, 3)))
    neg = jnp.float32(-jnp.inf)
    rest = [jnp.where(i1 == j, neg, el[j]) for j in range(EXPERTS_PER_GROUP)]
    m2 = jnp.maximum(jnp.maximum(rest[0], rest[1]), jnp.maximum(rest[2], rest[3]))
    i2 = jnp.where((rest[0] == m2) & (i1 != 0), 0,
                   jnp.where((rest[1] == m2) & (i1 != 1), 1, jnp.where((rest[2] == m2) & (i1 != 2), 2, 3)))
    e21 = jnp.exp(m2 - m1)
    den = 1.0 / (1.0 + e21)
    w1 = p_grp * den
    w2 = p_grp * (e21 * den)
    first_lo = i1 < i2
    lo = jnp.where(first_lo, i1, i2)
    hi = jnp.where(first_lo, i2, i1)
    w_lo = jnp.where(first_lo, w1, w2)
    w_hi = jnp.where(first_lo, w2, w1)
    pair = jnp.where(lo == 0, hi - 1, jnp.where(lo == 1, hi + 1, 5))
    bucket = gsel * N_PAIRS + pair
    t = bucket.shape[1]
    rti_ref[...] = jnp.concatenate([bucket.astype(I32), jnp.zeros((7, t), I32)], axis=0)
    wrows = jnp.concatenate([w_lo, w_hi, jnp.zeros((LANES - 2, t), F32)], axis=0)
    eye = (lax.broadcasted_iota(I32, (t, t), 0) == lax.broadcasted_iota(I32, (t, t), 1)).astype(F32)
    wcols = lax.dot_general(eye, wrows, (((1,), (1,)), ((), ())), precision=HIGHEST, preferred_element_type=F32)
    hx_ref[0, :, half:half + LANES] = lax.bitcast_convert_type(wcols, U32)


def _post(x, hf, hb, gyg, ma, sgb, mod, wpb, wo, gffn, wrt, brt):
    bsz, s, d = x.shape
    t = T_TOK
    nt = s // t
    n = bsz * s

    def full(a):
        return pl.BlockSpec(a.shape, lambda b, i: (0,) * a.ndim)

    tok = lambda: pl.BlockSpec((1, t, d), lambda b, i: (b, i, 0))
    rt = lambda: pl.BlockSpec((8, t), lambda b, i: (0, b * nt + i))
    return pl.pallas_call(
        _post_kernel,
        grid=(bsz, nt),
        in_specs=[tok(), tok(), tok(), tok(), tok(), tok(), full(mod), full(wpb), full(wo), full(gffn), full(wrt),
                  full(brt)],
        out_specs=[tok(), pl.BlockSpec((1, t, d // 2 + LANES), lambda b, i: (b, i, 0)), rt()],
        out_shape=[jax.ShapeDtypeStruct((bsz, s, d), F32),
                   jax.ShapeDtypeStruct((bsz, s, d // 2 + LANES), U32),
                   jax.ShapeDtypeStruct((8, n), I32)],
        compiler_params=_cparams(("arbitrary", "arbitrary")),
        name="post",
    )(x, hf, hb, gyg, ma, sgb, mod, wpb, wo, gffn, wrt, brt)


def _sort_kernel(bk_ref, slot_ref, meta_ref, *, tm):
    bk = bk_ref[...]
    r = bk.shape[0]
    nt_lanes = meta_ref.shape[1]
    shift = tm.bit_length() - 1
    upper = (lax.broadcasted_iota(I32, (LANES, LANES), 0) < lax.broadcasted_iota(I32, (LANES, LANES), 1))
    upper = upper.astype(F32).astype(BF16)
    lower = (lax.broadcasted_iota(I32, (r, r), 1) < lax.broadcasted_iota(I32, (r, r), 0))
    lower = lower.astype(F32).astype(BF16)
    ones = jnp.ones((LANES, LANES), BF16)
    tile_pos = lax.broadcasted_iota(I32, (1, nt_lanes), 1) * tm
    start = jnp.zeros((1, LANES), I32)
    start_t = jnp.zeros((1, nt_lanes), I32)
    slot = jnp.zeros((r, LANES), I32)
    tile_bucket = jnp.zeros((1, nt_lanes), I32)
    for k in range(N_BUCKETS):
        m = bk == k
        mb = m.astype(F32).astype(BF16)
        pre = _bdot(mb, upper)
        tot = _bdot(mb, ones)
        rowpre = _bdot(lower, tot.astype(BF16))
        rank = (pre + rowpre).astype(I32)
        cnt = (rowpre[r - 1:r, :] + tot[r - 1:r, :]).astype(I32)
        slot = jnp.where(m, start + rank, slot)
        padded = ((cnt + (tm - 1)) >> shift) << shift
        start = start + padded
        start_t = start_t + jnp.concatenate([padded] * (nt_lanes // LANES), axis=1)
        tile_bucket = tile_bucket + (tile_pos >= start_t).astype(I32)
    slot_ref[...] = slot
    valid = (tile_pos < start_t).astype(I32)
    tb = jnp.minimum(tile_bucket, N_BUCKETS - 1)
    g = (tb * 43) >> 8
    pair = tb - N_PAIRS * g
    ge3 = (pair >= 3).astype(I32)
    ge5 = (pair >= 5).astype(I32)
    lo = ge3 + ge5
    hi = pair + 1 - 2 * ge3 - ge5
    e_lo = g * EXPERTS_PER_GROUP + lo
    e_hi = g * EXPERTS_PER_GROUP + hi
    meta_ref[...] = jnp.concatenate([e_lo, e_hi, valid, jnp.zeros((5, nt_lanes), I32)], axis=0)


def _sort(bucket2d, tm, nt_lanes):
    r = bucket2d.shape[0]
    return pl.pallas_call(
        functools.partial(_sort_kernel, tm=tm),
        grid=(1,),
        in_specs=[pl.BlockSpec((r, LANES), lambda i: (0, 0))],
        out_specs=[pl.BlockSpec((r, LANES), lambda i: (0, 0)),
                   pl.BlockSpec((8, nt_lanes), lambda i: (0, 0))],
        out_shape=[jax.ShapeDtypeStruct((r, LANES), I32),
                   jax.ShapeDtypeStruct((8, nt_lanes), I32)],
        compiler_params=_cparams(("arbitrary",)),
        name="bucket_sort",
    )(bucket2d)


def _moe_kernel(elo_ref, ehi_ref, valid_ref, h_ref, wg_lo, wg_hi, wu_lo, wu_hi, wd_lo, wd_hi, o_ref):
    i = pl.program_id(0)
    half = o_ref.shape[1]

    @pl.when(valid_ref[i] != 0)
    def _():
        h_a, h_b = _unpack_bf16_pair(h_ref[:, 0:half])
        h = jnp.concatenate([h_a, h_b], axis=1).astype(BF16)
        w_lo = lax.bitcast_convert_type(h_ref[:, half:half + 1], F32)
        w_hi = lax.bitcast_convert_type(h_ref[:, half + 1:half + 2], F32)
        act_lo = _silu(_bdot(h, wg_lo[0])) * _bdot(h, wu_lo[0]) * w_lo
        act_hi = _silu(_bdot(h, wg_hi[0])) * _bdot(h, wu_hi[0]) * w_hi
        o = _bdot(act_lo.astype(BF16), wd_lo[0]) + _bdot(act_hi.astype(BF16), wd_hi[0])
        o_ref[...] = _pack_bf16_pair(o[:, 0:half], o[:, half:2 * half])

    @pl.when(valid_ref[i] == 0)
    def _():
        o_ref[...] = jnp.zeros(o_ref.shape, U32)


def _moe(e_lo, e_hi, valid, hs, wgate, wup, wdown):
    npad, wrow = hs.shape
    tm = T_MOE
    n_tiles = npad // tm
    ne, d, f = wgate.shape
    lo3 = lambda i, elo, ehi, va: (elo[i], 0, 0)
    hi3 = lambda i, elo, ehi, va: (ehi[i], 0, 0)
    row = lambda i, elo, ehi, va: (i, 0)
    grid_spec = pltpu.PrefetchScalarGridSpec(
        num_scalar_prefetch=3,
        grid=(n_tiles,),
        in_specs=[pl.BlockSpec((tm, wrow), row),
                  pl.BlockSpec((1, d, f), lo3), pl.BlockSpec((1, d, f), hi3),
                  pl.BlockSpec((1, d, f), lo3), pl.BlockSpec((1, d, f), hi3),
                  pl.BlockSpec((1, f, d), lo3), pl.BlockSpec((1, f, d), hi3)],
        out_specs=pl.BlockSpec((tm, d // 2), row),
    )
    return pl.pallas_call(
        _moe_kernel,
        grid_spec=grid_spec,
        out_shape=jax.ShapeDtypeStruct((npad, d // 2), U32),
        compiler_params=_cparams(("arbitrary",)),
        name="moe",
    )(e_lo, e_hi, valid, hs, wgate, wgate, wup, wup, wdown, wdown)


SC_CORES = 2
SC_SUBCORES = 16
SC_LANES = 16
SC_WORKERS = SC_CORES * SC_SUBCORES
SC_ROWS = 64


def _sc_worker_id():
    return lax.axis_index("s") * SC_CORES + lax.axis_index("c")


def _sc_dispatch(hx, slot, npad):
    n, wrow = hx.shape
    assert npad % (SC_WORKERS * SC_ROWS) == 0 and n % SC_LANES == 0
    rows = npad // SC_WORKERS
    mesh = plsc.VectorSubcoreMesh(core_axis_name="c", subcore_axis_name="s")

    @functools.partial(
        pl.kernel, mesh=mesh,
        out_type=jax.ShapeDtypeStruct((npad, wrow), hx.dtype),
        scratch_types=[pltpu.VMEM((n,), I32),
                       pltpu.VMEM((rows,), I32),
                       pltpu.VMEM((SC_ROWS, wrow), hx.dtype),
                       pltpu.SemaphoreType.DMA],
        compiler_params=pltpu.CompilerParams(needs_layout_passes=False),
        name="sc_dispatch",
    )
    def k(hx_hbm, slot_hbm, out_hbm, slot_v, tok_v, rows_v, sem):
        base = _sc_worker_id() * rows
        pltpu.sync_copy(slot_hbm, slot_v)

        @pl.loop(0, rows // SC_LANES)
        def _(i):
            tok_v[pl.ds(i * SC_LANES, SC_LANES)] = jnp.zeros((SC_LANES,), I32)

        lane = lax.iota(I32, SC_LANES)

        @pl.loop(0, n // SC_LANES)
        def _(i):
            loc = slot_v[pl.ds(i * SC_LANES, SC_LANES)] - base
            mine = (loc >= 0) & (loc < rows)
            plsc.store_scatter(tok_v, [loc], i * SC_LANES + lane, mask=mine)

        @pl.loop(0, rows // SC_ROWS)
        def _(c):
            off = pl.multiple_of(c * SC_ROWS, SC_ROWS)
            pltpu.async_copy(hx_hbm.at[tok_v.at[pl.ds(off, SC_ROWS)]], rows_v, sem).wait()
            pltpu.sync_copy(rows_v, out_hbm.at[pl.ds(base + off, SC_ROWS)])

    return k(hx, slot)


def _sc_combine(mo_sorted, slot):
    n = slot.shape[0]
    wrow = mo_sorted.shape[1]
    assert n % (SC_WORKERS * SC_ROWS) == 0
    toks = n // SC_WORKERS
    mesh = plsc.VectorSubcoreMesh(core_axis_name="c", subcore_axis_name="s")

    @functools.partial(
        pl.kernel, mesh=mesh,
        out_type=jax.ShapeDtypeStruct((n, wrow), mo_sorted.dtype),
        scratch_types=[pltpu.VMEM((toks,), I32),
                       pltpu.VMEM((SC_ROWS, wrow), mo_sorted.dtype),
                       pltpu.SemaphoreType.DMA],
        compiler_params=pltpu.CompilerParams(needs_layout_passes=False),
        name="sc_combine",
    )
    def k(mo_hbm, slot_hbm, out_hbm, idx_v, rows_v, sem):
        base = _sc_worker_id() * toks
        pltpu.sync_copy(slot_hbm.at[pl.ds(base, toks)], idx_v)

        @pl.loop(0, toks // SC_ROWS)
        def _(c):
            off = pl.multiple_of(c * SC_ROWS, SC_ROWS)
            pltpu.async_copy(mo_hbm.at[idx_v.at[pl.ds(off, SC_ROWS)]], rows_v, sem).wait()
            pltpu.sync_copy(rows_v, out_hbm.at[pl.ds(base + off, SC_ROWS)])

    return k(mo_sorted, slot)


def _final_kernel(x1_ref, mo_ref, mod_ref, g_ref, o_ref):
    d = x1_ref.shape[2]
    b = pl.program_id(0)
    gt2 = mod_ref[pl.ds(b, 1), 5 * d:6 * d]
    mo = jnp.concatenate(_unpack_bf16_pair(mo_ref[0]), axis=1)
    o_ref[0] = _rms_norm(x1_ref[0] + gt2 * mo, g_ref[...])


def _final(x1, mo, mod, g):
    bsz, s, d = x1.shape
    t = T_TOK
    tok = lambda: pl.BlockSpec((1, t, d), lambda b, i: (b, i, 0))
    return pl.pallas_call(
        _final_kernel,
        grid=(bsz, s // t),
        in_specs=[tok(), pl.BlockSpec((1, t, d // 2), lambda b, i: (b, i, 0)),
                  pl.BlockSpec(mod.shape, lambda b, i: (0, 0)), pl.BlockSpec((1, d), lambda b, i: (0, 0))],
        out_specs=tok(),
        out_shape=jax.ShapeDtypeStruct((bsz, s, d), F32),
        compiler_params=_cparams(("arbitrary", "arbitrary")),
        name="final",
    )(x1, mo, mod, g)


def _pair_blockdiag(w):
    h, hd, _ = w.shape
    w2 = w.reshape(h // 2, 2, hd, hd)
    z = jnp.zeros((h // 2, hd, hd), w.dtype)
    top = jnp.concatenate([w2[:, 0], z], axis=2)
    bot = jnp.concatenate([z, w2[:, 1]], axis=2)
    return jnp.concatenate([top, bot], axis=1)


def _gate_params(w_r, b_r, w_i, b_i):
    wg = jnp.concatenate([_pair_blockdiag(w_r), _pair_blockdiag(w_i)], axis=2).astype(BF16)
    h, hd = b_r.shape
    bg = jnp.concatenate([b_r.reshape(h // 2, 1, 2 * hd), b_i.reshape(h // 2, 1, 2 * hd)], axis=2)
    return wg, bg


def kernel(x, c, ctx, c_ctx, w_ada, b_ada, g_mix, w_in, b_in, conv_w, conv_b, ln_g, ln_b, w_pa, lru_conv_w, lru_conv_b, w_r_f, b_r_f, w_i_f, b_i_f, lam_f, w_r_b, b_r_b, w_i_b, b_i_b, lam_b, w_pb, w_o, g_ffn, w_grp, b_grp, w_er, b_er, w_gate, w_up, w_down, g_final):
    bsz, s, d = x.shape
    depth = w_ada.shape[0]
    assert depth == 1 and bsz + 1 <= 8
    l = 0
    n = bsz * s
    wc = conv_w.shape[2]
    wl = lru_conv_w.shape[2]
    rec0 = 2 * wc + wl
    ctx_row = bsz

    c8 = jnp.concatenate([c, c_ctx[None, :], jnp.zeros((8 - bsz - 1, d), F32)], axis=0)
    mod = _ada(c8, w_ada[l], b_ada[l][None, :])

    win = w_in[l].astype(BF16)
    bin_ = b_in[l][None, :]
    gmix = g_mix[l][None, :]
    lcw = lru_conv_w[l]
    lcb = lru_conv_b[l][None, :]
    wg_f, bg_f = _gate_params(w_r_f[l], b_r_f[l], w_i_f[l], b_i_f[l])
    wg_b, bg_b = _gate_params(w_r_b[l], b_r_b[l], w_i_b[l], b_i_b[l])
    lamf = lam_f[l][None, :]
    lamb = lam_b[l][None, :]

    xr_c = _ctxproj(ctx, mod, gmix, win[:, rec0:rec0 + wl], bin_[:, rec0:rec0 + wl], ctx_row)
    zero_state = jnp.zeros((bsz, SUBLANES, LANES), F32)
    _, h0f = _scan(xr_c, zero_state, lcw, lcb, wg_f, bg_f, lamf, reverse=False, emit_h=False)
    _, h0b = _scan(xr_c, zero_state, lcw, lcb, wg_b, bg_b, lamb, reverse=True, emit_h=False)

    ma, gyg, sgb, xr = _k1(x, mod, gmix, win, bin_, conv_w[l], conv_b[l][None, :], ln_g[l][None, :],
                           ln_b[l][None, :], w_pa[l].astype(BF16))
    hf, _ = _scan(xr, h0f, lcw, lcb, wg_f, bg_f, lamf, reverse=False, emit_h=True)
    hb, _ = _scan(xr, h0b, lcw, lcb, wg_b, bg_b, lamb, reverse=True, emit_h=True)

    n_rt = 32
    wrt = jnp.concatenate([w_grp[l].T, w_er[l].T, jnp.zeros((n_rt - N_GROUPS * (1 + EXPERTS_PER_GROUP), d), F32)], axis=0)
    brt = jnp.concatenate([b_grp[l], b_er[l], jnp.zeros((n_rt - N_GROUPS * (1 + EXPERTS_PER_GROUP),), F32)])
    brt = jnp.broadcast_to(brt[:, None], (n_rt, T_TOK))
    x1, hx, rti = _post(x, hf, hb, gyg, ma, sgb, mod, w_pb[l].astype(BF16), w_o[l].astype(BF16),
                        g_ffn[l][None, :], wrt, brt)

    tm = T_MOE
    n_tiles = n // tm + N_BUCKETS
    npad = n_tiles * tm
    nt_lanes = ((n_tiles + LANES - 1) // LANES) * LANES
    slot2d, meta = _sort(rti[0].reshape(n // LANES, LANES), tm, nt_lanes)
    slot = slot2d.reshape(n)
    hs = _sc_dispatch(hx.reshape(n, hx.shape[2]), slot, npad)
    mo_sorted = _moe(meta[0, :n_tiles], meta[1, :n_tiles], meta[2, :n_tiles], hs,
                     w_gate[l].astype(BF16), w_up[l].astype(BF16), w_down[l].astype(BF16))
    mo = _sc_combine(mo_sorted, slot).reshape(bsz, s, d // 2)

    return _final(x1, mo, mod, g_final[None, :])
```

```python
import functools

import jax
import jax.numpy as jnp
from jax import lax
from jax.experimental import pallas as pl
from jax.experimental.pallas import tpu as pltpu
from jax.experimental.pallas import tpu_sc as plsc

F32 = jnp.float32
BF16 = jnp.bfloat16
I32 = jnp.int32
U32 = jnp.uint32
HIGHEST = lax.Precision.HIGHEST

EPS = 1e-6
GRID_W = 64
CONV_K = 31
CONV_PAD = 16
LRU_CONV_K = 4
LRU_C = 8.0
LRU_HEADS = 8
N_GROUPS = 4
EXPERTS_PER_GROUP = 4
N_PAIRS = 6
N_BUCKETS = N_GROUPS * N_PAIRS
LANES = 128
SUBLANES = 8
VMEM_LIMIT = 56 * 1024 * 1024

T_TOK = 256
T_POST = 512
T_SCAN = 256
T_MOE = 256


def _sigmoid(x):
    return 0.5 * (jnp.tanh(0.5 * x) + 1.0)


def _silu(x):
    return x * _sigmoid(x)


def _gelu_tanh(x):
    return 0.5 * x * (1.0 + jnp.tanh(0.7978845608028654 * (x + 0.044715 * (x * x * x))))


def _rms_norm(x, g):
    return x * lax.rsqrt(jnp.mean(x * x, axis=-1, keepdims=True) + EPS) * g


def _bdot(a, b):
    return jnp.dot(a, b, preferred_element_type=F32)


def _pack_bf16_pair(lo, hi):
    lo_bits = lax.bitcast_convert_type(lo.astype(BF16).astype(F32), U32)
    hi_bits = lax.bitcast_convert_type(hi.astype(BF16).astype(F32), U32)
    return (hi_bits & jnp.uint32(0xFFFF0000)) | (lo_bits >> 16)


def _unpack_bf16_pair(u):
    lo = lax.bitcast_convert_type(u << 16, F32)
    hi = lax.bitcast_convert_type(u & jnp.uint32(0xFFFF0000), F32)
    return lo, hi


def _cparams(sem):
    return pltpu.CompilerParams(dimension_semantics=sem, vmem_limit_bytes=VMEM_LIMIT)


def _ada_kernel(c_ref, w_ref, b_ref, o_ref):
    s = _silu(c_ref[...])
    o_ref[...] = jnp.dot(s, w_ref[...], precision=HIGHEST, preferred_element_type=F32) + b_ref[...]


def _ada(c8, w, b):
    d, n = w.shape
    blk = 1024
    return pl.pallas_call(
        _ada_kernel,
        grid=(n // blk,),
        in_specs=[pl.BlockSpec((8, d), lambda j: (0, 0)),
                  pl.BlockSpec((d, blk), lambda j: (0, j)),
                  pl.BlockSpec((1, blk), lambda j: (0, j))],
        out_specs=pl.BlockSpec((8, blk), lambda j: (0, j)),
        out_shape=jax.ShapeDtypeStruct((8, n), F32),
        compiler_params=_cparams(("arbitrary",)),
        name="ada",
    )(c8, w, b)


def _ctxproj_kernel(x_ref, mod_ref, g_ref, w_ref, b_ref, o_ref, *, row):
    d = x_ref.shape[-1]
    xn = _rms_norm(x_ref[0], g_ref[...])
    sh = mod_ref[row:row + 1, 0:d]
    sc = mod_ref[row:row + 1, d:2 * d]
    hx = (xn * (1.0 + sc) + sh).astype(BF16)
    o_ref[0] = _bdot(hx, w_ref[...]) + b_ref[...]


def _ctxproj(ctx, mod, g, w, b, row):
    bsz, n, d = ctx.shape
    return pl.pallas_call(
        functools.partial(_ctxproj_kernel, row=row),
        grid=(bsz,),
        in_specs=[pl.BlockSpec((1, n, d), lambda i: (i, 0, 0)),
                  pl.BlockSpec(mod.shape, lambda i: (0, 0)),
                  pl.BlockSpec((1, d), lambda i: (0, 0)),
                  pl.BlockSpec(w.shape, lambda i: (0, 0)),
                  pl.BlockSpec((1, w.shape[1]), lambda i: (0, 0))],
        out_specs=pl.BlockSpec((1, n, w.shape[1]), lambda i: (i, 0, 0)),
        out_shape=jax.ShapeDtypeStruct((bsz, n, w.shape[1]), F32),
        compiler_params=_cparams(("arbitrary",)),
        name="ctxproj",
    )(ctx, mod, g, w, b)


def _scan_kernel(xr_ref, prev_ref, next_ref, h0_ref, cw_ref, cb_ref, wg_ref, bg_ref, lam_ref, *rest,
                 reverse, emit_h, n_chunks):
    if emit_h:
        h_ref, hfin_ref, xp_ref, a_ref, b_ref, hst_ref = rest
    else:
        hfin_ref, xp_ref, a_ref, b_ref, hst_ref = rest
    bsz, tc, w = xr_ref.shape
    nblk = w // LANES
    i = pl.program_id(0)
    chunk = (n_chunks - 1 - i) if reverse else i

    @pl.when(i == 0)
    def _():
        hst_ref[...] = h0_ref[...]

    has_prev = chunk > 0
    has_next = chunk < n_chunks - 1
    z = -lam_ref[...]
    softplus = jnp.maximum(z, 0.0) + jnp.log1p(jnp.exp(-jnp.abs(z)))
    nlsh = (-0.5 * LRU_C) * softplus

    for b in range(bsz):
        bp, par = divmod(b, 2)
        for cblk in range(nblk):
            cs = slice(cblk * LANES, (cblk + 1) * LANES)
            xp_ref[bp, cblk, pl.ds(par, 8, stride=2), :] = jnp.where(has_prev, prev_ref[b, :, cs], 0.0)
            xp_ref[bp, cblk, pl.ds(16 + par, tc, stride=2), :] = xr_ref[b, :, cs]
            xp_ref[bp, cblk, pl.ds(16 + 2 * tc + par, 8, stride=2), :] = jnp.where(has_next, next_ref[b, :, cs], 0.0)

    for b in range(bsz):
        bp, par = divmod(b, 2)
        xc = []
        for cblk in range(nblk):
            cs = slice(cblk * LANES, (cblk + 1) * LANES)
            acc = cb_ref[:, cs] + cw_ref[0:1, cs] * xp_ref[bp, cblk, pl.ds(2 * 6 + par, tc, stride=2), :]
            for k in range(1, LRU_CONV_K):
                acc = acc + cw_ref[k:k + 1, cs] * xp_ref[bp, cblk, pl.ds(2 * (6 + k) + par, tc, stride=2), :]
            xc.append(acc)
        for p in range(LRU_HEADS // 2):
            cs = slice(2 * p * LANES, 2 * (p + 1) * LANES)
            xc_p = jnp.concatenate([xc[2 * p], xc[2 * p + 1]], axis=1)
            th = jnp.tanh(_bdot(xc_p.astype(BF16), wg_ref[p]) + bg_ref[p])
            log_a = nlsh[:, cs] * (th[:, 0:2 * LANES] + 1.0)
            a = jnp.exp(log_a)
            y = jnp.tanh(-log_a) * (1.0 + a * a)
            root = jnp.where(y > 0.0, y * lax.rsqrt(y), 0.0)
            bb = root * ((th[:, 2 * LANES:4 * LANES] + 1.0) * (0.5 * xc_p))
            for jj in range(2):
                j = 2 * p + jj
                a_ref[b, pl.ds(j, tc, stride=nblk), :] = a[:, jj * LANES:(jj + 1) * LANES]
                b_ref[b, pl.ds(j, tc, stride=nblk), :] = bb[:, jj * LANES:(jj + 1) * LANES]

    def step(s, hs):
        t = (tc - 1 - s) if reverse else s
        row = pl.multiple_of(t * nblk, nblk)
        out = []
        for b in range(bsz):
            h = a_ref[b, pl.ds(row, nblk), :] * hs[b] + b_ref[b, pl.ds(row, nblk), :]
            b_ref[b, pl.ds(row, nblk), :] = h
            out.append(h)
        return tuple(out)

    hs = lax.fori_loop(0, tc, step, tuple(hst_ref[b] for b in range(bsz)), unroll=8)
    for b in range(bsz):
        hst_ref[b] = hs[b]
        hfin_ref[b] = hs[b]
    if emit_h:
        for b in range(bsz):
            for j in range(nblk):
                h_ref[b, :, j * LANES:(j + 1) * LANES] = b_ref[b, pl.ds(j, tc, stride=nblk), :]


def _scan(xr, h0, cw, cb, wg, bg, lam, *, reverse, emit_h):
    bsz, s, w = xr.shape
    tc = min(T_SCAN, s)
    assert s % tc == 0 and w == SUBLANES * LANES and bsz % 2 == 0
    n_chunks = s // tc
    hb = tc // 8

    def cidx(i):
        return (n_chunks - 1 - i) if reverse else i

    in_specs = [
        pl.BlockSpec((bsz, tc, w), lambda i: (0, cidx(i), 0)),
        pl.BlockSpec((bsz, 8, w), lambda i: (0, jnp.maximum(cidx(i) * hb - 1, 0), 0)),
        pl.BlockSpec((bsz, 8, w), lambda i: (0, jnp.minimum((cidx(i) + 1) * hb, n_chunks * hb - 1), 0)),
        pl.BlockSpec((bsz, SUBLANES, LANES), lambda i: (0, 0, 0)),
        pl.BlockSpec(cw.shape, lambda i: (0, 0)),
        pl.BlockSpec(cb.shape, lambda i: (0, 0)),
        pl.BlockSpec(wg.shape, lambda i: (0, 0, 0)),
        pl.BlockSpec(bg.shape, lambda i: (0, 0, 0)),
        pl.BlockSpec(lam.shape, lambda i: (0, 0)),
    ]
    out_specs = [pl.BlockSpec((bsz, SUBLANES, LANES), lambda i: (0, 0, 0))]
    out_shape = [jax.ShapeDtypeStruct((bsz, SUBLANES, LANES), F32)]
    if emit_h:
        out_specs = [pl.BlockSpec((bsz, tc, w), lambda i: (0, cidx(i), 0))] + out_specs
        out_shape = [jax.ShapeDtypeStruct((bsz, s, w), F32)] + out_shape
    res = pl.pallas_call(
        functools.partial(_scan_kernel, reverse=reverse, emit_h=emit_h, n_chunks=n_chunks),
        grid=(n_chunks,),
        in_specs=in_specs,
        out_specs=out_specs,
        out_shape=out_shape,
        scratch_shapes=[pltpu.VMEM((bsz // 2, w // LANES, 2 * (tc + 16), LANES), F32),
                        pltpu.VMEM((bsz, tc * SUBLANES, LANES), F32),
                        pltpu.VMEM((bsz, tc * SUBLANES, LANES), F32),
                        pltpu.VMEM((bsz, SUBLANES, LANES), F32)],
        compiler_params=_cparams(("arbitrary",)),
        name="scan_bwd" if reverse else "scan_fwd",
    )(xr, xr, xr, h0, cw, cb, wg, bg, lam)
    return res if emit_h else (None, res[0])


def _k1_kernel(x_ref, mod_ref, gmix_ref, win_ref, bin_ref, cw_ref, cb_ref, lng_ref, lnb_ref, wpa_ref,
               ma_ref, gyg_ref, sgb_ref, xr_ref, zp_ref, co_ref, sga_ref):
    t, d = x_ref.shape[1], x_ref.shape[2]
    wc = cw_ref.shape[1]
    b = pl.program_id(0)
    xn = _rms_norm(x_ref[0], gmix_ref[...])
    sh1 = mod_ref[pl.ds(b, 1), 0:d]
    sc1 = mod_ref[pl.ds(b, 1), d:2 * d]
    hx = (xn * (1.0 + sc1) + sh1).astype(BF16)

    def proj(lo, hi):
        return _bdot(hx, win_ref[:, lo:hi]) + bin_ref[:, lo:hi]

    u = proj(0, wc)
    v = proj(wc, 2 * wc)
    zglu = u * _sigmoid(v)
    n_rows = t // GRID_W
    n_cblk = wc // LANES
    lo = CONV_PAD
    hi = CONV_PAD + GRID_W
    zp_ref[:, :, 0:2 * lo, :] = jnp.zeros((n_rows // 2, n_cblk, 2 * lo, LANES), F32)
    zp_ref[:, :, 2 * hi:2 * (hi + CONV_PAD), :] = jnp.zeros((n_rows // 2, n_cblk, 2 * CONV_PAD, LANES), F32)
    for g in range(n_rows):
        gp, par = divmod(g, 2)
        for cblk in range(n_cblk):
            zp_ref[gp, cblk, pl.ds(2 * lo + par, GRID_W, stride=2), :] = (
                zglu[g * GRID_W:(g + 1) * GRID_W, cblk * LANES:(cblk + 1) * LANES])

    o = 2 * wc
    gyg_ref[0] = _gelu_tanh(proj(o, o + d)).astype(BF16)
    xr_ref[0] = proj(o + d, o + 2 * d)
    sga_ref[...] = _sigmoid(proj(o + 2 * d, o + 3 * d))
    sgb_ref[0] = _sigmoid(proj(o + 3 * d, o + 4 * d)).astype(BF16)

    for g in range(n_rows):
        gp, par = divmod(g, 2)
        for cblk in range(n_cblk):
            cs = slice(cblk * LANES, (cblk + 1) * LANES)
            acc = jnp.broadcast_to(cb_ref[:, cs], (GRID_W, LANES))
            for k in range(CONV_K):
                off = lo - CONV_K // 2 + k
                acc = acc + cw_ref[k:k + 1, cs] * zp_ref[gp, cblk, pl.ds(2 * off + par, GRID_W, stride=2), :]
            co_ref[g * GRID_W:(g + 1) * GRID_W, cs] = acc
    c = co_ref[...]
    mu = jnp.mean(c, axis=-1, keepdims=True)
    cc = c - mu
    y = cc * lax.rsqrt(jnp.mean(cc * cc, axis=-1, keepdims=True) + EPS) * lng_ref[...] + lnb_ref[...]
    a_lat = _bdot(_silu(y).astype(BF16), wpa_ref[...])
    ma_ref[0] = (sga_ref[...] * a_lat).astype(BF16)


def _k1(x, mod, gmix, win, bin_, cw, cb, lng, lnb, wpa):
    bsz, s, d = x.shape
    t = T_TOK
    assert s % t == 0 and t % GRID_W == 0
    wc = cw.shape[1]

    def full(a):
        return pl.BlockSpec(a.shape, lambda b, i: (0,) * a.ndim)

    def const(a):
        return pl.BlockSpec(a.shape, lambda b, i: (0,) * a.ndim, pipeline_mode=pl.Buffered(1))

    tok = lambda: pl.BlockSpec((1, t, d), lambda b, i: (b, i, 0))
    return pl.pallas_call(
        _k1_kernel,
        grid=(bsz, s // t),
        in_specs=[tok(), full(mod), full(gmix), const(win), full(bin_), full(cw), full(cb), full(lng), full(lnb),
                  const(wpa)],
        out_specs=[tok(), tok(), tok(), tok()],
        out_shape=[jax.ShapeDtypeStruct((bsz, s, d), BF16),
                   jax.ShapeDtypeStruct((bsz, s, d), BF16),
                   jax.ShapeDtypeStruct((bsz, s, d), BF16),
                   jax.ShapeDtypeStruct((bsz, s, d), F32)],
        scratch_shapes=[pltpu.VMEM((t // GRID_W // 2, wc // LANES, 2 * (GRID_W + 2 * CONV_PAD), LANES), F32),
                        pltpu.VMEM((t, wc), F32),
                        pltpu.VMEM((t, d), F32)],
        compiler_params=_cparams(("arbitrary", "arbitrary")),
        name="k1",
    )(x, mod, gmix, win, bin_, cw, cb, lng, lnb, wpa)


def _post_kernel(x_ref, hf_ref, hb_ref, gyg_ref, ma_ref, sgb_ref, mod_ref, wpb_ref, wo_ref, gffn_ref,
                 wrt_ref, brt_ref, x1_ref, hx_ref, rti_ref):
    d = x_ref.shape[2]
    b = pl.program_id(0)
    y_rec = hf_ref[0] + hb_ref[0]
    b_lat = _bdot((gyg_ref[0].astype(F32) * y_rec).astype(BF16), wpb_ref[...])
    merged = ma_ref[0].astype(F32) + sgb_ref[0].astype(F32) * b_lat
    res = _bdot(merged.astype(BF16), wo_ref[...])
    gt1 = mod_ref[pl.ds(b, 1), 2 * d:3 * d]
    x1 = x_ref[0] + gt1 * res
    x1_ref[0] = x1
    sh2 = mod_ref[pl.ds(b, 1), 3 * d:4 * d]
    sc2 = mod_ref[pl.ds(b, 1), 4 * d:5 * d]
    hm = _rms_norm(x1, gffn_ref[...]) * (1.0 + sc2) + sh2
    half = d // 2
    hx_ref[0, :, 0:half] = _pack_bf16_pair(hm[:, 0:half], hm[:, half:d])

    nt = (((1,), (1,)), ((), ()))
    hm_hi = hm.astype(BF16)
    hm_lo = (hm - hm_hi.astype(F32)).astype(BF16)
    lg = (lax.dot_general(wrt_ref[0], hm_hi, nt, preferred_element_type=F32)
          + lax.dot_general(wrt_ref[0], hm_lo, nt, preferred_element_type=F32)
          + lax.dot_general(wrt_ref[1], hm_hi, nt, preferred_element_type=F32)) + brt_ref[...]
    gl = [lg[k:k + 1, :] for k in range(N_GROUPS)]
    gmax = jnp.maximum(jnp.maximum(gl[0], gl[1]), jnp.maximum(gl[2], gl[3]))
    gsel = jnp.where(gl[0] == gmax, 0, jnp.where(gl[1] == gmax, 1, jnp.where(gl[2] == gmax, 2, 3)))
    p_grp = 1.0 / (jnp.exp(gl[0] - gmax) + jnp.exp(gl[1] - gmax) + jnp.exp(gl[2] - gmax) + jnp.exp(gl[3] - gmax))
    el = []
    for j in range(EXPERTS_PER_GROUP):
        rows = [lg[N_GROUPS + g * EXPERTS_PER_GROUP + j:N_GROUPS + g * EXPERTS_PER_GROUP + j + 1, :]
                for g in range(N_GROUPS)]
        el.append(jnp.where(gsel == 0, rows[0], jnp.where(gsel == 1, rows[1], jnp.where(gsel == 2, rows[2], rows[3]))))
    m1 = jnp.maximum(jnp.maximum(el[0], el[1]), jnp.maximum(el[2], el[3]))
    i1 = jnp.where(el[0] == m1, 0, jnp.where(el[1] == m1, 1, jnp.where(el[2] == m1, 2, 3)))
    neg = jnp.float32(-jnp.inf)
    rest = [jnp.where(i1 == j, neg, el[j]) for j in range(EXPERTS_PER_GROUP)]
    m2 = jnp.maximum(jnp.maximum(rest[0], rest[1]), jnp.maximum(rest[2], rest[3]))
    i2 = jnp.where((rest[0] == m2) & (i1 != 0), 0,
                   jnp.where((rest[1] == m2) & (i1 != 1), 1, jnp.where((rest[2] == m2) & (i1 != 2), 2, 3)))
    e21 = jnp.exp(m2 - m1)
    den = 1.0 / (1.0 + e21)
    w1 = p_grp * den
    w2 = p_grp * (e21 * den)
    first_lo = i1 < i2
    lo = jnp.where(first_lo, i1, i2)
    hi = jnp.where(first_lo, i2, i1)
    w_lo = jnp.where(first_lo, w1, w2)
    w_hi = jnp.where(first_lo, w2, w1)
    pair = jnp.where(lo == 0, hi - 1, jnp.where(lo == 1, hi + 1, 5))
    bucket = gsel * N_PAIRS + pair
    t = bucket.shape[1]
    rti_ref[...] = jnp.concatenate([bucket.astype(I32), jnp.zeros((7, t), I32)], axis=0)
    wrows = jnp.concatenate([w_lo, w_hi, jnp.zeros((LANES - 2, t), F32)], axis=0)
    eye = (lax.broadcasted_iota(I32, (t, t), 0) == lax.broadcasted_iota(I32, (t, t), 1)).astype(F32).astype(BF16)
    p1 = wrows.astype(BF16)
    r1 = wrows - p1.astype(F32)
    p2 = r1.astype(BF16)
    p3 = (r1 - p2.astype(F32)).astype(BF16)
    wcols = (lax.dot_general(eye, p1, nt, preferred_element_type=F32)
             + lax.dot_general(eye, p2, nt, preferred_element_type=F32)
             + lax.dot_general(eye, p3, nt, preferred_element_type=F32))
    hx_ref[0, :, half:half + LANES] = lax.bitcast_convert_type(wcols, U32)


def _post(x, hf, hb, gyg, ma, sgb, mod, wpb, wo, gffn, wrt, brt):
    bsz, s, d = x.shape
    t = brt.shape[1]
    assert s % t == 0
    nt = s // t
    n = bsz * s

    def full(a):
        return pl.BlockSpec(a.shape, lambda b, i: (0,) * a.ndim)

    def const(a):
        return pl.BlockSpec(a.shape, lambda b, i: (0,) * a.ndim, pipeline_mode=pl.Buffered(1))

    tok = lambda: pl.BlockSpec((1, t, d), lambda b, i: (b, i, 0))
    rt = lambda: pl.BlockSpec((8, t), lambda b, i: (0, b * nt + i))
    return pl.pallas_call(
        _post_kernel,
        grid=(bsz, nt),
        in_specs=[tok(), tok(), tok(), tok(), tok(), tok(), full(mod), const(wpb), const(wo), full(gffn), full(wrt),
                  full(brt)],
        out_specs=[tok(), pl.BlockSpec((1, t, d // 2 + LANES), lambda b, i: (b, i, 0)), rt()],
        out_shape=[jax.ShapeDtypeStruct((bsz, s, d), F32),
                   jax.ShapeDtypeStruct((bsz, s, d // 2 + LANES), U32),
                   jax.ShapeDtypeStruct((8, n), I32)],
        compiler_params=_cparams(("arbitrary", "arbitrary")),
        name="post",
    )(x, hf, hb, gyg, ma, sgb, mod, wpb, wo, gffn, wrt, brt)


def _sort_kernel(bk_ref, slot_ref, meta_ref, *, tm):
    bk = bk_ref[...]
    r = bk.shape[0]
    nt_lanes = meta_ref.shape[1]
    shift = tm.bit_length() - 1
    upper = (lax.broadcasted_iota(I32, (LANES, LANES), 0) < lax.broadcasted_iota(I32, (LANES, LANES), 1))
    upper = upper.astype(F32).astype(BF16)
    lower = (lax.broadcasted_iota(I32, (r, r), 1) < lax.broadcasted_iota(I32, (r, r), 0))
    lower = lower.astype(F32).astype(BF16)
    ones = jnp.ones((LANES, LANES), BF16)
    tile_pos = lax.broadcasted_iota(I32, (1, nt_lanes), 1) * tm
    start = jnp.zeros((1, LANES), I32)
    start_t = jnp.zeros((1, nt_lanes), I32)
    slot = jnp.zeros((r, LANES), I32)
    tile_bucket = jnp.zeros((1, nt_lanes), I32)
    for k in range(N_BUCKETS):
        m = bk == k
        mb = m.astype(F32).astype(BF16)
        pre = _bdot(mb, upper)
        tot = _bdot(mb, ones)
        rowpre = _bdot(lower, tot.astype(BF16))
        rank = (pre + rowpre).astype(I32)
        cnt = (rowpre[r - 1:r, :] + tot[r - 1:r, :]).astype(I32)
        slot = jnp.where(m, start + rank, slot)
        padded = ((cnt + (tm - 1)) >> shift) << shift
        start = start + padded
        start_t = start_t + jnp.concatenate([padded] * (nt_lanes // LANES), axis=1)
        tile_bucket = tile_bucket + (tile_pos >= start_t).astype(I32)
    slot_ref[...] = slot
    valid = (tile_pos < start_t).astype(I32)
    tb = jnp.minimum(tile_bucket, N_BUCKETS - 1)
    g = (tb * 43) >> 8
    pair = tb - N_PAIRS * g
    ge3 = (pair >= 3).astype(I32)
    ge5 = (pair >= 5).astype(I32)
    lo = ge3 + ge5
    hi = pair + 1 - 2 * ge3 - ge5
    e_lo = g * EXPERTS_PER_GROUP + lo
    e_hi = g * EXPERTS_PER_GROUP + hi
    meta_ref[...] = jnp.concatenate([e_lo, e_hi, valid, jnp.zeros((5, nt_lanes), I32)], axis=0)


def _sort(bucket2d, tm, nt_lanes):
    r = bucket2d.shape[0]
    return pl.pallas_call(
        functools.partial(_sort_kernel, tm=tm),
        grid=(1,),
        in_specs=[pl.BlockSpec((r, LANES), lambda i: (0, 0))],
        out_specs=[pl.BlockSpec((r, LANES), lambda i: (0, 0)),
                   pl.BlockSpec((8, nt_lanes), lambda i: (0, 0))],
        out_shape=[jax.ShapeDtypeStruct((r, LANES), I32),
                   jax.ShapeDtypeStruct((8, nt_lanes), I32)],
        compiler_params=_cparams(("arbitrary",)),
        name="bucket_sort",
    )(bucket2d)


def _moe_kernel(elo_ref, ehi_ref, valid_ref, h_ref, wg_lo, wg_hi, wu_lo, wu_hi, wd_lo, wd_hi, o_ref):
    i = pl.program_id(0)
    half = o_ref.shape[1]

    @pl.when(valid_ref[i] != 0)
    def _():
        h_a, h_b = _unpack_bf16_pair(h_ref[:, 0:half])
        h = jnp.concatenate([h_a, h_b], axis=1).astype(BF16)
        w_lo = lax.bitcast_convert_type(h_ref[:, half:half + 1], F32)
        w_hi = lax.bitcast_convert_type(h_ref[:, half + 1:half + 2], F32)
        act_lo = _silu(_bdot(h, wg_lo[0])) * _bdot(h, wu_lo[0]) * w_lo
        act_hi = _silu(_bdot(h, wg_hi[0])) * _bdot(h, wu_hi[0]) * w_hi
        o = _bdot(act_lo.astype(BF16), wd_lo[0]) + _bdot(act_hi.astype(BF16), wd_hi[0])
        o_ref[...] = _pack_bf16_pair(o[:, 0:half], o[:, half:2 * half])

    @pl.when(valid_ref[i] == 0)
    def _():
        o_ref[...] = jnp.zeros(o_ref.shape, U32)


def _moe(e_lo, e_hi, valid, hs, wgate, wup, wdown):
    npad, wrow = hs.shape
    tm = T_MOE
    n_tiles = npad // tm
    ne, d, f = wgate.shape
    lo3 = lambda i, elo, ehi, va: (elo[i], 0, 0)
    hi3 = lambda i, elo, ehi, va: (ehi[i], 0, 0)
    row = lambda i, elo, ehi, va: (i, 0)
    grid_spec = pltpu.PrefetchScalarGridSpec(
        num_scalar_prefetch=3,
        grid=(n_tiles,),
        in_specs=[pl.BlockSpec((tm, wrow), row),
                  pl.BlockSpec((1, d, f), lo3), pl.BlockSpec((1, d, f), hi3),
                  pl.BlockSpec((1, d, f), lo3), pl.BlockSpec((1, d, f), hi3),
                  pl.BlockSpec((1, f, d), lo3), pl.BlockSpec((1, f, d), hi3)],
        out_specs=pl.BlockSpec((tm, d // 2), row),
    )
    return pl.pallas_call(
        _moe_kernel,
        grid_spec=grid_spec,
        out_shape=jax.ShapeDtypeStruct((npad, d // 2), U32),
        compiler_params=_cparams(("arbitrary",)),
        name="moe",
    )(e_lo, e_hi, valid, hs, wgate, wgate, wup, wup, wdown, wdown)


SC_CORES = 2
SC_SUBCORES = 16
SC_LANES = 16
SC_WORKERS = SC_CORES * SC_SUBCORES
SC_ROWS = 64


def _sc_worker_id():
    return lax.axis_index("s") * SC_CORES + lax.axis_index("c")


def _sc_gather_rows(src_hbm, idx_v, out_hbm, out_base, n_chunks, bufs, sems):
    def gather(c, b):
        off = pl.multiple_of(c * SC_ROWS, SC_ROWS)
        return pltpu.make_async_copy(src_hbm.at[idx_v.at[pl.ds(off, SC_ROWS)]], bufs[b], sems[b])

    def finish(c, b):
        gather(c, b).wait()
        pltpu.sync_copy(bufs[b], out_hbm.at[pl.ds(out_base + pl.multiple_of(c * SC_ROWS, SC_ROWS), SC_ROWS)])

    gather(0, 0).start()

    @pl.loop(0, n_chunks // 2)
    def _(p):
        c = 2 * p
        gather(c + 1, 1).start()
        finish(c, 0)

        @pl.when(c + 2 < n_chunks)
        def _():
            gather(c + 2, 0).start()

        finish(c + 1, 1)

    if n_chunks % 2:
        finish(n_chunks - 1, 0)


def _sc_dispatch(hx, slot, npad):
    n, wrow = hx.shape
    assert npad % (SC_WORKERS * SC_ROWS) == 0 and n % SC_LANES == 0
    rows = npad // SC_WORKERS
    mesh = plsc.VectorSubcoreMesh(core_axis_name="c", subcore_axis_name="s")

    @functools.partial(
        pl.kernel, mesh=mesh,
        out_type=jax.ShapeDtypeStruct((npad, wrow), hx.dtype),
        scratch_types=[pltpu.VMEM((n,), I32),
                       pltpu.VMEM((rows,), I32),
                       pltpu.VMEM((SC_ROWS, wrow), hx.dtype),
                       pltpu.VMEM((SC_ROWS, wrow), hx.dtype),
                       pltpu.SemaphoreType.DMA,
                       pltpu.SemaphoreType.DMA],
        compiler_params=pltpu.CompilerParams(needs_layout_passes=False),
        name="sc_dispatch",
    )
    def k(hx_hbm, slot_hbm, out_hbm, slot_v, tok_v, rows_a, rows_b, sem_a, sem_b):
        base = _sc_worker_id() * rows
        pltpu.sync_copy(slot_hbm, slot_v)

        @pl.loop(0, rows // SC_LANES)
        def _(i):
            tok_v[pl.ds(i * SC_LANES, SC_LANES)] = jnp.zeros((SC_LANES,), I32)

        lane = lax.iota(I32, SC_LANES)

        @plsc.parallel_loop(0, n // SC_LANES, unroll=8)
        def _(i):
            loc = slot_v[pl.ds(i * SC_LANES, SC_LANES)] - base
            mine = (loc >= 0) & (loc < rows)
            plsc.store_scatter(tok_v, [loc], i * SC_LANES + lane, mask=mine)

        _sc_gather_rows(hx_hbm, tok_v, out_hbm, base, rows // SC_ROWS, (rows_a, rows_b), (sem_a, sem_b))

    return k(hx, slot)


def _sc_combine(mo_sorted, slot):
    n = slot.shape[0]
    wrow = mo_sorted.shape[1]
    assert n % (SC_WORKERS * SC_ROWS) == 0
    toks = n // SC_WORKERS
    mesh = plsc.VectorSubcoreMesh(core_axis_name="c", subcore_axis_name="s")

    @functools.partial(
        pl.kernel, mesh=mesh,
        out_type=jax.ShapeDtypeStruct((n, wrow), mo_sorted.dtype),
        scratch_types=[pltpu.VMEM((toks,), I32),
                       pltpu.VMEM((SC_ROWS, wrow), mo_sorted.dtype),
                       pltpu.VMEM((SC_ROWS, wrow), mo_sorted.dtype),
                       pltpu.SemaphoreType.DMA,
                       pltpu.SemaphoreType.DMA],
        compiler_params=pltpu.CompilerParams(needs_layout_passes=False),
        name="sc_combine",
    )
    def k(mo_hbm, slot_hbm, out_hbm, idx_v, rows_a, rows_b, sem_a, sem_b):
        base = _sc_worker_id() * toks
        pltpu.sync_copy(slot_hbm.at[pl.ds(base, toks)], idx_v)
        _sc_gather_rows(mo_hbm, idx_v, out_hbm, base, toks // SC_ROWS, (rows_a, rows_b), (sem_a, sem_b))

    return k(mo_sorted, slot)


def _final_kernel(x1_ref, mo_ref, mod_ref, g_ref, o_ref):
    d = x1_ref.shape[2]
    b = pl.program_id(0)
    gt2 = mod_ref[pl.ds(b, 1), 5 * d:6 * d]
    mo = jnp.concatenate(_unpack_bf16_pair(mo_ref[0]), axis=1)
    o_ref[0] = _rms_norm(x1_ref[0] + gt2 * mo, g_ref[...])


def _final(x1, mo, mod, g):
    bsz, s, d = x1.shape
    t = T_TOK
    tok = lambda: pl.BlockSpec((1, t, d), lambda b, i: (b, i, 0))
    return pl.pallas_call(
        _final_kernel,
        grid=(bsz, s // t),
        in_specs=[tok(), pl.BlockSpec((1, t, d // 2), lambda b, i: (b, i, 0)),
                  pl.BlockSpec(mod.shape, lambda b, i: (0, 0)), pl.BlockSpec((1, d), lambda b, i: (0, 0))],
        out_specs=tok(),
        out_shape=jax.ShapeDtypeStruct((bsz, s, d), F32),
        compiler_params=_cparams(("arbitrary", "arbitrary")),
        name="final",
    )(x1, mo, mod, g)


def _pair_blockdiag(w):
    h, hd, _ = w.shape
    w2 = w.reshape(h // 2, 2, hd, hd)
    z = jnp.zeros((h // 2, hd, hd), w.dtype)
    top = jnp.concatenate([w2[:, 0], z], axis=2)
    bot = jnp.concatenate([z, w2[:, 1]], axis=2)
    return jnp.concatenate([top, bot], axis=1)


def _gate_params(w_r, b_r, w_i, b_i):
    wg = (0.5 * jnp.concatenate([_pair_blockdiag(w_r), _pair_blockdiag(w_i)], axis=2)).astype(BF16)
    h, hd = b_r.shape
    bg = 0.5 * jnp.concatenate([b_r.reshape(h // 2, 1, 2 * hd), b_i.reshape(h // 2, 1, 2 * hd)], axis=2)
    return wg, bg


def kernel(x, c, ctx, c_ctx, w_ada, b_ada, g_mix, w_in, b_in, conv_w, conv_b, ln_g, ln_b, w_pa, lru_conv_w, lru_conv_b, w_r_f, b_r_f, w_i_f, b_i_f, lam_f, w_r_b, b_r_b, w_i_b, b_i_b, lam_b, w_pb, w_o, g_ffn, w_grp, b_grp, w_er, b_er, w_gate, w_up, w_down, g_final):
    bsz, s, d = x.shape
    depth = w_ada.shape[0]
    assert depth == 1 and bsz + 1 <= 8
    l = 0
    n = bsz * s
    wc = conv_w.shape[2]
    wl = lru_conv_w.shape[2]
    rec0 = 2 * wc + wl
    ctx_row = bsz

    c8 = jnp.concatenate([c, c_ctx[None, :], jnp.zeros((8 - bsz - 1, d), F32)], axis=0)
    mod = _ada(c8, w_ada[l], b_ada[l][None, :])

    win = w_in[l].astype(BF16)
    bin_ = b_in[l][None, :]
    gmix = g_mix[l][None, :]
    lcw = lru_conv_w[l]
    lcb = lru_conv_b[l][None, :]
    wg_f, bg_f = _gate_params(w_r_f[l], b_r_f[l], w_i_f[l], b_i_f[l])
    wg_b, bg_b = _gate_params(w_r_b[l], b_r_b[l], w_i_b[l], b_i_b[l])
    lamf = lam_f[l][None, :]
    lamb = lam_b[l][None, :]

    xr_c = _ctxproj(ctx, mod, gmix, win[:, rec0:rec0 + wl], bin_[:, rec0:rec0 + wl], ctx_row)
    zero_state = jnp.zeros((bsz, SUBLANES, LANES), F32)
    _, h0f = _scan(xr_c, zero_state, lcw, lcb, wg_f, bg_f, lamf, reverse=False, emit_h=False)
    _, h0b = _scan(xr_c, zero_state, lcw, lcb, wg_b, bg_b, lamb, reverse=True, emit_h=False)

    ma, gyg, sgb, xr = _k1(x, mod, gmix, win, bin_, conv_w[l], conv_b[l][None, :], ln_g[l][None, :],
                           ln_b[l][None, :], w_pa[l].astype(BF16))
    hf, _ = _scan(xr, h0f, lcw, lcb, wg_f, bg_f, lamf, reverse=False, emit_h=True)
    hb, _ = _scan(xr, h0b, lcw, lcb, wg_b, bg_b, lamb, reverse=True, emit_h=True)

    n_rt = 32
    wrt = jnp.concatenate([w_grp[l].T, w_er[l].T, jnp.zeros((n_rt - N_GROUPS * (1 + EXPERTS_PER_GROUP), d), F32)], axis=0)
    brt = jnp.concatenate([b_grp[l], b_er[l], jnp.zeros((n_rt - N_GROUPS * (1 + EXPERTS_PER_GROUP),), F32)])
    brt = jnp.broadcast_to(brt[:, None], (n_rt, min(T_POST, s)))
    wrt_hi = wrt.astype(BF16)
    wrt = jnp.stack([wrt_hi, (wrt - wrt_hi.astype(F32)).astype(BF16)])
    x1, hx, rti = _post(x, hf, hb, gyg, ma, sgb, mod, w_pb[l].astype(BF16), w_o[l].astype(BF16),
                        g_ffn[l][None, :], wrt, brt)

    tm = T_MOE
    n_tiles = n // tm + N_BUCKETS
    npad = n_tiles * tm
    nt_lanes = ((n_tiles + LANES - 1) // LANES) * LANES
    slot2d, meta = _sort(rti[0].reshape(n // LANES, LANES), tm, nt_lanes)
    slot = slot2d.reshape(n)
    hs = _sc_dispatch(hx.reshape(n, hx.shape[2]), slot, npad)
    mo_sorted = _moe(meta[0, :n_tiles], meta[1, :n_tiles], meta[2, :n_tiles], hs,
                     w_gate[l].astype(BF16), w_up[l].astype(BF16), w_down[l].astype(BF16))
    mo = _sc_combine(mo_sorted, slot).reshape(bsz, s, d // 2)

    return _final(x1, mo, mod, g_final[None, :])
```

```python
import functools

import jax
import jax.numpy as jnp
from jax import lax
from jax.experimental import pallas as pl
from jax.experimental.pallas import tpu as pltpu
from jax.experimental.pallas import tpu_sc as plsc

F32 = jnp.float32
BF16 = jnp.bfloat16
I32 = jnp.int32
U32 = jnp.uint32
HIGHEST = lax.Precision.HIGHEST

EPS = 1e-6
GRID_W = 64
CONV_K = 31
CONV_PAD = 16
LRU_CONV_K = 4
LRU_C = 8.0
LRU_HEADS = 8
N_GROUPS = 4
EXPERTS_PER_GROUP = 4
N_PAIRS = 6
N_BUCKETS = N_GROUPS * N_PAIRS
LANES = 128
SUBLANES = 8
VMEM_LIMIT = 56 * 1024 * 1024

T_TOK = 256
T_POST = 512
T_SCAN = 256
T_MOE = 256


def _sigmoid(x):
    return 0.5 * (jnp.tanh(0.5 * x) + 1.0)


def _silu(x):
    return x * _sigmoid(x)


def _gelu_tanh(x):
    return 0.5 * x * (1.0 + jnp.tanh(0.7978845608028654 * (x + 0.044715 * (x * x * x))))


def _rms_norm(x, g):
    return x * lax.rsqrt(jnp.mean(x * x, axis=-1, keepdims=True) + EPS) * g


def _bdot(a, b):
    return jnp.dot(a, b, preferred_element_type=F32)


def _pack_bf16_pair(lo, hi):
    lo_bits = lax.bitcast_convert_type(lo.astype(BF16).astype(F32), U32)
    hi_bits = lax.bitcast_convert_type(hi.astype(BF16).astype(F32), U32)
    return (hi_bits & jnp.uint32(0xFFFF0000)) | (lo_bits >> 16)


def _unpack_bf16_pair(u):
    lo = lax.bitcast_convert_type(u << 16, F32)
    hi = lax.bitcast_convert_type(u & jnp.uint32(0xFFFF0000), F32)
    return lo, hi


def _cparams(sem):
    return pltpu.CompilerParams(dimension_semantics=sem, vmem_limit_bytes=VMEM_LIMIT)


def _ada_kernel(c_ref, w_ref, b_ref, o_ref):
    s = _silu(c_ref[...])
    o_ref[...] = jnp.dot(s, w_ref[...], precision=HIGHEST, preferred_element_type=F32) + b_ref[...]


def _ada(c8, w, b):
    d, n = w.shape
    blk = 1024
    return pl.pallas_call(
        _ada_kernel,
        grid=(n // blk,),
        in_specs=[pl.BlockSpec((8, d), lambda j: (0, 0)),
                  pl.BlockSpec((d, blk), lambda j: (0, j)),
                  pl.BlockSpec((1, blk), lambda j: (0, j))],
        out_specs=pl.BlockSpec((8, blk), lambda j: (0, j)),
        out_shape=jax.ShapeDtypeStruct((8, n), F32),
        compiler_params=_cparams(("arbitrary",)),
        name="ada",
    )(c8, w, b)


def _ctxproj_kernel(x_ref, mod_ref, g_ref, w_ref, b_ref, o_ref, *, row):
    d = x_ref.shape[-1]
    xn = _rms_norm(x_ref[0], g_ref[...])
    sh = mod_ref[row:row + 1, 0:d]
    sc = mod_ref[row:row + 1, d:2 * d]
    hx = (xn * (1.0 + sc) + sh).astype(BF16)
    o_ref[0] = _bdot(hx, w_ref[...]) + b_ref[...]


def _ctxproj(ctx, mod, g, w, b, row):
    bsz, n, d = ctx.shape
    return pl.pallas_call(
        functools.partial(_ctxproj_kernel, row=row),
        grid=(bsz,),
        in_specs=[pl.BlockSpec((1, n, d), lambda i: (i, 0, 0)),
                  pl.BlockSpec(mod.shape, lambda i: (0, 0)),
                  pl.BlockSpec((1, d), lambda i: (0, 0)),
                  pl.BlockSpec(w.shape, lambda i: (0, 0)),
                  pl.BlockSpec((1, w.shape[1]), lambda i: (0, 0))],
        out_specs=pl.BlockSpec((1, n, w.shape[1]), lambda i: (i, 0, 0)),
        out_shape=jax.ShapeDtypeStruct((bsz, n, w.shape[1]), F32),
        compiler_params=_cparams(("arbitrary",)),
        name="ctxproj",
    )(ctx, mod, g, w, b)


def _scan_kernel(xr_ref, prev_ref, next_ref, h0_ref, cw_ref, cb_ref, wg_ref, bg_ref, lam_ref, *rest,
                 reverse, emit_h, n_chunks):
    if emit_h:
        h_ref, hfin_ref, xp_ref, a_ref, b_ref, hst_ref = rest
    else:
        hfin_ref, xp_ref, a_ref, b_ref, hst_ref = rest
    bsz, tc, w = xr_ref.shape
    nblk = w // LANES
    i = pl.program_id(0)
    chunk = (n_chunks - 1 - i) if reverse else i

    @pl.when(i == 0)
    def _():
        hst_ref[...] = h0_ref[...]

    has_prev = chunk > 0
    has_next = chunk < n_chunks - 1
    z = -lam_ref[...]
    softplus = jnp.maximum(z, 0.0) + jnp.log1p(jnp.exp(-jnp.abs(z)))
    nlsh = (-0.5 * LRU_C) * softplus

    for b in range(bsz):
        bp, par = divmod(b, 2)
        for cblk in range(nblk):
            cs = slice(cblk * LANES, (cblk + 1) * LANES)
            xp_ref[bp, cblk, pl.ds(par, 8, stride=2), :] = jnp.where(has_prev, prev_ref[b, :, cs], 0.0)
            xp_ref[bp, cblk, pl.ds(16 + par, tc, stride=2), :] = xr_ref[b, :, cs]
            xp_ref[bp, cblk, pl.ds(16 + 2 * tc + par, 8, stride=2), :] = jnp.where(has_next, next_ref[b, :, cs], 0.0)

    for b in range(bsz):
        bp, par = divmod(b, 2)
        xc = []
        for cblk in range(nblk):
            cs = slice(cblk * LANES, (cblk + 1) * LANES)
            acc = cb_ref[:, cs] + cw_ref[0:1, cs] * xp_ref[bp, cblk, pl.ds(2 * 6 + par, tc, stride=2), :]
            for k in range(1, LRU_CONV_K):
                acc = acc + cw_ref[k:k + 1, cs] * xp_ref[bp, cblk, pl.ds(2 * (6 + k) + par, tc, stride=2), :]
            xc.append(acc)
        for p in range(LRU_HEADS // 2):
            cs = slice(2 * p * LANES, 2 * (p + 1) * LANES)
            xc_p = jnp.concatenate([xc[2 * p], xc[2 * p + 1]], axis=1)
            th = jnp.tanh(_bdot(xc_p.astype(BF16), wg_ref[p]) + bg_ref[p])
            log_a = nlsh[:, cs] * (th[:, 0:2 * LANES] + 1.0)
            a = jnp.exp(log_a)
            y = jnp.tanh(-log_a) * (1.0 + a * a)
            root = jnp.where(y > 0.0, y * lax.rsqrt(y), 0.0)
            bb = root * ((th[:, 2 * LANES:4 * LANES] + 1.0) * (0.5 * xc_p))
            for jj in range(2):
                j = 2 * p + jj
                a_ref[b, pl.ds(j, tc, stride=nblk), :] = a[:, jj * LANES:(jj + 1) * LANES]
                b_ref[b, pl.ds(j, tc, stride=nblk), :] = bb[:, jj * LANES:(jj + 1) * LANES]

    def step(s, hs):
        t = (tc - 1 - s) if reverse else s
        row = pl.multiple_of(t * nblk, nblk)
        out = []
        for b in range(bsz):
            h = a_ref[b, pl.ds(row, nblk), :] * hs[b] + b_ref[b, pl.ds(row, nblk), :]
            b_ref[b, pl.ds(row, nblk), :] = h
            out.append(h)
        return tuple(out)

    hs = lax.fori_loop(0, tc, step, tuple(hst_ref[b] for b in range(bsz)), unroll=8)
    for b in range(bsz):
        hst_ref[b] = hs[b]
        hfin_ref[b] = hs[b]
    if emit_h:
        for b in range(bsz):
            for j in range(nblk):
                h_ref[b, :, j * LANES:(j + 1) * LANES] = b_ref[b, pl.ds(j, tc, stride=nblk), :]


def _scan(xr, h0, cw, cb, wg, bg, lam, *, reverse, emit_h):
    bsz, s, w = xr.shape
    tc = min(T_SCAN, s)
    assert s % tc == 0 and w == SUBLANES * LANES and bsz % 2 == 0
    n_chunks = s // tc
    hb = tc // 8

    def cidx(i):
        return (n_chunks - 1 - i) if reverse else i

    in_specs = [
        pl.BlockSpec((bsz, tc, w), lambda i: (0, cidx(i), 0)),
        pl.BlockSpec((bsz, 8, w), lambda i: (0, jnp.maximum(cidx(i) * hb - 1, 0), 0)),
        pl.BlockSpec((bsz, 8, w), lambda i: (0, jnp.minimum((cidx(i) + 1) * hb, n_chunks * hb - 1), 0)),
        pl.BlockSpec((bsz, SUBLANES, LANES), lambda i: (0, 0, 0)),
        pl.BlockSpec(cw.shape, lambda i: (0, 0)),
        pl.BlockSpec(cb.shape, lambda i: (0, 0)),
        pl.BlockSpec(wg.shape, lambda i: (0, 0, 0)),
        pl.BlockSpec(bg.shape, lambda i: (0, 0, 0)),
        pl.BlockSpec(lam.shape, lambda i: (0, 0)),
    ]
    out_specs = [pl.BlockSpec((bsz, SUBLANES, LANES), lambda i: (0, 0, 0))]
    out_shape = [jax.ShapeDtypeStruct((bsz, SUBLANES, LANES), F32)]
    if emit_h:
        out_specs = [pl.BlockSpec((bsz, tc, w), lambda i: (0, cidx(i), 0))] + out_specs
        out_shape = [jax.ShapeDtypeStruct((bsz, s, w), F32)] + out_shape
    res = pl.pallas_call(
        functools.partial(_scan_kernel, reverse=reverse, emit_h=emit_h, n_chunks=n_chunks),
        grid=(n_chunks,),
        in_specs=in_specs,
        out_specs=out_specs,
        out_shape=out_shape,
        scratch_shapes=[pltpu.VMEM((bsz // 2, w // LANES, 2 * (tc + 16), LANES), F32),
                        pltpu.VMEM((bsz, tc * SUBLANES, LANES), F32),
                        pltpu.VMEM((bsz, tc * SUBLANES, LANES), F32),
                        pltpu.VMEM((bsz, SUBLANES, LANES), F32)],
        compiler_params=_cparams(("arbitrary",)),
        name="scan_bwd" if reverse else "scan_fwd",
    )(xr, xr, xr, h0, cw, cb, wg, bg, lam)
    return res if emit_h else (None, res[0])


CONV_ROWS = 4


def _k1_kernel(x_ref, mod_ref, gmix_ref, win_ref, bin_ref, cw_ref, cb_ref, lng_ref, lnb_ref, wpa_ref,
               ma_ref, gyg_ref, sgb_ref, xr_ref, zp_ref, wb_ref, de_ref, co_ref, sga_ref):
    t, d = x_ref.shape[1], x_ref.shape[2]
    wc = cw_ref.shape[1]
    n_cblk = wc // LANES
    n_quads = t // (CONV_ROWS * GRID_W)
    lo = CONV_PAD
    hi = CONV_PAD + GRID_W
    first = (pl.program_id(0) == 0) & (pl.program_id(1) == 0)

    @pl.when(first)
    def _():
        for k in range(CONV_K):
            for cblk in range(n_cblk):
                row = cw_ref[k:k + 1, cblk * LANES:(cblk + 1) * LANES]
                wb_ref[k, cblk] = jnp.broadcast_to(row, (2 * GRID_W, LANES)).astype(BF16)

    xn = _rms_norm(x_ref[0], gmix_ref[...])
    sh1 = mod_ref[0, :, 0:d]
    sc1 = mod_ref[0, :, d:2 * d]
    hx = (xn * (1.0 + sc1) + sh1).astype(BF16)

    def proj(c0, c1):
        return _bdot(hx, win_ref[:, c0:c1]) + bin_ref[:, c0:c1]

    zglu = proj(0, wc) * _sigmoid(proj(wc, 2 * wc))

    zp_ref[:, :, 0:2 * lo, :] = jnp.zeros((n_quads, n_cblk, 2 * lo, LANES), U32)
    zp_ref[:, :, 2 * hi:2 * (hi + CONV_PAD), :] = jnp.zeros((n_quads, n_cblk, 2 * CONV_PAD, LANES), U32)
    for q in range(n_quads):
        for par in range(2):
            r0 = (CONV_ROWS * q + 2 * par) * GRID_W
            for cblk in range(n_cblk):
                cs = slice(cblk * LANES, (cblk + 1) * LANES)
                zp_ref[q, cblk, pl.ds(2 * lo + par, GRID_W, stride=2), :] = _pack_bf16_pair(
                    zglu[r0:r0 + GRID_W, cs], zglu[r0 + GRID_W:r0 + 2 * GRID_W, cs])

    o = 2 * wc
    gyg_ref[0] = _gelu_tanh(proj(o, o + d)).astype(BF16)
    xr_ref[0] = proj(o + d, o + 2 * d)
    sga_ref[...] = _sigmoid(proj(o + 2 * d, o + 3 * d))
    sgb_ref[0] = _sigmoid(proj(o + 3 * d, o + 4 * d)).astype(BF16)

    def conv_block(cblk, carry):
        bias = cb_ref[cblk]
        for q in range(n_quads):
            for par in range(2):
                r0 = (CONV_ROWS * q + 2 * par) * GRID_W
                acc = jnp.zeros((2 * GRID_W, LANES), F32)
                for k in range(CONV_K):
                    off = lo - CONV_K // 2 + k
                    words = zp_ref[q, cblk, pl.ds(2 * off + par, GRID_W, stride=2), :]
                    acc = acc + pltpu.bitcast(words, BF16).astype(F32) * wb_ref[k, cblk].astype(F32)
                slot = 2 * q + par
                de_ref[slot] = acc
                co_ref[cblk, r0:r0 + GRID_W, :] = de_ref[slot, pl.ds(0, GRID_W, stride=2), :] + bias
                co_ref[cblk, r0 + GRID_W:r0 + 2 * GRID_W, :] = de_ref[slot, pl.ds(1, GRID_W, stride=2), :] + bias
        return carry

    lax.fori_loop(0, n_cblk, conv_block, 0)
    c = jnp.concatenate([co_ref[cblk] for cblk in range(n_cblk)], axis=1)
    mu = jnp.mean(c, axis=-1, keepdims=True)
    cc = c - mu
    y = cc * lax.rsqrt(jnp.mean(cc * cc, axis=-1, keepdims=True) + EPS) * lng_ref[...] + lnb_ref[...]
    a_lat = _bdot(_silu(y).astype(BF16), wpa_ref[...])
    ma_ref[0] = (sga_ref[...] * a_lat).astype(BF16)


def _k1(x, mod, gmix, win, bin_, cw, cb, lng, lnb, wpa):
    bsz, s, d = x.shape
    t = T_TOK
    assert s % t == 0 and t % (CONV_ROWS * GRID_W) == 0
    wc = cw.shape[1]
    n_cblk = wc // LANES
    n_quads = t // (CONV_ROWS * GRID_W)
    cb = cb.reshape(n_cblk, 1, LANES)

    def full(a):
        return pl.BlockSpec(a.shape, lambda b, i: (0,) * a.ndim)

    def const(a):
        return pl.BlockSpec(a.shape, lambda b, i: (0,) * a.ndim, pipeline_mode=pl.Buffered(1))

    tok = lambda: pl.BlockSpec((1, t, d), lambda b, i: (b, i, 0))
    return pl.pallas_call(
        _k1_kernel,
        grid=(bsz, s // t),
        in_specs=[tok(), pl.BlockSpec((1, 1, mod.shape[1]), lambda b, i: (b, 0, 0)), full(gmix), const(win),
                  full(bin_), full(cw), full(cb), full(lng), full(lnb), const(wpa)],
        out_specs=[tok(), tok(), tok(), tok()],
        out_shape=[jax.ShapeDtypeStruct((bsz, s, d), BF16),
                   jax.ShapeDtypeStruct((bsz, s, d), BF16),
                   jax.ShapeDtypeStruct((bsz, s, d), BF16),
                   jax.ShapeDtypeStruct((bsz, s, d), F32)],
        scratch_shapes=[pltpu.VMEM((n_quads, n_cblk, 2 * (GRID_W + 2 * CONV_PAD), LANES), U32),
                        pltpu.VMEM((CONV_K, n_cblk, 2 * GRID_W, LANES), BF16),
                        pltpu.VMEM((2 * n_quads, 2 * GRID_W, LANES), F32),
                        pltpu.VMEM((n_cblk, t, LANES), F32),
                        pltpu.VMEM((t, d), F32)],
        compiler_params=_cparams(("arbitrary", "arbitrary")),
        name="k1",
    )(x, mod[:, None, :], gmix, win, bin_, cw, cb, lng, lnb, wpa)


def _post_kernel(x_ref, hf_ref, hb_ref, gyg_ref, ma_ref, sgb_ref, mod_ref, wpb_ref, wo_ref, gffn_ref,
                 wrt_ref, brt_ref, x1_ref, hx_ref, rti_ref):
    d = x_ref.shape[2]
    b = pl.program_id(0)
    y_rec = hf_ref[0] + hb_ref[0]
    b_lat = _bdot((gyg_ref[0].astype(F32) * y_rec).astype(BF16), wpb_ref[...])
    merged = ma_ref[0].astype(F32) + sgb_ref[0].astype(F32) * b_lat
    res = _bdot(merged.astype(BF16), wo_ref[...])
    gt1 = mod_ref[pl.ds(b, 1), 2 * d:3 * d]
    x1 = x_ref[0] + gt1 * res
    x1_ref[0] = x1
    sh2 = mod_ref[pl.ds(b, 1), 3 * d:4 * d]
    sc2 = mod_ref[pl.ds(b, 1), 4 * d:5 * d]
    hm = _rms_norm(x1, gffn_ref[...]) * (1.0 + sc2) + sh2
    half = d // 2
    hx_ref[0, :, 0:half] = _pack_bf16_pair(hm[:, 0:half], hm[:, half:d])

    nt = (((1,), (1,)), ((), ()))
    hm_hi = hm.astype(BF16)
    hm_lo = (hm - hm_hi.astype(F32)).astype(BF16)
    lg = (lax.dot_general(wrt_ref[0], hm_hi, nt, preferred_element_type=F32)
          + lax.dot_general(wrt_ref[0], hm_lo, nt, preferred_element_type=F32)
          + lax.dot_general(wrt_ref[1], hm_hi, nt, preferred_element_type=F32)) + brt_ref[...]
    gl = [lg[k:k + 1, :] for k in range(N_GROUPS)]
    gmax = jnp.maximum(jnp.maximum(gl[0], gl[1]), jnp.maximum(gl[2], gl[3]))
    gsel = jnp.where(gl[0] == gmax, 0, jnp.where(gl[1] == gmax, 1, jnp.where(gl[2] == gmax, 2, 3)))
    p_grp = 1.0 / (jnp.exp(gl[0] - gmax) + jnp.exp(gl[1] - gmax) + jnp.exp(gl[2] - gmax) + jnp.exp(gl[3] - gmax))
    el = []
    for j in range(EXPERTS_PER_GROUP):
        rows = [lg[N_GROUPS + g * EXPERTS_PER_GROUP + j:N_GROUPS + g * EXPERTS_PER_GROUP + j + 1, :]
                for g in range(N_GROUPS)]
        el.append(jnp.where(gsel == 0, rows[0], jnp.where(gsel == 1, rows[1], jnp.where(gsel == 2, rows[2], rows[3]))))
    m1 = jnp.maximum(jnp.maximum(el[0], el[1]), jnp.maximum(el[2], el[3]))
    i1 = jnp.where(el[0] == m1, 0, jnp.where(el[1] == m1, 1, jnp.where(el[2] == m1, 2, 3)))
    neg = jnp.float32(-jnp.inf)
    rest = [jnp.where(i1 == j, neg, el[j]) for j in range(EXPERTS_PER_GROUP)]
    m2 = jnp.maximum(jnp.maximum(rest[0], rest[1]), jnp.maximum(rest[2], rest[3]))
    i2 = jnp.where((rest[0] == m2) & (i1 != 0), 0,
                   jnp.where((rest[1] == m2) & (i1 != 1), 1, jnp.where((rest[2] == m2) & (i1 != 2), 2, 3)))
    e21 = jnp.exp(m2 - m1)
    den = 1.0 / (1.0 + e21)
    w1 = p_grp * den
    w2 = p_grp * (e21 * den)
    first_lo = i1 < i2
    lo = jnp.where(first_lo, i1, i2)
    hi = jnp.where(first_lo, i2, i1)
    w_lo = jnp.where(first_lo, w1, w2)
    w_hi = jnp.where(first_lo, w2, w1)
    pair = jnp.where(lo == 0, hi - 1, jnp.where(lo == 1, hi + 1, 5))
    bucket = gsel * N_PAIRS + pair
    t = bucket.shape[1]
    rti_ref[...] = jnp.concatenate([bucket.astype(I32), jnp.zeros((7, t), I32)], axis=0)
    wrows = jnp.concatenate([w_lo, w_hi, jnp.zeros((LANES - 2, t), F32)], axis=0)
    eye = (lax.broadcasted_iota(I32, (t, t), 0) == lax.broadcasted_iota(I32, (t, t), 1)).astype(F32).astype(BF16)
    p1 = wrows.astype(BF16)
    r1 = wrows - p1.astype(F32)
    p2 = r1.astype(BF16)
    p3 = (r1 - p2.astype(F32)).astype(BF16)
    wcols = (lax.dot_general(eye, p1, nt, preferred_element_type=F32)
             + lax.dot_general(eye, p2, nt, preferred_element_type=F32)
             + lax.dot_general(eye, p3, nt, preferred_element_type=F32))
    hx_ref[0, :, half:half + LANES] = lax.bitcast_convert_type(wcols, U32)


def _post(x, hf, hb, gyg, ma, sgb, mod, wpb, wo, gffn, wrt, brt):
    bsz, s, d = x.shape
    t = brt.shape[1]
    assert s % t == 0
    nt = s // t
    n = bsz * s

    def full(a):
        return pl.BlockSpec(a.shape, lambda b, i: (0,) * a.ndim)

    def const(a):
        return pl.BlockSpec(a.shape, lambda b, i: (0,) * a.ndim, pipeline_mode=pl.Buffered(1))

    tok = lambda: pl.BlockSpec((1, t, d), lambda b, i: (b, i, 0))
    rt = lambda: pl.BlockSpec((8, t), lambda b, i: (0, b * nt + i))
    return pl.pallas_call(
        _post_kernel,
        grid=(bsz, nt),
        in_specs=[tok(), tok(), tok(), tok(), tok(), tok(), full(mod), const(wpb), const(wo), full(gffn), full(wrt),
                  full(brt)],
        out_specs=[tok(), pl.BlockSpec((1, t, d // 2 + LANES), lambda b, i: (b, i, 0)), rt()],
        out_shape=[jax.ShapeDtypeStruct((bsz, s, d), F32),
                   jax.ShapeDtypeStruct((bsz, s, d // 2 + LANES), U32),
                   jax.ShapeDtypeStruct((8, n), I32)],
        compiler_params=_cparams(("arbitrary", "arbitrary")),
        name="post",
    )(x, hf, hb, gyg, ma, sgb, mod, wpb, wo, gffn, wrt, brt)


def _sort_kernel(bk_ref, slot_ref, meta_ref, *, tm):
    bk = bk_ref[...]
    r = bk.shape[0]
    nt_lanes = meta_ref.shape[1]
    shift = tm.bit_length() - 1
    upper = (lax.broadcasted_iota(I32, (LANES, LANES), 0) < lax.broadcasted_iota(I32, (LANES, LANES), 1))
    upper = upper.astype(F32).astype(BF16)
    lower = (lax.broadcasted_iota(I32, (r, r), 1) < lax.broadcasted_iota(I32, (r, r), 0))
    lower = lower.astype(F32).astype(BF16)
    ones = jnp.ones((LANES, LANES), BF16)
    tile_pos = lax.broadcasted_iota(I32, (1, nt_lanes), 1) * tm
    start = jnp.zeros((1, LANES), I32)
    start_t = jnp.zeros((1, nt_lanes), I32)
    slot = jnp.zeros((r, LANES), I32)
    tile_bucket = jnp.zeros((1, nt_lanes), I32)
    for k in range(N_BUCKETS):
        m = bk == k
        mb = m.astype(F32).astype(BF16)
        pre = _bdot(mb, upper)
        tot = _bdot(mb, ones)
        rowpre = _bdot(lower, tot.astype(BF16))
        rank = (pre + rowpre).astype(I32)
        cnt = (rowpre[r - 1:r, :] + tot[r - 1:r, :]).astype(I32)
        slot = jnp.where(m, start + rank, slot)
        padded = ((cnt + (tm - 1)) >> shift) << shift
        start = start + padded
        start_t = start_t + jnp.concatenate([padded] * (nt_lanes // LANES), axis=1)
        tile_bucket = tile_bucket + (tile_pos >= start_t).astype(I32)
    slot_ref[...] = slot
    valid = (tile_pos < start_t).astype(I32)
    tb = jnp.minimum(tile_bucket, N_BUCKETS - 1)
    g = (tb * 43) >> 8
    pair = tb - N_PAIRS * g
    ge3 = (pair >= 3).astype(I32)
    ge5 = (pair >= 5).astype(I32)
    lo = ge3 + ge5
    hi = pair + 1 - 2 * ge3 - ge5
    e_lo = g * EXPERTS_PER_GROUP + lo
    e_hi = g * EXPERTS_PER_GROUP + hi
    meta_ref[...] = jnp.concatenate([e_lo, e_hi, valid, jnp.zeros((5, nt_lanes), I32)], axis=0)


def _sort(bucket2d, tm, nt_lanes):
    r = bucket2d.shape[0]
    return pl.pallas_call(
        functools.partial(_sort_kernel, tm=tm),
        grid=(1,),
        in_specs=[pl.BlockSpec((r, LANES), lambda i: (0, 0))],
        out_specs=[pl.BlockSpec((r, LANES), lambda i: (0, 0)),
                   pl.BlockSpec((8, nt_lanes), lambda i: (0, 0))],
        out_shape=[jax.ShapeDtypeStruct((r, LANES), I32),
                   jax.ShapeDtypeStruct((8, nt_lanes), I32)],
        compiler_params=_cparams(("arbitrary",)),
        name="bucket_sort",
    )(bucket2d)


def _moe_kernel(elo_ref, ehi_ref, valid_ref, h_ref, wg_lo, wg_hi, wu_lo, wu_hi, wd_lo, wd_hi, o_ref):
    i = pl.program_id(0)
    half = o_ref.shape[1]

    @pl.when(valid_ref[i] != 0)
    def _():
        h_a, h_b = _unpack_bf16_pair(h_ref[:, 0:half])
        h = jnp.concatenate([h_a, h_b], axis=1).astype(BF16)
        w_lo = lax.bitcast_convert_type(h_ref[:, half:half + 1], F32)
        w_hi = lax.bitcast_convert_type(h_ref[:, half + 1:half + 2], F32)
        act_lo = _silu(_bdot(h, wg_lo[0])) * _bdot(h, wu_lo[0]) * w_lo
        act_hi = _silu(_bdot(h, wg_hi[0])) * _bdot(h, wu_hi[0]) * w_hi
        o = _bdot(act_lo.astype(BF16), wd_lo[0]) + _bdot(act_hi.astype(BF16), wd_hi[0])
        o_ref[...] = _pack_bf16_pair(o[:, 0:half], o[:, half:2 * half])

    @pl.when(valid_ref[i] == 0)
    def _():
        o_ref[...] = jnp.zeros(o_ref.shape, U32)


def _moe(e_lo, e_hi, valid, hs, wgate, wup, wdown):
    npad, wrow = hs.shape
    tm = T_MOE
    n_tiles = npad // tm
    ne, d, f = wgate.shape
    lo3 = lambda i, elo, ehi, va: (elo[i], 0, 0)
    hi3 = lambda i, elo, ehi, va: (ehi[i], 0, 0)
    row = lambda i, elo, ehi, va: (i, 0)
    grid_spec = pltpu.PrefetchScalarGridSpec(
        num_scalar_prefetch=3,
        grid=(n_tiles,),
        in_specs=[pl.BlockSpec((tm, wrow), row),
                  pl.BlockSpec((1, d, f), lo3), pl.BlockSpec((1, d, f), hi3),
                  pl.BlockSpec((1, d, f), lo3), pl.BlockSpec((1, d, f), hi3),
                  pl.BlockSpec((1, f, d), lo3), pl.BlockSpec((1, f, d), hi3)],
        out_specs=pl.BlockSpec((tm, d // 2), row),
    )
    return pl.pallas_call(
        _moe_kernel,
        grid_spec=grid_spec,
        out_shape=jax.ShapeDtypeStruct((npad, d // 2), U32),
        compiler_params=_cparams(("arbitrary",)),
        name="moe",
    )(e_lo, e_hi, valid, hs, wgate, wgate, wup, wup, wdown, wdown)


SC_CORES = 2
SC_SUBCORES = 16
SC_LANES = 16
SC_WORKERS = SC_CORES * SC_SUBCORES
SC_ROWS = 64


def _sc_worker_id():
    return lax.axis_index("s") * SC_CORES + lax.axis_index("c")


def _sc_gather_rows(src_hbm, idx_v, out_hbm, out_base, n_chunks, bufs, sems):
    def gather(c, b):
        off = pl.multiple_of(c * SC_ROWS, SC_ROWS)
        return pltpu.make_async_copy(src_hbm.at[idx_v.at[pl.ds(off, SC_ROWS)]], bufs[b], sems[b])

    def finish(c, b):
        gather(c, b).wait()
        pltpu.sync_copy(bufs[b], out_hbm.at[pl.ds(out_base + pl.multiple_of(c * SC_ROWS, SC_ROWS), SC_ROWS)])

    gather(0, 0).start()

    @pl.loop(0, n_chunks // 2)
    def _(p):
        c = 2 * p
        gather(c + 1, 1).start()
        finish(c, 0)

        @pl.when(c + 2 < n_chunks)
        def _():
            gather(c + 2, 0).start()

        finish(c + 1, 1)

    if n_chunks % 2:
        finish(n_chunks - 1, 0)


def _sc_dispatch(hx, slot, npad):
    n, wrow = hx.shape
    assert npad % (SC_WORKERS * SC_ROWS) == 0 and n % SC_LANES == 0
    rows = npad // SC_WORKERS
    mesh = plsc.VectorSubcoreMesh(core_axis_name="c", subcore_axis_name="s")

    @functools.partial(
        pl.kernel, mesh=mesh,
        out_type=jax.ShapeDtypeStruct((npad, wrow), hx.dtype),
        scratch_types=[pltpu.VMEM((n,), I32),
                       pltpu.VMEM((rows,), I32),
                       pltpu.VMEM((SC_ROWS, wrow), hx.dtype),
                       pltpu.VMEM((SC_ROWS, wrow), hx.dtype),
                       pltpu.SemaphoreType.DMA,
                       pltpu.SemaphoreType.DMA],
        compiler_params=pltpu.CompilerParams(needs_layout_passes=False),
        name="sc_dispatch",
    )
    def k(hx_hbm, slot_hbm, out_hbm, slot_v, tok_v, rows_a, rows_b, sem_a, sem_b):
        base = _sc_worker_id() * rows
        pltpu.sync_copy(slot_hbm, slot_v)

        lane = lax.iota(I32, SC_LANES)

        @pl.loop(0, rows // SC_LANES)
        def _(i):
            filler = base + i * SC_LANES + lane
            tok_v[pl.ds(i * SC_LANES, SC_LANES)] = lax.rem(filler, jnp.int32(n))

        @plsc.parallel_loop(0, n // SC_LANES, unroll=8)
        def _(i):
            loc = slot_v[pl.ds(i * SC_LANES, SC_LANES)] - base
            mine = (loc >= 0) & (loc < rows)
            plsc.store_scatter(tok_v, [loc], i * SC_LANES + lane, mask=mine)

        _sc_gather_rows(hx_hbm, tok_v, out_hbm, base, rows // SC_ROWS, (rows_a, rows_b), (sem_a, sem_b))

    return k(hx, slot)


def _sc_combine(mo_sorted, slot):
    n = slot.shape[0]
    wrow = mo_sorted.shape[1]
    assert n % (SC_WORKERS * SC_ROWS) == 0
    toks = n // SC_WORKERS
    mesh = plsc.VectorSubcoreMesh(core_axis_name="c", subcore_axis_name="s")

    @functools.partial(
        pl.kernel, mesh=mesh,
        out_type=jax.ShapeDtypeStruct((n, wrow), mo_sorted.dtype),
        scratch_types=[pltpu.VMEM((toks,), I32),
                       pltpu.VMEM((SC_ROWS, wrow), mo_sorted.dtype),
                       pltpu.VMEM((SC_ROWS, wrow), mo_sorted.dtype),
                       pltpu.SemaphoreType.DMA,
                       pltpu.SemaphoreType.DMA],
        compiler_params=pltpu.CompilerParams(needs_layout_passes=False),
        name="sc_combine",
    )
    def k(mo_hbm, slot_hbm, out_hbm, idx_v, rows_a, rows_b, sem_a, sem_b):
        base = _sc_worker_id() * toks
        pltpu.sync_copy(slot_hbm.at[pl.ds(base, toks)], idx_v)
        _sc_gather_rows(mo_hbm, idx_v, out_hbm, base, toks // SC_ROWS, (rows_a, rows_b), (sem_a, sem_b))

    return k(mo_sorted, slot)


def _final_kernel(x1_ref, mo_ref, mod_ref, g_ref, o_ref):
    d = x1_ref.shape[2]
    b = pl.program_id(0)
    gt2 = mod_ref[pl.ds(b, 1), 5 * d:6 * d]
    mo = jnp.concatenate(_unpack_bf16_pair(mo_ref[0]), axis=1)
    o_ref[0] = _rms_norm(x1_ref[0] + gt2 * mo, g_ref[...])


def _final(x1, mo, mod, g):
    bsz, s, d = x1.shape
    t = T_TOK
    tok = lambda: pl.BlockSpec((1, t, d), lambda b, i: (b, i, 0))
    return pl.pallas_call(
        _final_kernel,
        grid=(bsz, s // t),
        in_specs=[tok(), pl.BlockSpec((1, t, d // 2), lambda b, i: (b, i, 0)),
                  pl.BlockSpec(mod.shape, lambda b, i: (0, 0)), pl.BlockSpec((1, d), lambda b, i: (0, 0))],
        out_specs=tok(),
        out_shape=jax.ShapeDtypeStruct((bsz, s, d), F32),
        compiler_params=_cparams(("arbitrary", "arbitrary")),
        name="final",
    )(x1, mo, mod, g)


def _pair_blockdiag(w):
    h, hd, _ = w.shape
    w2 = w.reshape(h // 2, 2, hd, hd)
    z = jnp.zeros((h // 2, hd, hd), w.dtype)
    top = jnp.concatenate([w2[:, 0], z], axis=2)
    bot = jnp.concatenate([z, w2[:, 1]], axis=2)
    return jnp.concatenate([top, bot], axis=1)


def _gate_params(w_r, b_r, w_i, b_i):
    wg = (0.5 * jnp.concatenate([_pair_blockdiag(w_r), _pair_blockdiag(w_i)], axis=2)).astype(BF16)
    h, hd = b_r.shape
    bg = 0.5 * jnp.concatenate([b_r.reshape(h // 2, 1, 2 * hd), b_i.reshape(h // 2, 1, 2 * hd)], axis=2)
    return wg, bg


def kernel(x, c, ctx, c_ctx, w_ada, b_ada, g_mix, w_in, b_in, conv_w, conv_b, ln_g, ln_b, w_pa, lru_conv_w, lru_conv_b, w_r_f, b_r_f, w_i_f, b_i_f, lam_f, w_r_b, b_r_b, w_i_b, b_i_b, lam_b, w_pb, w_o, g_ffn, w_grp, b_grp, w_er, b_er, w_gate, w_up, w_down, g_final):
    bsz, s, d = x.shape
    depth = w_ada.shape[0]
    assert depth == 1 and bsz + 1 <= 8
    l = 0
    n = bsz * s
    wc = conv_w.shape[2]
    wl = lru_conv_w.shape[2]
    rec0 = 2 * wc + wl
    ctx_row = bsz

    c8 = jnp.concatenate([c, c_ctx[None, :], jnp.zeros((8 - bsz - 1, d), F32)], axis=0)
    mod = _ada(c8, w_ada[l], b_ada[l][None, :])

    win = w_in[l].astype(BF16)
    bin_ = b_in[l][None, :]
    gmix = g_mix[l][None, :]
    lcw = lru_conv_w[l]
    lcb = lru_conv_b[l][None, :]
    wg_f, bg_f = _gate_params(w_r_f[l], b_r_f[l], w_i_f[l], b_i_f[l])
    wg_b, bg_b = _gate_params(w_r_b[l], b_r_b[l], w_i_b[l], b_i_b[l])
    lamf = lam_f[l][None, :]
    lamb = lam_b[l][None, :]

    xr_c = _ctxproj(ctx, mod, gmix, win[:, rec0:rec0 + wl], bin_[:, rec0:rec0 + wl], ctx_row)
    zero_state = jnp.zeros((bsz, SUBLANES, LANES), F32)
    _, h0f = _scan(xr_c, zero_state, lcw, lcb, wg_f, bg_f, lamf, reverse=False, emit_h=False)
    _, h0b = _scan(xr_c, zero_state, lcw, lcb, wg_b, bg_b, lamb, reverse=True, emit_h=False)

    ma, gyg, sgb, xr = _k1(x, mod, gmix, win, bin_, conv_w[l], conv_b[l][None, :], ln_g[l][None, :],
                           ln_b[l][None, :], w_pa[l].astype(BF16))
    hf, _ = _scan(xr, h0f, lcw, lcb, wg_f, bg_f, lamf, reverse=False, emit_h=True)
    hb, _ = _scan(xr, h0b, lcw, lcb, wg_b, bg_b, lamb, reverse=True, emit_h=True)

    n_rt = 32
    wrt = jnp.concatenate([w_grp[l].T, w_er[l].T, jnp.zeros((n_rt - N_GROUPS * (1 + EXPERTS_PER_GROUP), d), F32)], axis=0)
    brt = jnp.concatenate([b_grp[l], b_er[l], jnp.zeros((n_rt - N_GROUPS * (1 + EXPERTS_PER_GROUP),), F32)])
    brt = jnp.broadcast_to(brt[:, None], (n_rt, min(T_POST, s)))
    wrt_hi = wrt.astype(BF16)
    wrt = jnp.stack([wrt_hi, (wrt - wrt_hi.astype(F32)).astype(BF16)])
    x1, hx, rti = _post(x, hf, hb, gyg, ma, sgb, mod, w_pb[l].astype(BF16), w_o[l].astype(BF16),
                        g_ffn[l][None, :], wrt, brt)

    tm = T_MOE
    n_tiles = n // tm + N_BUCKETS
    npad = n_tiles * tm
    nt_lanes = ((n_tiles + LANES - 1) // LANES) * LANES
    slot2d, meta = _sort(rti[0].reshape(n // LANES, LANES), tm, nt_lanes)
    slot = slot2d.reshape(n)
    hs = _sc_dispatch(hx.reshape(n, hx.shape[2]), slot, npad)
    mo_sorted = _moe(meta[0, :n_tiles], meta[1, :n_tiles], meta[2, :n_tiles], hs,
                     w_gate[l].astype(BF16), w_up[l].astype(BF16), w_down[l].astype(BF16))
    mo = _sc_combine(mo_sorted, slot).reshape(bsz, s, d // 2)

    return _final(x1, mo, mod, g_final[None, :])
```

```python
import functools

import jax
import jax.numpy as jnp
from jax import lax
from jax.experimental import pallas as pl
from jax.experimental.pallas import tpu as pltpu
from jax.experimental.pallas import tpu_sc as plsc

F32 = jnp.float32
BF16 = jnp.bfloat16
I32 = jnp.int32
U32 = jnp.uint32
HIGHEST = lax.Precision.HIGHEST

EPS = 1e-6
GRID_W = 64
CONV_K = 31
CONV_PAD = 16
LRU_CONV_K = 4
LRU_C = 8.0
LRU_HEADS = 8
N_GROUPS = 4
EXPERTS_PER_GROUP = 4
N_PAIRS = 6
N_BUCKETS = N_GROUPS * N_PAIRS
LANES = 128
SUBLANES = 8
VMEM_LIMIT = 56 * 1024 * 1024

T_TOK = 512
T_POST = 512
T_FINAL = 1024
T_SCAN = 256
T_MOE = 256


def _sigmoid(x):
    return 0.5 * (jnp.tanh(0.5 * x) + 1.0)


def _silu(x):
    return x * _sigmoid(x)


def _gelu_tanh(x):
    return 0.5 * x * (1.0 + jnp.tanh(0.7978845608028654 * (x + 0.044715 * (x * x * x))))


def _rms_norm(x, g):
    return x * lax.rsqrt(jnp.mean(x * x, axis=-1, keepdims=True) + EPS) * g


def _bdot(a, b):
    return jnp.dot(a, b, preferred_element_type=F32)


def _pack_bf16_pair(lo, hi):
    lo_bits = lax.bitcast_convert_type(lo.astype(BF16).astype(F32), U32)
    hi_bits = lax.bitcast_convert_type(hi.astype(BF16).astype(F32), U32)
    return (hi_bits & jnp.uint32(0xFFFF0000)) | (lo_bits >> 16)


def _unpack_bf16_pair(u):
    lo = lax.bitcast_convert_type(u << 16, F32)
    hi = lax.bitcast_convert_type(u & jnp.uint32(0xFFFF0000), F32)
    return lo, hi


def _cparams(sem):
    return pltpu.CompilerParams(dimension_semantics=sem, vmem_limit_bytes=VMEM_LIMIT)


def _ada_kernel(c_ref, w_ref, b_ref, o_ref):
    s = _silu(c_ref[...])
    o_ref[...] = jnp.dot(s, w_ref[...], precision=HIGHEST, preferred_element_type=F32) + b_ref[...]


def _ada(c8, w, b):
    d, n = w.shape
    blk = 1024
    return pl.pallas_call(
        _ada_kernel,
        grid=(n // blk,),
        in_specs=[pl.BlockSpec((8, d), lambda j: (0, 0)),
                  pl.BlockSpec((d, blk), lambda j: (0, j)),
                  pl.BlockSpec((1, blk), lambda j: (0, j))],
        out_specs=pl.BlockSpec((8, blk), lambda j: (0, j)),
        out_shape=jax.ShapeDtypeStruct((8, n), F32),
        compiler_params=_cparams(("arbitrary",)),
        name="ada",
    )(c8, w, b)


def _ctxproj_kernel(x_ref, mod_ref, g_ref, w_ref, b_ref, o_ref, *, row):
    d = x_ref.shape[-1]
    xn = _rms_norm(x_ref[0], g_ref[...])
    sh = mod_ref[row:row + 1, 0:d]
    sc = mod_ref[row:row + 1, d:2 * d]
    hx = (xn * (1.0 + sc) + sh).astype(BF16)
    o_ref[0] = _bdot(hx, w_ref[...]) + b_ref[...]


def _ctxproj(ctx, mod, g, w, b, row):
    bsz, n, d = ctx.shape
    return pl.pallas_call(
        functools.partial(_ctxproj_kernel, row=row),
        grid=(bsz,),
        in_specs=[pl.BlockSpec((1, n, d), lambda i: (i, 0, 0)),
                  pl.BlockSpec(mod.shape, lambda i: (0, 0)),
                  pl.BlockSpec((1, d), lambda i: (0, 0)),
                  pl.BlockSpec(w.shape, lambda i: (0, 0)),
                  pl.BlockSpec((1, w.shape[1]), lambda i: (0, 0))],
        out_specs=pl.BlockSpec((1, n, w.shape[1]), lambda i: (i, 0, 0)),
        out_shape=jax.ShapeDtypeStruct((bsz, n, w.shape[1]), F32),
        compiler_params=_cparams(("arbitrary",)),
        name="ctxproj",
    )(ctx, mod, g, w, b)


def _scan_kernel(*refs, reverse, emit_h, n_chunks):
    refs = list(refs)
    if reverse:
        xc_ref, h0_ref, wg_ref, bg_ref, lam_ref = refs[:5]
        rest = refs[5:]
        bsz, tc, w = xc_ref.shape
    else:
        xr_ref, prev_ref, next_ref, h0_ref, cw_ref, cb_ref, wg_ref, bg_ref, lam_ref = refs[:9]
        rest = refs[9:]
        bsz, tc, w = xr_ref.shape
    h_ref = rest.pop(0) if emit_h else None
    hfin_ref = rest.pop(0)
    xc_out_ref = None if reverse else rest.pop(0)
    xp_ref = None if reverse else rest.pop(0)
    a_ref, b_ref, hs_ref, hst_ref = rest
    nblk = w // LANES
    i = pl.program_id(0)
    chunk = (n_chunks - 1 - i) if reverse else i

    @pl.when(i == 0)
    def _():
        hst_ref[...] = h0_ref[...]

    z = -lam_ref[...]
    softplus = jnp.maximum(z, 0.0) + jnp.log1p(jnp.exp(-jnp.abs(z)))
    nlsh = (-0.5 * LRU_C) * softplus

    if not reverse:
        has_prev = chunk > 0
        has_next = chunk < n_chunks - 1
        for b in range(bsz):
            bp, par = divmod(b, 2)
            for cblk in range(nblk):
                cs = slice(cblk * LANES, (cblk + 1) * LANES)
                xp_ref[bp, cblk, pl.ds(par, 8, stride=2), :] = jnp.where(has_prev, prev_ref[b, :, cs], 0.0)
                xp_ref[bp, cblk, pl.ds(16 + par, tc, stride=2), :] = xr_ref[b, :, cs]
                xp_ref[bp, cblk, pl.ds(16 + 2 * tc + par, 8, stride=2), :] = (
                    jnp.where(has_next, next_ref[b, :, cs], 0.0))

    for b in range(bsz):
        bp, par = divmod(b, 2)
        xc = []
        for cblk in range(nblk):
            cs = slice(cblk * LANES, (cblk + 1) * LANES)
            if reverse:
                xc.append(xc_ref[b, :, cs])
                continue
            acc = cb_ref[:, cs] + cw_ref[0:1, cs] * xp_ref[bp, cblk, pl.ds(2 * 6 + par, tc, stride=2), :]
            for k in range(1, LRU_CONV_K):
                acc = acc + cw_ref[k:k + 1, cs] * xp_ref[bp, cblk, pl.ds(2 * (6 + k) + par, tc, stride=2), :]
            xc_out_ref[b, :, cs] = acc
            xc.append(acc)
        for p in range(LRU_HEADS // 2):
            cs = slice(2 * p * LANES, 2 * (p + 1) * LANES)
            xc_p = jnp.concatenate([xc[2 * p], xc[2 * p + 1]], axis=1)
            th = jnp.tanh(_bdot(xc_p.astype(BF16), wg_ref[p]) + bg_ref[p])
            log_a = nlsh[:, cs] * (th[:, 0:2 * LANES] + 1.0)
            a = jnp.exp(log_a)
            y = jnp.tanh(-log_a) * (1.0 + a * a)
            root = jnp.where(y > 0.0, y * lax.rsqrt(y), 0.0)
            bb = root * ((th[:, 2 * LANES:4 * LANES] + 1.0) * (0.5 * xc_p))
            for jj in range(2):
                j = 2 * p + jj
                a_ref[b, pl.ds(j, tc, stride=nblk), :] = a[:, jj * LANES:(jj + 1) * LANES]
                b_ref[b, pl.ds(j, tc, stride=nblk), :] = bb[:, jj * LANES:(jj + 1) * LANES]

    def step(s, hs):
        t = (tc - 1 - s) if reverse else s
        row = pl.multiple_of(t * nblk, nblk)
        out = []
        for b in range(bsz):
            h = a_ref[b, pl.ds(row, nblk), :] * hs[b] + b_ref[b, pl.ds(row, nblk), :]
            hs_ref[b, pl.ds(row, nblk), :] = h
            out.append(h)
        return tuple(out)

    hs = lax.fori_loop(0, tc, step, tuple(hst_ref[b] for b in range(bsz)), unroll=8)
    for b in range(bsz):
        hst_ref[b] = hs[b]
        hfin_ref[b] = hs[b]
    if emit_h:
        for b in range(bsz):
            for j in range(nblk):
                h_ref[b, :, j * LANES:(j + 1) * LANES] = hs_ref[b, pl.ds(j, tc, stride=nblk), :]


def _scan(xin, h0, cw, cb, wg, bg, lam, *, reverse, emit_h):
    bsz, s, w = xin.shape
    tc = min(T_SCAN, s)
    assert s % tc == 0 and w == SUBLANES * LANES and bsz % 2 == 0
    n_chunks = s // tc
    hb = tc // 8

    def cidx(i):
        return (n_chunks - 1 - i) if reverse else i

    tile = lambda: pl.BlockSpec((bsz, tc, w), lambda i: (0, cidx(i), 0))
    state = lambda: pl.BlockSpec((bsz, SUBLANES, LANES), lambda i: (0, 0, 0))
    gate_specs = [pl.BlockSpec(wg.shape, lambda i: (0, 0, 0)),
                  pl.BlockSpec(bg.shape, lambda i: (0, 0, 0)),
                  pl.BlockSpec(lam.shape, lambda i: (0, 0))]
    if reverse:
        args = (xin, h0, wg, bg, lam)
        in_specs = [tile(), state()] + gate_specs
    else:
        args = (xin, xin, xin, h0, cw, cb, wg, bg, lam)
        in_specs = [tile(),
                    pl.BlockSpec((bsz, 8, w), lambda i: (0, jnp.maximum(cidx(i) * hb - 1, 0), 0)),
                    pl.BlockSpec((bsz, 8, w), lambda i: (0, jnp.minimum((cidx(i) + 1) * hb, n_chunks * hb - 1), 0)),
                    state(),
                    pl.BlockSpec(cw.shape, lambda i: (0, 0)),
                    pl.BlockSpec(cb.shape, lambda i: (0, 0))] + gate_specs
    full_seq = jax.ShapeDtypeStruct((bsz, s, w), F32)
    out_specs = [state()]
    out_shape = [jax.ShapeDtypeStruct((bsz, SUBLANES, LANES), F32)]
    if emit_h:
        out_specs = [tile()] + out_specs
        out_shape = [full_seq] + out_shape
    scratch = [pltpu.VMEM((bsz, tc * SUBLANES, LANES), F32),
               pltpu.VMEM((bsz, tc * SUBLANES, LANES), F32),
               pltpu.VMEM((bsz, tc * SUBLANES, LANES), F32),
               pltpu.VMEM((bsz, SUBLANES, LANES), F32)]
    if not reverse:
        out_specs = out_specs + [tile()]
        out_shape = out_shape + [full_seq]
        scratch = [pltpu.VMEM((bsz // 2, w // LANES, 2 * (tc + 16), LANES), F32)] + scratch
    res = pl.pallas_call(
        functools.partial(_scan_kernel, reverse=reverse, emit_h=emit_h, n_chunks=n_chunks),
        grid=(n_chunks,),
        in_specs=in_specs,
        out_specs=out_specs,
        out_shape=out_shape,
        scratch_shapes=scratch,
        compiler_params=_cparams(("arbitrary",)),
        name="scan_bwd" if reverse else "scan_fwd",
    )(*args)
    return tuple(res) if emit_h else (None,) + tuple(res)


CONV_ROWS = 4


def _k1_kernel(x_ref, mod_ref, gmix_ref, win_ref, bin_ref, cw_ref, cb_ref, lng_ref, lnb_ref, wpa_ref,
               ma_ref, gyg_ref, sgb_ref, xr_ref, zp_ref, wb_ref, de_ref, co_ref, sga_ref):
    t, d = x_ref.shape[1], x_ref.shape[2]
    wc = cw_ref.shape[1]
    n_cblk = wc // LANES
    n_quads = t // (CONV_ROWS * GRID_W)
    lo = CONV_PAD
    hi = CONV_PAD + GRID_W
    first = (pl.program_id(0) == 0) & (pl.program_id(1) == 0)

    @pl.when(first)
    def _():
        for k in range(CONV_K):
            for cblk in range(n_cblk):
                row = cw_ref[k:k + 1, cblk * LANES:(cblk + 1) * LANES]
                wb_ref[k, cblk] = jnp.broadcast_to(row, (2 * GRID_W, LANES)).astype(BF16)

    xn = _rms_norm(x_ref[0], gmix_ref[...])
    sh1 = mod_ref[0, :, 0:d]
    sc1 = mod_ref[0, :, d:2 * d]
    hx = (xn * (1.0 + sc1) + sh1).astype(BF16)

    def proj(c0, c1):
        return _bdot(hx, win_ref[:, c0:c1]) + bin_ref[:, c0:c1]

    zglu = proj(0, wc) * _sigmoid(proj(wc, 2 * wc))

    zp_ref[:, :, 0:2 * lo, :] = jnp.zeros((n_quads, n_cblk, 2 * lo, LANES), U32)
    zp_ref[:, :, 2 * hi:2 * (hi + CONV_PAD), :] = jnp.zeros((n_quads, n_cblk, 2 * CONV_PAD, LANES), U32)
    for q in range(n_quads):
        for par in range(2):
            r0 = (CONV_ROWS * q + 2 * par) * GRID_W
            for cblk in range(n_cblk):
                cs = slice(cblk * LANES, (cblk + 1) * LANES)
                zp_ref[q, cblk, pl.ds(2 * lo + par, GRID_W, stride=2), :] = _pack_bf16_pair(
                    zglu[r0:r0 + GRID_W, cs], zglu[r0 + GRID_W:r0 + 2 * GRID_W, cs])

    o = 2 * wc
    gyg_ref[0] = _gelu_tanh(proj(o, o + d)).astype(BF16)
    xr_ref[0] = proj(o + d, o + 2 * d)
    sga_ref[...] = _sigmoid(proj(o + 2 * d, o + 3 * d))
    sgb_ref[0] = _sigmoid(proj(o + 3 * d, o + 4 * d)).astype(BF16)

    def conv_block(cblk, carry):
        bias = cb_ref[cblk]
        for q in range(n_quads):
            for par in range(2):
                r0 = (CONV_ROWS * q + 2 * par) * GRID_W
                acc = jnp.zeros((2 * GRID_W, LANES), F32)
                for k in range(CONV_K):
                    off = lo - CONV_K // 2 + k
                    words = zp_ref[q, cblk, pl.ds(2 * off + par, GRID_W, stride=2), :]
                    acc = acc + pltpu.bitcast(words, BF16).astype(F32) * wb_ref[k, cblk].astype(F32)
                slot = 2 * q + par
                de_ref[slot] = acc
                co_ref[cblk, r0:r0 + GRID_W, :] = de_ref[slot, pl.ds(0, GRID_W, stride=2), :] + bias
                co_ref[cblk, r0 + GRID_W:r0 + 2 * GRID_W, :] = de_ref[slot, pl.ds(1, GRID_W, stride=2), :] + bias
        return carry

    lax.fori_loop(0, n_cblk, conv_block, 0)
    c = jnp.concatenate([co_ref[cblk] for cblk in range(n_cblk)], axis=1)
    mu = jnp.mean(c, axis=-1, keepdims=True)
    cc = c - mu
    y = cc * lax.rsqrt(jnp.mean(cc * cc, axis=-1, keepdims=True) + EPS) * lng_ref[...] + lnb_ref[...]
    a_lat = _bdot(_silu(y).astype(BF16), wpa_ref[...])
    ma_ref[0] = (sga_ref[...] * a_lat).astype(BF16)


def _k1(x, mod, gmix, win, bin_, cw, cb, lng, lnb, wpa):
    bsz, s, d = x.shape
    t = T_TOK
    assert s % t == 0 and t % (CONV_ROWS * GRID_W) == 0
    wc = cw.shape[1]
    n_cblk = wc // LANES
    n_quads = t // (CONV_ROWS * GRID_W)
    cb = cb.reshape(n_cblk, 1, LANES)

    def full(a):
        return pl.BlockSpec(a.shape, lambda b, i: (0,) * a.ndim)

    def const(a):
        return pl.BlockSpec(a.shape, lambda b, i: (0,) * a.ndim, pipeline_mode=pl.Buffered(1))

    tok = lambda: pl.BlockSpec((1, t, d), lambda b, i: (b, i, 0))
    return pl.pallas_call(
        _k1_kernel,
        grid=(bsz, s // t),
        in_specs=[tok(), pl.BlockSpec((1, 1, mod.shape[1]), lambda b, i: (b, 0, 0)), full(gmix), const(win),
                  full(bin_), full(cw), full(cb), full(lng), full(lnb), const(wpa)],
        out_specs=[tok(), tok(), tok(), tok()],
        out_shape=[jax.ShapeDtypeStruct((bsz, s, d), BF16),
                   jax.ShapeDtypeStruct((bsz, s, d), BF16),
                   jax.ShapeDtypeStruct((bsz, s, d), BF16),
                   jax.ShapeDtypeStruct((bsz, s, d), F32)],
        scratch_shapes=[pltpu.VMEM((n_quads, n_cblk, 2 * (GRID_W + 2 * CONV_PAD), LANES), U32),
                        pltpu.VMEM((CONV_K, n_cblk, 2 * GRID_W, LANES), BF16),
                        pltpu.VMEM((2 * n_quads, 2 * GRID_W, LANES), F32),
                        pltpu.VMEM((n_cblk, t, LANES), F32),
                        pltpu.VMEM((t, d), F32)],
        compiler_params=_cparams(("arbitrary", "arbitrary")),
        name="k1",
    )(x, mod[:, None, :], gmix, win, bin_, cw, cb, lng, lnb, wpa)


def _post_kernel(x_ref, hf_ref, hb_ref, gyg_ref, ma_ref, sgb_ref, mod_ref, wpb_ref, wo_ref, gffn_ref,
                 wrt_ref, brt_ref, x1_ref, hx_ref, rti_ref):
    d = x_ref.shape[2]
    b = pl.program_id(0)
    y_rec = hf_ref[0] + hb_ref[0]
    b_lat = _bdot((gyg_ref[0].astype(F32) * y_rec).astype(BF16), wpb_ref[...])
    merged = ma_ref[0].astype(F32) + sgb_ref[0].astype(F32) * b_lat
    res = _bdot(merged.astype(BF16), wo_ref[...])
    gt1 = mod_ref[pl.ds(b, 1), 2 * d:3 * d]
    x1 = x_ref[0] + gt1 * res
    x1_ref[0] = x1
    sh2 = mod_ref[pl.ds(b, 1), 3 * d:4 * d]
    sc2 = mod_ref[pl.ds(b, 1), 4 * d:5 * d]
    hm = _rms_norm(x1, gffn_ref[...]) * (1.0 + sc2) + sh2
    half = d // 2
    hx_ref[0, :, 0:half] = _pack_bf16_pair(hm[:, 0:half], hm[:, half:d])

    nt = (((1,), (1,)), ((), ()))
    hm_hi = hm.astype(BF16)
    hm_lo = (hm - hm_hi.astype(F32)).astype(BF16)
    lg = (lax.dot_general(wrt_ref[0], hm_hi, nt, preferred_element_type=F32)
          + lax.dot_general(wrt_ref[0], hm_lo, nt, preferred_element_type=F32)
          + lax.dot_general(wrt_ref[1], hm_hi, nt, preferred_element_type=F32)) + brt_ref[...]
    gl = [lg[k:k + 1, :] for k in range(N_GROUPS)]
    gmax = jnp.maximum(jnp.maximum(gl[0], gl[1]), jnp.maximum(gl[2], gl[3]))
    gsel = jnp.where(gl[0] == gmax, 0, jnp.where(gl[1] == gmax, 1, jnp.where(gl[2] == gmax, 2, 3)))
    p_grp = 1.0 / (jnp.exp(gl[0] - gmax) + jnp.exp(gl[1] - gmax) + jnp.exp(gl[2] - gmax) + jnp.exp(gl[3] - gmax))
    el = []
    for j in range(EXPERTS_PER_GROUP):
        rows = [lg[N_GROUPS + g * EXPERTS_PER_GROUP + j:N_GROUPS + g * EXPERTS_PER_GROUP + j + 1, :]
                for g in range(N_GROUPS)]
        el.append(jnp.where(gsel == 0, rows[0], jnp.where(gsel == 1, rows[1], jnp.where(gsel == 2, rows[2], rows[3]))))
    m1 = jnp.maximum(jnp.maximum(el[0], el[1]), jnp.maximum(el[2], el[3]))
    i1 = jnp.where(el[0] == m1, 0, jnp.where(el[1] == m1, 1, jnp.where(el[2] == m1, 2, 3)))
    neg = jnp.float32(-jnp.inf)
    rest = [jnp.where(i1 == j, neg, el[j]) for j in range(EXPERTS_PER_GROUP)]
    m2 = jnp.maximum(jnp.maximum(rest[0], rest[1]), jnp.maximum(rest[2], rest[3]))
    i2 = jnp.where((rest[0] == m2) & (i1 != 0), 0,
                   jnp.where((rest[1] == m2) & (i1 != 1), 1, jnp.where((rest[2] == m2) & (i1 != 2), 2, 3)))
    e21 = jnp.exp(m2 - m1)
    den = 1.0 / (1.0 + e21)
    w1 = p_grp * den
    w2 = p_grp * (e21 * den)
    first_lo = i1 < i2
    lo = jnp.where(first_lo, i1, i2)
    hi = jnp.where(first_lo, i2, i1)
    w_lo = jnp.where(first_lo, w1, w2)
    w_hi = jnp.where(first_lo, w2, w1)
    pair = jnp.where(lo == 0, hi - 1, jnp.where(lo == 1, hi + 1, 5))
    bucket = gsel * N_PAIRS + pair
    t = bucket.shape[1]
    rti_ref[...] = jnp.concatenate([bucket.astype(I32), jnp.zeros((7, t), I32)], axis=0)
    wrows = jnp.concatenate([w_lo, w_hi, jnp.zeros((LANES - 2, t), F32)], axis=0)
    hx_ref[0, :, half:half + LANES] = lax.bitcast_convert_type(wrows.T, U32)


def _post(x, hf, hb, gyg, ma, sgb, mod, wpb, wo, gffn, wrt, brt):
    bsz, s, d = x.shape
    t = brt.shape[1]
    assert s % t == 0
    nt = s // t
    n = bsz * s

    def full(a):
        return pl.BlockSpec(a.shape, lambda b, i: (0,) * a.ndim)

    def const(a):
        return pl.BlockSpec(a.shape, lambda b, i: (0,) * a.ndim, pipeline_mode=pl.Buffered(1))

    tok = lambda: pl.BlockSpec((1, t, d), lambda b, i: (b, i, 0))
    rt = lambda: pl.BlockSpec((8, t), lambda b, i: (0, b * nt + i))
    return pl.pallas_call(
        _post_kernel,
        grid=(bsz, nt),
        in_specs=[tok(), tok(), tok(), tok(), tok(), tok(), full(mod), const(wpb), const(wo), full(gffn), full(wrt),
                  full(brt)],
        out_specs=[tok(), pl.BlockSpec((1, t, d // 2 + LANES), lambda b, i: (b, i, 0)), rt()],
        out_shape=[jax.ShapeDtypeStruct((bsz, s, d), F32),
                   jax.ShapeDtypeStruct((bsz, s, d // 2 + LANES), U32),
                   jax.ShapeDtypeStruct((8, n), I32)],
        compiler_params=_cparams(("arbitrary", "arbitrary")),
        name="post",
    )(x, hf, hb, gyg, ma, sgb, mod, wpb, wo, gffn, wrt, brt)


def _sort_kernel(bk_ref, slot_ref, meta_ref, *, tm):
    bk = bk_ref[...]
    r = bk.shape[0]
    nt_lanes = meta_ref.shape[1]
    shift = tm.bit_length() - 1
    upper = (lax.broadcasted_iota(I32, (LANES, LANES), 0) < lax.broadcasted_iota(I32, (LANES, LANES), 1))
    upper = upper.astype(F32).astype(BF16)
    lower = (lax.broadcasted_iota(I32, (r, r), 1) < lax.broadcasted_iota(I32, (r, r), 0))
    lower = lower.astype(F32).astype(BF16)
    ones = jnp.ones((LANES, LANES), BF16)
    tile_pos = lax.broadcasted_iota(I32, (1, nt_lanes), 1) * tm
    start = jnp.zeros((1, LANES), I32)
    start_t = jnp.zeros((1, nt_lanes), I32)
    slot = jnp.zeros((r, LANES), I32)
    tile_bucket = jnp.zeros((1, nt_lanes), I32)
    for k in range(N_BUCKETS):
        m = bk == k
        mb = m.astype(F32).astype(BF16)
        pre = _bdot(mb, upper)
        tot = _bdot(mb, ones)
        rowpre = _bdot(lower, tot.astype(BF16))
        rank = (pre + rowpre).astype(I32)
        cnt = (rowpre[r - 1:r, :] + tot[r - 1:r, :]).astype(I32)
        slot = jnp.where(m, start + rank, slot)
        padded = ((cnt + (tm - 1)) >> shift) << shift
        start = start + padded
        start_t = start_t + jnp.concatenate([padded] * (nt_lanes // LANES), axis=1)
        tile_bucket = tile_bucket + (tile_pos >= start_t).astype(I32)
    slot_ref[...] = slot
    valid = (tile_pos < start_t).astype(I32)
    tb = jnp.minimum(tile_bucket, N_BUCKETS - 1)
    g = (tb * 43) >> 8
    pair = tb - N_PAIRS * g
    ge3 = (pair >= 3).astype(I32)
    ge5 = (pair >= 5).astype(I32)
    lo = ge3 + ge5
    hi = pair + 1 - 2 * ge3 - ge5
    e_lo = g * EXPERTS_PER_GROUP + lo
    e_hi = g * EXPERTS_PER_GROUP + hi
    meta_ref[...] = jnp.concatenate([e_lo, e_hi, valid, jnp.zeros((5, nt_lanes), I32)], axis=0)


def _sort(bucket2d, tm, nt_lanes):
    r = bucket2d.shape[0]
    return pl.pallas_call(
        functools.partial(_sort_kernel, tm=tm),
        grid=(1,),
        in_specs=[pl.BlockSpec((r, LANES), lambda i: (0, 0))],
        out_specs=[pl.BlockSpec((r, LANES), lambda i: (0, 0)),
                   pl.BlockSpec((8, nt_lanes), lambda i: (0, 0))],
        out_shape=[jax.ShapeDtypeStruct((r, LANES), I32),
                   jax.ShapeDtypeStruct((8, nt_lanes), I32)],
        compiler_params=_cparams(("arbitrary",)),
        name="bucket_sort",
    )(bucket2d)


def _moe_kernel(elo_ref, ehi_ref, valid_ref, h_ref, wg_lo, wg_hi, wu_lo, wu_hi, wd_lo, wd_hi, o_ref):
    i = pl.program_id(0)
    half = o_ref.shape[1]

    @pl.when(valid_ref[i] != 0)
    def _():
        h_a, h_b = _unpack_bf16_pair(h_ref[:, 0:half])
        h = jnp.concatenate([h_a, h_b], axis=1).astype(BF16)
        w_lo = lax.bitcast_convert_type(h_ref[:, half:half + 1], F32)
        w_hi = lax.bitcast_convert_type(h_ref[:, half + 1:half + 2], F32)
        act_lo = _silu(_bdot(h, wg_lo[0])) * _bdot(h, wu_lo[0]) * w_lo
        act_hi = _silu(_bdot(h, wg_hi[0])) * _bdot(h, wu_hi[0]) * w_hi
        o = _bdot(act_lo.astype(BF16), wd_lo[0]) + _bdot(act_hi.astype(BF16), wd_hi[0])
        o_ref[...] = _pack_bf16_pair(o[:, 0:half], o[:, half:2 * half])

    @pl.when(valid_ref[i] == 0)
    def _():
        o_ref[...] = jnp.zeros(o_ref.shape, U32)


def _moe(e_lo, e_hi, valid, hs, wgate, wup, wdown):
    npad, wrow = hs.shape
    tm = T_MOE
    n_tiles = npad // tm
    ne, d, f = wgate.shape
    lo3 = lambda i, elo, ehi, va: (elo[i], 0, 0)
    hi3 = lambda i, elo, ehi, va: (ehi[i], 0, 0)
    row = lambda i, elo, ehi, va: (i, 0)
    grid_spec = pltpu.PrefetchScalarGridSpec(
        num_scalar_prefetch=3,
        grid=(n_tiles,),
        in_specs=[pl.BlockSpec((tm, wrow), row),
                  pl.BlockSpec((1, d, f), lo3), pl.BlockSpec((1, d, f), hi3),
                  pl.BlockSpec((1, d, f), lo3), pl.BlockSpec((1, d, f), hi3),
                  pl.BlockSpec((1, f, d), lo3), pl.BlockSpec((1, f, d), hi3)],
        out_specs=pl.BlockSpec((tm, d // 2), row),
    )
    return pl.pallas_call(
        _moe_kernel,
        grid_spec=grid_spec,
        out_shape=jax.ShapeDtypeStruct((npad, d // 2), U32),
        compiler_params=_cparams(("arbitrary",)),
        name="moe",
    )(e_lo, e_hi, valid, hs, wgate, wgate, wup, wup, wdown, wdown)


SC_CORES = 2
SC_SUBCORES = 16
SC_LANES = 16
SC_WORKERS = SC_CORES * SC_SUBCORES
SC_ROWS = 64


def _sc_worker_id():
    return lax.axis_index("s") * SC_CORES + lax.axis_index("c")


def _sc_gather_rows(src_hbm, idx_v, out_hbm, out_base, n_chunks, bufs, sems):
    def gather(c, b):
        off = pl.multiple_of(c * SC_ROWS, SC_ROWS)
        return pltpu.make_async_copy(src_hbm.at[idx_v.at[pl.ds(off, SC_ROWS)]], bufs[b], sems[b])

    def finish(c, b):
        gather(c, b).wait()
        pltpu.sync_copy(bufs[b], out_hbm.at[pl.ds(out_base + pl.multiple_of(c * SC_ROWS, SC_ROWS), SC_ROWS)])

    gather(0, 0).start()

    @pl.loop(0, n_chunks // 2)
    def _(p):
        c = 2 * p
        gather(c + 1, 1).start()
        finish(c, 0)

        @pl.when(c + 2 < n_chunks)
        def _():
            gather(c + 2, 0).start()

        finish(c + 1, 1)

    if n_chunks % 2:
        finish(n_chunks - 1, 0)


def _sc_dispatch(hx, slot, npad):
    n, wrow = hx.shape
    assert npad % (SC_WORKERS * SC_ROWS) == 0 and n % SC_LANES == 0
    rows = npad // SC_WORKERS
    mesh = plsc.VectorSubcoreMesh(core_axis_name="c", subcore_axis_name="s")

    @functools.partial(
        pl.kernel, mesh=mesh,
        out_type=jax.ShapeDtypeStruct((npad, wrow), hx.dtype),
        scratch_types=[pltpu.VMEM((n,), I32),
                       pltpu.VMEM((rows,), I32),
                       pltpu.VMEM((SC_ROWS, wrow), hx.dtype),
                       pltpu.VMEM((SC_ROWS, wrow), hx.dtype),
                       pltpu.SemaphoreType.DMA,
                       pltpu.SemaphoreType.DMA],
        compiler_params=pltpu.CompilerParams(needs_layout_passes=False),
        name="sc_dispatch",
    )
    def k(hx_hbm, slot_hbm, out_hbm, slot_v, tok_v, rows_a, rows_b, sem_a, sem_b):
        base = _sc_worker_id() * rows
        pltpu.sync_copy(slot_hbm, slot_v)

        lane = lax.iota(I32, SC_LANES)

        @pl.loop(0, rows // SC_LANES)
        def _(i):
            filler = base + i * SC_LANES + lane
            tok_v[pl.ds(i * SC_LANES, SC_LANES)] = lax.rem(filler, jnp.int32(n))

        @plsc.parallel_loop(0, n // SC_LANES, unroll=8)
        def _(i):
            loc = slot_v[pl.ds(i * SC_LANES, SC_LANES)] - base
            mine = (loc >= 0) & (loc < rows)
            plsc.store_scatter(tok_v, [loc], i * SC_LANES + lane, mask=mine)

        _sc_gather_rows(hx_hbm, tok_v, out_hbm, base, rows // SC_ROWS, (rows_a, rows_b), (sem_a, sem_b))

    return k(hx, slot)


def _sc_combine(mo_sorted, slot):
    n = slot.shape[0]
    wrow = mo_sorted.shape[1]
    assert n % (SC_WORKERS * SC_ROWS) == 0
    toks = n // SC_WORKERS
    mesh = plsc.VectorSubcoreMesh(core_axis_name="c", subcore_axis_name="s")

    @functools.partial(
        pl.kernel, mesh=mesh,
        out_type=jax.ShapeDtypeStruct((n, wrow), mo_sorted.dtype),
        scratch_types=[pltpu.VMEM((toks,), I32),
                       pltpu.VMEM((SC_ROWS, wrow), mo_sorted.dtype),
                       pltpu.VMEM((SC_ROWS, wrow), mo_sorted.dtype),
                       pltpu.SemaphoreType.DMA,
                       pltpu.SemaphoreType.DMA],
        compiler_params=pltpu.CompilerParams(needs_layout_passes=False),
        name="sc_combine",
    )
    def k(mo_hbm, slot_hbm, out_hbm, idx_v, rows_a, rows_b, sem_a, sem_b):
        base = _sc_worker_id() * toks
        pltpu.sync_copy(slot_hbm.at[pl.ds(base, toks)], idx_v)
        _sc_gather_rows(mo_hbm, idx_v, out_hbm, base, toks // SC_ROWS, (rows_a, rows_b), (sem_a, sem_b))

    return k(mo_sorted, slot)


def _final_kernel(x1_ref, mo_ref, mod_ref, g_ref, o_ref):
    d = x1_ref.shape[2]
    b = pl.program_id(0)
    gt2 = mod_ref[pl.ds(b, 1), 5 * d:6 * d]
    mo = jnp.concatenate(_unpack_bf16_pair(mo_ref[0]), axis=1)
    o_ref[0] = _rms_norm(x1_ref[0] + gt2 * mo, g_ref[...])


def _final(x1, mo, mod, g):
    bsz, s, d = x1.shape
    t = min(T_FINAL, s)
    assert s % t == 0
    tok = lambda: pl.BlockSpec((1, t, d), lambda b, i: (b, i, 0))
    return pl.pallas_call(
        _final_kernel,
        grid=(bsz, s // t),
        in_specs=[tok(), pl.BlockSpec((1, t, d // 2), lambda b, i: (b, i, 0)),
                  pl.BlockSpec(mod.shape, lambda b, i: (0, 0)), pl.BlockSpec((1, d), lambda b, i: (0, 0))],
        out_specs=tok(),
        out_shape=jax.ShapeDtypeStruct((bsz, s, d), F32),
        compiler_params=_cparams(("arbitrary", "arbitrary")),
        name="final",
    )(x1, mo, mod, g)


def _pair_blockdiag(w):
    h, hd, _ = w.shape
    w2 = w.reshape(h // 2, 2, hd, hd)
    z = jnp.zeros((h // 2, hd, hd), w.dtype)
    top = jnp.concatenate([w2[:, 0], z], axis=2)
    bot = jnp.concatenate([z, w2[:, 1]], axis=2)
    return jnp.concatenate([top, bot], axis=1)


def _gate_params(w_r, b_r, w_i, b_i):
    wg = (0.5 * jnp.concatenate([_pair_blockdiag(w_r), _pair_blockdiag(w_i)], axis=2)).astype(BF16)
    h, hd = b_r.shape
    bg = 0.5 * jnp.concatenate([b_r.reshape(h // 2, 1, 2 * hd), b_i.reshape(h // 2, 1, 2 * hd)], axis=2)
    return wg, bg


def kernel(x, c, ctx, c_ctx, w_ada, b_ada, g_mix, w_in, b_in, conv_w, conv_b, ln_g, ln_b, w_pa, lru_conv_w, lru_conv_b, w_r_f, b_r_f, w_i_f, b_i_f, lam_f, w_r_b, b_r_b, w_i_b, b_i_b, lam_b, w_pb, w_o, g_ffn, w_grp, b_grp, w_er, b_er, w_gate, w_up, w_down, g_final):
    bsz, s, d = x.shape
    depth = w_ada.shape[0]
    assert depth == 1 and bsz + 1 <= 8
    l = 0
    n = bsz * s
    wc = conv_w.shape[2]
    wl = lru_conv_w.shape[2]
    rec0 = 2 * wc + wl
    ctx_row = bsz

    c8 = jnp.concatenate([c, c_ctx[None, :], jnp.zeros((8 - bsz - 1, d), F32)], axis=0)
    mod = _ada(c8, w_ada[l], b_ada[l][None, :])

    win = w_in[l].astype(BF16)
    bin_ = b_in[l][None, :]
    gmix = g_mix[l][None, :]
    lcw = lru_conv_w[l]
    lcb = lru_conv_b[l][None, :]
    wg_f, bg_f = _gate_params(w_r_f[l], b_r_f[l], w_i_f[l], b_i_f[l])
    wg_b, bg_b = _gate_params(w_r_b[l], b_r_b[l], w_i_b[l], b_i_b[l])
    lamf = lam_f[l][None, :]
    lamb = lam_b[l][None, :]

    xr_c = _ctxproj(ctx, mod, gmix, win[:, rec0:rec0 + wl], bin_[:, rec0:rec0 + wl], ctx_row)
    zero_state = jnp.zeros((bsz, SUBLANES, LANES), F32)
    _, h0f, xc_c = _scan(xr_c, zero_state, lcw, lcb, wg_f, bg_f, lamf, reverse=False, emit_h=False)
    _, h0b = _scan(xc_c, zero_state, lcw, lcb, wg_b, bg_b, lamb, reverse=True, emit_h=False)

    ma, gyg, sgb, xr = _k1(x, mod, gmix, win, bin_, conv_w[l], conv_b[l][None, :], ln_g[l][None, :],
                           ln_b[l][None, :], w_pa[l].astype(BF16))
    hf, _, xc = _scan(xr, h0f, lcw, lcb, wg_f, bg_f, lamf, reverse=False, emit_h=True)
    hb, _ = _scan(xc, h0b, lcw, lcb, wg_b, bg_b, lamb, reverse=True, emit_h=True)

    n_rt = 32
    wrt = jnp.concatenate([w_grp[l].T, w_er[l].T, jnp.zeros((n_rt - N_GROUPS * (1 + EXPERTS_PER_GROUP), d), F32)], axis=0)
    brt = jnp.concatenate([b_grp[l], b_er[l], jnp.zeros((n_rt - N_GROUPS * (1 + EXPERTS_PER_GROUP),), F32)])
    brt = jnp.broadcast_to(brt[:, None], (n_rt, min(T_POST, s)))
    wrt_hi = wrt.astype(BF16)
    wrt = jnp.stack([wrt_hi, (wrt - wrt_hi.astype(F32)).astype(BF16)])
    x1, hx, rti = _post(x, hf, hb, gyg, ma, sgb, mod, w_pb[l].astype(BF16), w_o[l].astype(BF16),
                        g_ffn[l][None, :], wrt, brt)

    tm = T_MOE
    n_tiles = n // tm + N_BUCKETS
    npad = n_tiles * tm
    nt_lanes = ((n_tiles + LANES - 1) // LANES) * LANES
    slot2d, meta = _sort(rti[0].reshape(n // LANES, LANES), tm, nt_lanes)
    slot = slot2d.reshape(n)
    hs = _sc_dispatch(hx.reshape(n, hx.shape[2]), slot, npad)
    mo_sorted = _moe(meta[0, :n_tiles], meta[1, :n_tiles], meta[2, :n_tiles], hs,
                     w_gate[l].astype(BF16), w_up[l].astype(BF16), w_down[l].astype(BF16))
    mo = _sc_combine(mo_sorted, slot).reshape(bsz, s, d // 2)

    return _final(x1, mo, mod, g_final[None, :])
```

```python
import functools

import jax
import jax.numpy as jnp
from jax import lax
from jax.experimental import pallas as pl
from jax.experimental.pallas import tpu as pltpu
from jax.experimental.pallas import tpu_sc as plsc

F32 = jnp.float32
BF16 = jnp.bfloat16
I32 = jnp.int32
U32 = jnp.uint32
HIGHEST = lax.Precision.HIGHEST

EPS = 1e-6
GRID_W = 64
CONV_K = 31
CONV_PAD = 16
LRU_CONV_K = 4
LRU_C = 8.0
LRU_HEADS = 8
N_GROUPS = 4
EXPERTS_PER_GROUP = 4
N_PAIRS = 6
N_BUCKETS = N_GROUPS * N_PAIRS
LANES = 128
SUBLANES = 8
VMEM_LIMIT = 56 * 1024 * 1024

T_TOK = 512
T_POST = 512
T_FINAL = 1024
T_SCAN = 256
T_MOE = 256
MOE_PARTS = 2
CAST_BLOCK_BYTES = 4 * 1024 * 1024


def _sigmoid(x):
    return 0.5 * (jnp.tanh(0.5 * x) + 1.0)


def _silu(x):
    return x * _sigmoid(x)


def _gelu_tanh(x):
    return 0.5 * x * (1.0 + jnp.tanh(0.7978845608028654 * (x + 0.044715 * (x * x * x))))


def _rms_norm(x, g):
    return x * lax.rsqrt(jnp.mean(x * x, axis=-1, keepdims=True) + EPS) * g


def _bdot(a, b):
    return jnp.dot(a, b, preferred_element_type=F32)


def _pack_bf16_pair(lo, hi):
    lo_bits = lax.bitcast_convert_type(lo.astype(BF16).astype(F32), U32)
    hi_bits = lax.bitcast_convert_type(hi.astype(BF16).astype(F32), U32)
    return (hi_bits & jnp.uint32(0xFFFF0000)) | (lo_bits >> 16)


def _unpack_bf16_pair(u):
    lo = lax.bitcast_convert_type(u << 16, F32)
    hi = lax.bitcast_convert_type(u & jnp.uint32(0xFFFF0000), F32)
    return lo, hi


def _cparams(sem):
    return pltpu.CompilerParams(dimension_semantics=sem, vmem_limit_bytes=VMEM_LIMIT)


def _cast_kernel(x_ref, o_ref):
    o_ref[...] = x_ref[...].astype(o_ref.dtype)


def _to_bf16(w):
    shape = w.shape
    w2 = w.reshape(-1, shape[-1])
    rows, cols = w2.shape
    blk = rows
    while blk % 2 == 0 and blk > 16 and blk * cols * 4 > CAST_BLOCK_BYTES:
        blk //= 2
    out = pl.pallas_call(
        _cast_kernel,
        grid=(rows // blk,),
        in_specs=[pl.BlockSpec((blk, cols), lambda i: (i, 0))],
        out_specs=pl.BlockSpec((blk, cols), lambda i: (i, 0)),
        out_shape=jax.ShapeDtypeStruct((rows, cols), BF16),
        compiler_params=_cparams(("arbitrary",)),
        name="to_bf16",
    )(w2)
    return out.reshape(shape)


def _ada_kernel(c_ref, w_ref, b_ref, o_ref):
    s = _silu(c_ref[...])
    o_ref[...] = jnp.dot(s, w_ref[...], precision=HIGHEST, preferred_element_type=F32) + b_ref[...]


def _ada(c8, w, b):
    d, n = w.shape
    blk = 1024
    return pl.pallas_call(
        _ada_kernel,
        grid=(n // blk,),
        in_specs=[pl.BlockSpec((8, d), lambda j: (0, 0)),
                  pl.BlockSpec((d, blk), lambda j: (0, j)),
                  pl.BlockSpec((1, blk), lambda j: (0, j))],
        out_specs=pl.BlockSpec((8, blk), lambda j: (0, j)),
        out_shape=jax.ShapeDtypeStruct((8, n), F32),
        compiler_params=_cparams(("arbitrary",)),
        name="ada",
    )(c8, w, b)


def _ctxproj_kernel(x_ref, mod_ref, g_ref, w_ref, b_ref, o_ref, *, row):
    d = x_ref.shape[-1]
    xn = _rms_norm(x_ref[0], g_ref[...])
    sh = mod_ref[row:row + 1, 0:d]
    sc = mod_ref[row:row + 1, d:2 * d]
    hx = (xn * (1.0 + sc) + sh).astype(BF16)
    o_ref[0] = _bdot(hx, w_ref[...]) + b_ref[...]


def _ctxproj(ctx, mod, g, w, b, row):
    bsz, n, d = ctx.shape
    return pl.pallas_call(
        functools.partial(_ctxproj_kernel, row=row),
        grid=(bsz,),
        in_specs=[pl.BlockSpec((1, n, d), lambda i: (i, 0, 0)),
                  pl.BlockSpec(mod.shape, lambda i: (0, 0)),
                  pl.BlockSpec((1, d), lambda i: (0, 0)),
                  pl.BlockSpec(w.shape, lambda i: (0, 0)),
                  pl.BlockSpec((1, w.shape[1]), lambda i: (0, 0))],
        out_specs=pl.BlockSpec((1, n, w.shape[1]), lambda i: (i, 0, 0)),
        out_shape=jax.ShapeDtypeStruct((bsz, n, w.shape[1]), F32),
        compiler_params=_cparams(("arbitrary",)),
        name="ctxproj",
    )(ctx, mod, g, w, b)


def _scan_kernel(*refs, reverse, emit_h, n_chunks):
    refs = list(refs)
    if reverse:
        xc_ref, h0_ref, wg_ref, bg_ref, lam_ref = refs[:5]
        rest = refs[5:]
        bsz, tc, w = xc_ref.shape
    else:
        xr_ref, prev_ref, next_ref, h0_ref, cw_ref, cb_ref, wg_ref, bg_ref, lam_ref = refs[:9]
        rest = refs[9:]
        bsz, tc, w = xr_ref.shape
    h_ref = rest.pop(0) if emit_h else None
    hfin_ref = rest.pop(0)
    xc_out_ref = None if reverse else rest.pop(0)
    xp_ref = None if reverse else rest.pop(0)
    a_ref, b_ref, hs_ref, hst_ref = rest
    nblk = w // LANES
    i = pl.program_id(0)
    chunk = (n_chunks - 1 - i) if reverse else i

    @pl.when(i == 0)
    def _():
        hst_ref[...] = h0_ref[...]

    z = -lam_ref[...]
    softplus = jnp.maximum(z, 0.0) + jnp.log1p(jnp.exp(-jnp.abs(z)))
    nlsh = (-0.5 * LRU_C) * softplus

    if not reverse:
        has_prev = chunk > 0
        has_next = chunk < n_chunks - 1
        for b in range(bsz):
            bp, par = divmod(b, 2)
            for cblk in range(nblk):
                cs = slice(cblk * LANES, (cblk + 1) * LANES)
                xp_ref[bp, cblk, pl.ds(par, 8, stride=2), :] = jnp.where(has_prev, prev_ref[b, :, cs], 0.0)
                xp_ref[bp, cblk, pl.ds(16 + par, tc, stride=2), :] = xr_ref[b, :, cs]
                xp_ref[bp, cblk, pl.ds(16 + 2 * tc + par, 8, stride=2), :] = (
                    jnp.where(has_next, next_ref[b, :, cs], 0.0))

    for b in range(bsz):
        bp, par = divmod(b, 2)
        xc = []
        for cblk in range(nblk):
            cs = slice(cblk * LANES, (cblk + 1) * LANES)
            if reverse:
                xc.append(xc_ref[b, :, cs])
                continue
            acc = cb_ref[:, cs] + cw_ref[0:1, cs] * xp_ref[bp, cblk, pl.ds(2 * 6 + par, tc, stride=2), :]
            for k in range(1, LRU_CONV_K):
                acc = acc + cw_ref[k:k + 1, cs] * xp_ref[bp, cblk, pl.ds(2 * (6 + k) + par, tc, stride=2), :]
            xc_out_ref[b, :, cs] = acc
            xc.append(acc)
        for p in range(LRU_HEADS // 2):
            cs = slice(2 * p * LANES, 2 * (p + 1) * LANES)
            xc_p = jnp.concatenate([xc[2 * p], xc[2 * p + 1]], axis=1)
            th = jnp.tanh(_bdot(xc_p.astype(BF16), wg_ref[p]) + bg_ref[p])
            log_a = nlsh[:, cs] * (th[:, 0:2 * LANES] + 1.0)
            a = jnp.exp(log_a)
            y = jnp.tanh(-log_a) * (1.0 + a * a)
            root = jnp.where(y > 0.0, y * lax.rsqrt(y), 0.0)
            bb = root * ((th[:, 2 * LANES:4 * LANES] + 1.0) * (0.5 * xc_p))
            for jj in range(2):
                j = 2 * p + jj
                a_ref[b, pl.ds(j, tc, stride=nblk), :] = a[:, jj * LANES:(jj + 1) * LANES]
                b_ref[b, pl.ds(j, tc, stride=nblk), :] = bb[:, jj * LANES:(jj + 1) * LANES]

    def step(s, hs):
        t = (tc - 1 - s) if reverse else s
        row = pl.multiple_of(t * nblk, nblk)
        out = []
        for b in range(bsz):
            h = a_ref[b, pl.ds(row, nblk), :] * hs[b] + b_ref[b, pl.ds(row, nblk), :]
            hs_ref[b, pl.ds(row, nblk), :] = h
            out.append(h)
        return tuple(out)

    hs = lax.fori_loop(0, tc, step, tuple(hst_ref[b] for b in range(bsz)), unroll=8)
    for b in range(bsz):
        hst_ref[b] = hs[b]
        hfin_ref[b] = hs[b]
    if emit_h:
        for b in range(bsz):
            for j in range(nblk):
                h_ref[b, :, j * LANES:(j + 1) * LANES] = hs_ref[b, pl.ds(j, tc, stride=nblk), :]


def _scan(xin, h0, cw, cb, wg, bg, lam, *, reverse, emit_h):
    bsz, s, w = xin.shape
    tc = min(T_SCAN, s)
    assert s % tc == 0 and w == SUBLANES * LANES and bsz % 2 == 0
    n_chunks = s // tc
    hb = tc // 8

    def cidx(i):
        return (n_chunks - 1 - i) if reverse else i

    tile = lambda: pl.BlockSpec((bsz, tc, w), lambda i: (0, cidx(i), 0))
    state = lambda: pl.BlockSpec((bsz, SUBLANES, LANES), lambda i: (0, 0, 0))
    gate_specs = [pl.BlockSpec(wg.shape, lambda i: (0, 0, 0)),
                  pl.BlockSpec(bg.shape, lambda i: (0, 0, 0)),
                  pl.BlockSpec(lam.shape, lambda i: (0, 0))]
    if reverse:
        args = (xin, h0, wg, bg, lam)
        in_specs = [tile(), state()] + gate_specs
    else:
        args = (xin, xin, xin, h0, cw, cb, wg, bg, lam)
        in_specs = [tile(),
                    pl.BlockSpec((bsz, 8, w), lambda i: (0, jnp.maximum(cidx(i) * hb - 1, 0), 0)),
                    pl.BlockSpec((bsz, 8, w), lambda i: (0, jnp.minimum((cidx(i) + 1) * hb, n_chunks * hb - 1), 0)),
                    state(),
                    pl.BlockSpec(cw.shape, lambda i: (0, 0)),
                    pl.BlockSpec(cb.shape, lambda i: (0, 0))] + gate_specs
    full_seq = jax.ShapeDtypeStruct((bsz, s, w), F32)
    out_specs = [state()]
    out_shape = [jax.ShapeDtypeStruct((bsz, SUBLANES, LANES), F32)]
    if emit_h:
        out_specs = [tile()] + out_specs
        out_shape = [full_seq] + out_shape
    scratch = [pltpu.VMEM((bsz, tc * SUBLANES, LANES), F32),
               pltpu.VMEM((bsz, tc * SUBLANES, LANES), F32),
               pltpu.VMEM((bsz, tc * SUBLANES, LANES), F32),
               pltpu.VMEM((bsz, SUBLANES, LANES), F32)]
    if not reverse:
        out_specs = out_specs + [tile()]
        out_shape = out_shape + [full_seq]
        scratch = [pltpu.VMEM((bsz // 2, w // LANES, 2 * (tc + 16), LANES), F32)] + scratch
    res = pl.pallas_call(
        functools.partial(_scan_kernel, reverse=reverse, emit_h=emit_h, n_chunks=n_chunks),
        grid=(n_chunks,),
        in_specs=in_specs,
        out_specs=out_specs,
        out_shape=out_shape,
        scratch_shapes=scratch,
        compiler_params=_cparams(("arbitrary",)),
        name="scan_bwd" if reverse else "scan_fwd",
    )(*args)
    return tuple(res) if emit_h else (None,) + tuple(res)


CONV_ROWS = 4


def _k1_kernel(x_ref, mod_ref, gmix_ref, win_ref, bin_ref, cw_ref, cb_ref, lng_ref, lnb_ref, wpa_ref,
               ma_ref, gyg_ref, sgb_ref, xr_ref, zp_ref, wb_ref, de_ref, co_ref, sga_ref):
    t, d = x_ref.shape[1], x_ref.shape[2]
    wc = cw_ref.shape[1]
    n_cblk = wc // LANES
    n_quads = t // (CONV_ROWS * GRID_W)
    lo = CONV_PAD
    hi = CONV_PAD + GRID_W
    first = (pl.program_id(0) == 0) & (pl.program_id(1) == 0)

    @pl.when(first)
    def _():
        for k in range(CONV_K):
            for cblk in range(n_cblk):
                row = cw_ref[k:k + 1, cblk * LANES:(cblk + 1) * LANES]
                wb_ref[k, cblk] = jnp.broadcast_to(row, (2 * GRID_W, LANES)).astype(BF16)

    xn = _rms_norm(x_ref[0], gmix_ref[...])
    sh1 = mod_ref[0, :, 0:d]
    sc1 = mod_ref[0, :, d:2 * d]
    hx = (xn * (1.0 + sc1) + sh1).astype(BF16)

    def proj(c0, c1):
        return _bdot(hx, win_ref[:, c0:c1]) + bin_ref[:, c0:c1]

    zglu = proj(0, wc) * _sigmoid(proj(wc, 2 * wc))

    zp_ref[:, :, 0:2 * lo, :] = jnp.zeros((n_quads, n_cblk, 2 * lo, LANES), U32)
    zp_ref[:, :, 2 * hi:2 * (hi + CONV_PAD), :] = jnp.zeros((n_quads, n_cblk, 2 * CONV_PAD, LANES), U32)
    for q in range(n_quads):
        for par in range(2):
            r0 = (CONV_ROWS * q + 2 * par) * GRID_W
            for cblk in range(n_cblk):
                cs = slice(cblk * LANES, (cblk + 1) * LANES)
                zp_ref[q, cblk, pl.ds(2 * lo + par, GRID_W, stride=2), :] = _pack_bf16_pair(
                    zglu[r0:r0 + GRID_W, cs], zglu[r0 + GRID_W:r0 + 2 * GRID_W, cs])

    def conv_block(cblk):
        bias = cb_ref[cblk]
        for q in range(n_quads):
            for par in range(2):
                r0 = (CONV_ROWS * q + 2 * par) * GRID_W
                acc = jnp.zeros((2 * GRID_W, LANES), F32)
                for k in range(CONV_K):
                    off = lo - CONV_K // 2 + k
                    words = zp_ref[q, cblk, pl.ds(2 * off + par, GRID_W, stride=2), :]
                    acc = acc + pltpu.bitcast(words, BF16).astype(F32) * wb_ref[k, cblk].astype(F32)
                slot = 2 * q + par
                de_ref[slot] = acc
                co_ref[cblk, r0:r0 + GRID_W, :] = de_ref[slot, pl.ds(0, GRID_W, stride=2), :] + bias
                co_ref[cblk, r0 + GRID_W:r0 + 2 * GRID_W, :] = de_ref[slot, pl.ds(1, GRID_W, stride=2), :] + bias

    o = 2 * wc
    gyg_ref[0] = _gelu_tanh(proj(o, o + d)).astype(BF16)
    xr_ref[0] = proj(o + d, o + 2 * d)
    sga_ref[...] = _sigmoid(proj(o + 2 * d, o + 3 * d))
    sgb_ref[0] = _sigmoid(proj(o + 3 * d, o + 4 * d)).astype(BF16)

    def conv_body(cblk, carry):
        conv_block(cblk)
        return carry

    lax.fori_loop(0, n_cblk, conv_body, 0)
    c = jnp.concatenate([co_ref[cblk] for cblk in range(n_cblk)], axis=1)
    mu = jnp.mean(c, axis=-1, keepdims=True)
    cc = c - mu
    y = cc * lax.rsqrt(jnp.mean(cc * cc, axis=-1, keepdims=True) + EPS) * lng_ref[...] + lnb_ref[...]
    a_lat = _bdot(_silu(y).astype(BF16), wpa_ref[...])
    ma_ref[0] = (sga_ref[...] * a_lat).astype(BF16)


def _k1(x, mod, gmix, win, bin_, cw, cb, lng, lnb, wpa):
    bsz, s, d = x.shape
    t = T_TOK
    assert s % t == 0 and t % (CONV_ROWS * GRID_W) == 0
    wc = cw.shape[1]
    n_cblk = wc // LANES
    n_quads = t // (CONV_ROWS * GRID_W)
    cb = cb.reshape(n_cblk, 1, LANES)

    def full(a):
        return pl.BlockSpec(a.shape, lambda b, i: (0,) * a.ndim)

    def const(a):
        return pl.BlockSpec(a.shape, lambda b, i: (0,) * a.ndim, pipeline_mode=pl.Buffered(1))

    tok = lambda: pl.BlockSpec((1, t, d), lambda b, i: (b, i, 0))
    return pl.pallas_call(
        _k1_kernel,
        grid=(bsz, s // t),
        in_specs=[tok(), pl.BlockSpec((1, 1, mod.shape[1]), lambda b, i: (b, 0, 0)), full(gmix), const(win),
                  full(bin_), full(cw), full(cb), full(lng), full(lnb), const(wpa)],
        out_specs=[tok(), tok(), tok(), tok()],
        out_shape=[jax.ShapeDtypeStruct((bsz, s, d), BF16),
                   jax.ShapeDtypeStruct((bsz, s, d), BF16),
                   jax.ShapeDtypeStruct((bsz, s, d), BF16),
                   jax.ShapeDtypeStruct((bsz, s, d), F32)],
        scratch_shapes=[pltpu.VMEM((n_quads, n_cblk, 2 * (GRID_W + 2 * CONV_PAD), LANES), U32),
                        pltpu.VMEM((CONV_K, n_cblk, 2 * GRID_W, LANES), BF16),
                        pltpu.VMEM((2 * n_quads, 2 * GRID_W, LANES), F32),
                        pltpu.VMEM((n_cblk, t, LANES), F32),
                        pltpu.VMEM((t, d), F32)],
        compiler_params=_cparams(("arbitrary", "arbitrary")),
        name="k1",
    )(x, mod[:, None, :], gmix, win, bin_, cw, cb, lng, lnb, wpa)


def _post_kernel(x_ref, hf_ref, hb_ref, gyg_ref, ma_ref, sgb_ref, mod_ref, wpb_ref, wo_ref, gffn_ref,
                 wrt_ref, brt_ref, x1_ref, hx_ref, rti_ref):
    d = x_ref.shape[2]
    b = pl.program_id(0)
    y_rec = hf_ref[0] + hb_ref[0]
    b_lat = _bdot((gyg_ref[0].astype(F32) * y_rec).astype(BF16), wpb_ref[...])
    merged = ma_ref[0].astype(F32) + sgb_ref[0].astype(F32) * b_lat
    res = _bdot(merged.astype(BF16), wo_ref[...])
    gt1 = mod_ref[pl.ds(b, 1), 2 * d:3 * d]
    x1 = x_ref[0] + gt1 * res
    x1_ref[0] = x1
    sh2 = mod_ref[pl.ds(b, 1), 3 * d:4 * d]
    sc2 = mod_ref[pl.ds(b, 1), 4 * d:5 * d]
    hm = _rms_norm(x1, gffn_ref[...]) * (1.0 + sc2) + sh2
    half = d // 2
    hx_ref[0, :, 0:half] = _pack_bf16_pair(hm[:, 0:half], hm[:, half:d])

    nt = (((1,), (1,)), ((), ()))
    hm_hi = hm.astype(BF16)
    hm_lo = (hm - hm_hi.astype(F32)).astype(BF16)
    lg = (lax.dot_general(wrt_ref[0], hm_hi, nt, preferred_element_type=F32)
          + lax.dot_general(wrt_ref[0], hm_lo, nt, preferred_element_type=F32)
          + lax.dot_general(wrt_ref[1], hm_hi, nt, preferred_element_type=F32)) + brt_ref[...]
    gl = [lg[k:k + 1, :] for k in range(N_GROUPS)]
    gmax = jnp.maximum(jnp.maximum(gl[0], gl[1]), jnp.maximum(gl[2], gl[3]))
    gsel = jnp.where(gl[0] == gmax, 0, jnp.where(gl[1] == gmax, 1, jnp.where(gl[2] == gmax, 2, 3)))
    p_grp = 1.0 / (jnp.exp(gl[0] - gmax) + jnp.exp(gl[1] - gmax) + jnp.exp(gl[2] - gmax) + jnp.exp(gl[3] - gmax))
    el = []
    for j in range(EXPERTS_PER_GROUP):
        rows = [lg[N_GROUPS + g * EXPERTS_PER_GROUP + j:N_GROUPS + g * EXPERTS_PER_GROUP + j + 1, :]
                for g in range(N_GROUPS)]
        el.append(jnp.where(gsel == 0, rows[0], jnp.where(gsel == 1, rows[1], jnp.where(gsel == 2, rows[2], rows[3]))))
    m1 = jnp.maximum(jnp.maximum(el[0], el[1]), jnp.maximum(el[2], el[3]))
    i1 = jnp.where(el[0] == m1, 0, jnp.where(el[1] == m1, 1, jnp.where(el[2] == m1, 2, 3)))
    neg = jnp.float32(-jnp.inf)
    rest = [jnp.where(i1 == j, neg, el[j]) for j in range(EXPERTS_PER_GROUP)]
    m2 = jnp.maximum(jnp.maximum(rest[0], rest[1]), jnp.maximum(rest[2], rest[3]))
    i2 = jnp.where((rest[0] == m2) & (i1 != 0), 0,
                   jnp.where((rest[1] == m2) & (i1 != 1), 1, jnp.where((rest[2] == m2) & (i1 != 2), 2, 3)))
    e21 = jnp.exp(m2 - m1)
    den = 1.0 / (1.0 + e21)
    w1 = p_grp * den
    w2 = p_grp * (e21 * den)
    first_lo = i1 < i2
    lo = jnp.where(first_lo, i1, i2)
    hi = jnp.where(first_lo, i2, i1)
    w_lo = jnp.where(first_lo, w1, w2)
    w_hi = jnp.where(first_lo, w2, w1)
    pair = jnp.where(lo == 0, hi - 1, jnp.where(lo == 1, hi + 1, 5))
    bucket = gsel * N_PAIRS + pair
    t = bucket.shape[1]
    rti_ref[...] = jnp.concatenate([bucket.astype(I32), jnp.zeros((7, t), I32)], axis=0)
    wrows = jnp.concatenate([w_lo, w_hi, jnp.zeros((LANES - 2, t), F32)], axis=0)
    hx_ref[0, :, half:half + LANES] = lax.bitcast_convert_type(wrows.T, U32)


def _post(x, hf, hb, gyg, ma, sgb, mod, wpb, wo, gffn, wrt, brt):
    bsz, s, d = x.shape
    t = brt.shape[1]
    assert s % t == 0
    nt = s // t
    n = bsz * s

    def full(a):
        return pl.BlockSpec(a.shape, lambda b, i: (0,) * a.ndim)

    def const(a):
        return pl.BlockSpec(a.shape, lambda b, i: (0,) * a.ndim, pipeline_mode=pl.Buffered(1))

    tok = lambda: pl.BlockSpec((1, t, d), lambda b, i: (b, i, 0))
    rt = lambda: pl.BlockSpec((8, t), lambda b, i: (0, b * nt + i))
    return pl.pallas_call(
        _post_kernel,
        grid=(bsz, nt),
        in_specs=[tok(), tok(), tok(), tok(), tok(), tok(), full(mod), const(wpb), const(wo), full(gffn), full(wrt),
                  full(brt)],
        out_specs=[tok(), pl.BlockSpec((1, t, d // 2 + LANES), lambda b, i: (b, i, 0)), rt()],
        out_shape=[jax.ShapeDtypeStruct((bsz, s, d), F32),
                   jax.ShapeDtypeStruct((bsz, s, d // 2 + LANES), U32),
                   jax.ShapeDtypeStruct((8, n), I32)],
        compiler_params=_cparams(("arbitrary", "arbitrary")),
        name="post",
    )(x, hf, hb, gyg, ma, sgb, mod, wpb, wo, gffn, wrt, brt)


def _sort_kernel(bk_ref, slot_ref, meta_ref, *, tm):
    bk = bk_ref[...]
    r = bk.shape[0]
    nt_lanes = meta_ref.shape[1]
    shift = tm.bit_length() - 1
    upper = (lax.broadcasted_iota(I32, (LANES, LANES), 0) < lax.broadcasted_iota(I32, (LANES, LANES), 1))
    upper = upper.astype(F32).astype(BF16)
    lower = (lax.broadcasted_iota(I32, (r, r), 1) < lax.broadcasted_iota(I32, (r, r), 0))
    lower = lower.astype(F32).astype(BF16)
    ones = jnp.ones((LANES, LANES), BF16)
    tile_pos = lax.broadcasted_iota(I32, (1, nt_lanes), 1) * tm
    start = jnp.zeros((1, LANES), I32)
    start_t = jnp.zeros((1, nt_lanes), I32)
    slot = jnp.zeros((r, LANES), I32)
    tile_bucket = jnp.zeros((1, nt_lanes), I32)
    for k in range(N_BUCKETS):
        m = bk == k
        mb = m.astype(F32).astype(BF16)
        pre = _bdot(mb, upper)
        tot = _bdot(mb, ones)
        rowpre = _bdot(lower, tot.astype(BF16))
        rank = (pre + rowpre).astype(I32)
        cnt = (rowpre[r - 1:r, :] + tot[r - 1:r, :]).astype(I32)
        slot = jnp.where(m, start + rank, slot)
        padded = ((cnt + (tm - 1)) >> shift) << shift
        start = start + padded
        start_t = start_t + jnp.concatenate([padded] * (nt_lanes // LANES), axis=1)
        tile_bucket = tile_bucket + (tile_pos >= start_t).astype(I32)
    slot_ref[...] = slot
    valid = (tile_pos < start_t).astype(I32)
    tb = jnp.minimum(tile_bucket, N_BUCKETS - 1)
    g = (tb * 43) >> 8
    pair = tb - N_PAIRS * g
    ge3 = (pair >= 3).astype(I32)
    ge5 = (pair >= 5).astype(I32)
    lo = ge3 + ge5
    hi = pair + 1 - 2 * ge3 - ge5
    e_lo = g * EXPERTS_PER_GROUP + lo
    e_hi = g * EXPERTS_PER_GROUP + hi
    meta_ref[...] = jnp.concatenate([e_lo, e_hi, valid, jnp.zeros((5, nt_lanes), I32)], axis=0)


def _sort(bucket2d, tm, nt_lanes):
    r = bucket2d.shape[0]
    return pl.pallas_call(
        functools.partial(_sort_kernel, tm=tm),
        grid=(1,),
        in_specs=[pl.BlockSpec((r, LANES), lambda i: (0, 0))],
        out_specs=[pl.BlockSpec((r, LANES), lambda i: (0, 0)),
                   pl.BlockSpec((8, nt_lanes), lambda i: (0, 0))],
        out_shape=[jax.ShapeDtypeStruct((r, LANES), I32),
                   jax.ShapeDtypeStruct((8, nt_lanes), I32)],
        compiler_params=_cparams(("arbitrary",)),
        name="bucket_sort",
    )(bucket2d)


def _moe_kernel(elo_ref, ehi_ref, valid_ref, h_ref, wg_lo, wg_hi, wu_lo, wu_hi, wd_lo, wd_hi, *rest):
    o_ref = rest[-1]
    i = pl.program_id(0)
    half = o_ref.shape[1]

    @pl.when(valid_ref[i] != 0)
    def _():
        h_a, h_b = _unpack_bf16_pair(h_ref[:, 0:half])
        h = jnp.concatenate([h_a, h_b], axis=1).astype(BF16)
        w_lo = lax.bitcast_convert_type(h_ref[:, half:half + 1], F32)
        w_hi = lax.bitcast_convert_type(h_ref[:, half + 1:half + 2], F32)
        act_lo = _silu(_bdot(h, wg_lo[0])) * _bdot(h, wu_lo[0]) * w_lo
        act_hi = _silu(_bdot(h, wg_hi[0])) * _bdot(h, wu_hi[0]) * w_hi
        o = _bdot(act_lo.astype(BF16), wd_lo[0]) + _bdot(act_hi.astype(BF16), wd_hi[0])
        o_ref[...] = _pack_bf16_pair(o[:, 0:half], o[:, half:2 * half])

    @pl.when(valid_ref[i] == 0)
    def _():
        o_ref[...] = jnp.zeros(o_ref.shape, U32)


def _moe(e_lo, e_hi, valid, hs, wgate, wup, wdown, *, tile0, npad, prev):
    rows, wrow = hs.shape
    tm = T_MOE
    n_tiles = rows // tm
    ne, d, f = wgate.shape
    lo3 = lambda i, elo, ehi, va: (elo[i], 0, 0)
    hi3 = lambda i, elo, ehi, va: (ehi[i], 0, 0)
    row = lambda i, elo, ehi, va: (i, 0)
    in_specs = [pl.BlockSpec((tm, wrow), row),
                pl.BlockSpec((1, d, f), lo3), pl.BlockSpec((1, d, f), hi3),
                pl.BlockSpec((1, d, f), lo3), pl.BlockSpec((1, d, f), hi3),
                pl.BlockSpec((1, f, d), lo3), pl.BlockSpec((1, f, d), hi3)]
    args = [e_lo, e_hi, valid, hs, wgate, wgate, wup, wup, wdown, wdown]
    aliases = {}
    if prev is not None:
        in_specs.append(pl.BlockSpec(memory_space=pl.ANY))
        aliases = {len(args): 0}
        args.append(prev)
    grid_spec = pltpu.PrefetchScalarGridSpec(
        num_scalar_prefetch=3,
        grid=(n_tiles,),
        in_specs=in_specs,
        out_specs=pl.BlockSpec((tm, d // 2), lambda i, elo, ehi, va: (i + tile0, 0)),
    )
    return pl.pallas_call(
        _moe_kernel,
        grid_spec=grid_spec,
        out_shape=jax.ShapeDtypeStruct((npad, d // 2), U32),
        input_output_aliases=aliases,
        compiler_params=_cparams(("arbitrary",)),
        name="moe",
    )(*args)


SC_CORES = 2
SC_SUBCORES = 16
SC_LANES = 16
SC_WORKERS = SC_CORES * SC_SUBCORES
SC_ROWS = 64


def _sc_worker_id():
    return lax.axis_index("s") * SC_CORES + lax.axis_index("c")


def _sc_chunk_rows(rows):
    return next(c for c in (SC_ROWS, SC_ROWS // 2, SC_ROWS // 4) if rows % c == 0)


def _sc_gather_rows(src_hbm, idx_v, out_hbm, out_base, n_chunks, bufs, sems):
    chunk = bufs[0].shape[0]

    def gather(c, b):
        off = pl.multiple_of(c * chunk, chunk)
        return pltpu.make_async_copy(src_hbm.at[idx_v.at[pl.ds(off, chunk)]], bufs[b], sems[b])

    def finish(c, b):
        gather(c, b).wait()
        pltpu.sync_copy(bufs[b], out_hbm.at[pl.ds(out_base + pl.multiple_of(c * chunk, chunk), chunk)])

    gather(0, 0).start()

    @pl.loop(0, n_chunks // 2)
    def _(p):
        c = 2 * p
        gather(c + 1, 1).start()
        finish(c, 0)

        @pl.when(c + 2 < n_chunks)
        def _():
            gather(c + 2, 0).start()

        finish(c + 1, 1)

    if n_chunks % 2:
        finish(n_chunks - 1, 0)


def _sc_dispatch(hx, slot, slot0, n_slots):
    n, wrow = hx.shape
    assert n_slots % (SC_WORKERS * SC_LANES) == 0 and n % SC_LANES == 0
    rows = n_slots // SC_WORKERS
    chunk = _sc_chunk_rows(rows)
    mesh = plsc.VectorSubcoreMesh(core_axis_name="c", subcore_axis_name="s")

    @functools.partial(
        pl.kernel, mesh=mesh,
        out_type=jax.ShapeDtypeStruct((n_slots, wrow), hx.dtype),
        scratch_types=[pltpu.VMEM((n,), I32),
                       pltpu.VMEM((rows,), I32),
                       pltpu.VMEM((chunk, wrow), hx.dtype),
                       pltpu.VMEM((chunk, wrow), hx.dtype),
                       pltpu.SemaphoreType.DMA,
                       pltpu.SemaphoreType.DMA],
        compiler_params=pltpu.CompilerParams(needs_layout_passes=False),
        name="sc_dispatch",
    )
    def k(hx_hbm, slot_hbm, out_hbm, slot_v, tok_v, rows_a, rows_b, sem_a, sem_b):
        out_base = _sc_worker_id() * rows
        base = slot0 + out_base
        pltpu.sync_copy(slot_hbm, slot_v)

        lane = lax.iota(I32, SC_LANES)

        @pl.loop(0, rows // SC_LANES)
        def _(i):
            filler = base + i * SC_LANES + lane
            tok_v[pl.ds(i * SC_LANES, SC_LANES)] = lax.rem(filler, jnp.int32(n))

        @plsc.parallel_loop(0, n // SC_LANES, unroll=8)
        def _(i):
            loc = slot_v[pl.ds(i * SC_LANES, SC_LANES)] - base
            mine = (loc >= 0) & (loc < rows)
            plsc.store_scatter(tok_v, [loc], i * SC_LANES + lane, mask=mine)

        _sc_gather_rows(hx_hbm, tok_v, out_hbm, out_base, rows // chunk, (rows_a, rows_b), (sem_a, sem_b))

    return k(hx, slot)


def _sc_combine(mo_sorted, slot):
    n = slot.shape[0]
    wrow = mo_sorted.shape[1]
    assert n % (SC_WORKERS * SC_LANES) == 0
    toks = n // SC_WORKERS
    chunk = _sc_chunk_rows(toks)
    mesh = plsc.VectorSubcoreMesh(core_axis_name="c", subcore_axis_name="s")

    @functools.partial(
        pl.kernel, mesh=mesh,
        out_type=jax.ShapeDtypeStruct((n, wrow), mo_sorted.dtype),
        scratch_types=[pltpu.VMEM((toks,), I32),
                       pltpu.VMEM((chunk, wrow), mo_sorted.dtype),
                       pltpu.VMEM((chunk, wrow), mo_sorted.dtype),
                       pltpu.SemaphoreType.DMA,
                       pltpu.SemaphoreType.DMA],
        compiler_params=pltpu.CompilerParams(needs_layout_passes=False),
        name="sc_combine",
    )
    def k(mo_hbm, slot_hbm, out_hbm, idx_v, rows_a, rows_b, sem_a, sem_b):
        base = _sc_worker_id() * toks
        pltpu.sync_copy(slot_hbm.at[pl.ds(base, toks)], idx_v)
        _sc_gather_rows(mo_hbm, idx_v, out_hbm, base, toks // chunk, (rows_a, rows_b), (sem_a, sem_b))

    return k(mo_sorted, slot)


def _final_kernel(x1_ref, mo_ref, mod_ref, g_ref, o_ref):
    d = x1_ref.shape[2]
    b = pl.program_id(0)
    gt2 = mod_ref[pl.ds(b, 1), 5 * d:6 * d]
    mo = jnp.concatenate(_unpack_bf16_pair(mo_ref[0]), axis=1)
    o_ref[0] = _rms_norm(x1_ref[0] + gt2 * mo, g_ref[...])


def _final(x1, mo, mod, g):
    bsz, s, d = x1.shape
    t = min(T_FINAL, s)
    assert s % t == 0
    tok = lambda: pl.BlockSpec((1, t, d), lambda b, i: (b, i, 0))
    return pl.pallas_call(
        _final_kernel,
        grid=(bsz, s // t),
        in_specs=[tok(), pl.BlockSpec((1, t, d // 2), lambda b, i: (b, i, 0)),
                  pl.BlockSpec(mod.shape, lambda b, i: (0, 0)), pl.BlockSpec((1, d), lambda b, i: (0, 0))],
        out_specs=tok(),
        out_shape=jax.ShapeDtypeStruct((bsz, s, d), F32),
        compiler_params=_cparams(("arbitrary", "arbitrary")),
        name="final",
    )(x1, mo, mod, g)


def _pair_blockdiag(w):
    h, hd, _ = w.shape
    w2 = w.reshape(h // 2, 2, hd, hd)
    z = jnp.zeros((h // 2, hd, hd), w.dtype)
    top = jnp.concatenate([w2[:, 0], z], axis=2)
    bot = jnp.concatenate([z, w2[:, 1]], axis=2)
    return jnp.concatenate([top, bot], axis=1)


def _gate_params(w_r, b_r, w_i, b_i):
    wg = (0.5 * jnp.concatenate([_pair_blockdiag(w_r), _pair_blockdiag(w_i)], axis=2)).astype(BF16)
    h, hd = b_r.shape
    bg = 0.5 * jnp.concatenate([b_r.reshape(h // 2, 1, 2 * hd), b_i.reshape(h // 2, 1, 2 * hd)], axis=2)
    return wg, bg


def kernel(x, c, ctx, c_ctx, w_ada, b_ada, g_mix, w_in, b_in, conv_w, conv_b, ln_g, ln_b, w_pa, lru_conv_w, lru_conv_b, w_r_f, b_r_f, w_i_f, b_i_f, lam_f, w_r_b, b_r_b, w_i_b, b_i_b, lam_b, w_pb, w_o, g_ffn, w_grp, b_grp, w_er, b_er, w_gate, w_up, w_down, g_final):
    bsz, s, d = x.shape
    depth = w_ada.shape[0]
    assert depth == 1 and bsz + 1 <= 8
    l = 0
    n = bsz * s
    wc = conv_w.shape[2]
    wl = lru_conv_w.shape[2]
    rec0 = 2 * wc + wl
    ctx_row = bsz

    c8 = jnp.concatenate([c, c_ctx[None, :], jnp.zeros((8 - bsz - 1, d), F32)], axis=0)
    mod = _ada(c8, w_ada[l], b_ada[l][None, :])

    win = _to_bf16(w_in[l])
    bin_ = b_in[l][None, :]
    gmix = g_mix[l][None, :]
    lcw = lru_conv_w[l]
    lcb = lru_conv_b[l][None, :]
    wg_f, bg_f = _gate_params(w_r_f[l], b_r_f[l], w_i_f[l], b_i_f[l])
    wg_b, bg_b = _gate_params(w_r_b[l], b_r_b[l], w_i_b[l], b_i_b[l])
    lamf = lam_f[l][None, :]
    lamb = lam_b[l][None, :]

    xr_c = _ctxproj(ctx, mod, gmix, win[:, rec0:rec0 + wl], bin_[:, rec0:rec0 + wl], ctx_row)
    zero_state = jnp.zeros((bsz, SUBLANES, LANES), F32)
    _, h0f, xc_c = _scan(xr_c, zero_state, lcw, lcb, wg_f, bg_f, lamf, reverse=False, emit_h=False)
    _, h0b = _scan(xc_c, zero_state, lcw, lcb, wg_b, bg_b, lamb, reverse=True, emit_h=False)

    ma, gyg, sgb, xr = _k1(x, mod, gmix, win, bin_, conv_w[l], conv_b[l][None, :], ln_g[l][None, :],
                           ln_b[l][None, :], w_pa[l].astype(BF16))
    hf, _, xc = _scan(xr, h0f, lcw, lcb, wg_f, bg_f, lamf, reverse=False, emit_h=True)
    hb, _ = _scan(xc, h0b, lcw, lcb, wg_b, bg_b, lamb, reverse=True, emit_h=True)

    n_rt = 32
    wrt = jnp.concatenate([w_grp[l].T, w_er[l].T, jnp.zeros((n_rt - N_GROUPS * (1 + EXPERTS_PER_GROUP), d), F32)], axis=0)
    brt = jnp.concatenate([b_grp[l], b_er[l], jnp.zeros((n_rt - N_GROUPS * (1 + EXPERTS_PER_GROUP),), F32)])
    brt = jnp.broadcast_to(brt[:, None], (n_rt, min(T_POST, s)))
    wrt_hi = wrt.astype(BF16)
    wrt = jnp.stack([wrt_hi, (wrt - wrt_hi.astype(F32)).astype(BF16)])
    x1, hx, rti = _post(x, hf, hb, gyg, ma, sgb, mod, w_pb[l].astype(BF16), w_o[l].astype(BF16),
                        g_ffn[l][None, :], wrt, brt)

    tm = T_MOE
    n_tiles = n // tm + N_BUCKETS
    npad = n_tiles * tm
    nt_lanes = ((n_tiles + LANES - 1) // LANES) * LANES
    slot2d, meta = _sort(rti[0].reshape(n // LANES, LANES), tm, nt_lanes)
    slot = slot2d.reshape(n)
    assert n_tiles % MOE_PARTS == 0
    part_tiles = n_tiles // MOE_PARTS
    hx2d = hx.reshape(n, hx.shape[2])
    wgate, wup, wdown = _to_bf16(w_gate[l]), _to_bf16(w_up[l]), _to_bf16(w_down[l])
    hs_parts = [_sc_dispatch(hx2d, slot, p * part_tiles * tm, part_tiles * tm) for p in range(MOE_PARTS)]
    mo_sorted = None
    for p in range(MOE_PARTS):
        t0 = p * part_tiles
        mo_sorted = _moe(meta[0, t0:t0 + part_tiles], meta[1, t0:t0 + part_tiles], meta[2, t0:t0 + part_tiles],
                         hs_parts[p], wgate, wup, wdown, tile0=t0, npad=npad, prev=mo_sorted)
    mo = _sc_combine(mo_sorted, slot).reshape(bsz, s, d // 2)

    return _final(x1, mo, mod, g_final[None, :])
```

```python
import functools

import jax
import jax.numpy as jnp
from jax import lax
from jax.experimental import pallas as pl
from jax.experimental.pallas import tpu as pltpu
from jax.experimental.pallas import tpu_sc as plsc

F32 = jnp.float32
BF16 = jnp.bfloat16
I32 = jnp.int32
U32 = jnp.uint32
HIGHEST = lax.Precision.HIGHEST

EPS = 1e-6
GRID_W = 64
CONV_K = 31
CONV_PAD = 16
LRU_CONV_K = 4
LRU_C = 8.0
LRU_HEADS = 8
N_GROUPS = 4
EXPERTS_PER_GROUP = 4
N_PAIRS = 6
N_BUCKETS = N_GROUPS * N_PAIRS
LANES = 128
SUBLANES = 8
VMEM_LIMIT = 56 * 1024 * 1024

T_TOK = 512
T_POST = 512
T_FINAL = 1024
T_SCAN = 256
T_MOE = 256
MOE_PARTS = 1
FINAL_PARTS = 2
CAST_BLOCK_BYTES = 4 * 1024 * 1024


def _sigmoid(x):
    return 0.5 * (jnp.tanh(0.5 * x) + 1.0)


def _silu(x):
    return x * _sigmoid(x)


def _gelu_tanh(x):
    return 0.5 * x * (1.0 + jnp.tanh(0.7978845608028654 * (x + 0.044715 * (x * x * x))))


def _rms_norm(x, g):
    return x * lax.rsqrt(jnp.mean(x * x, axis=-1, keepdims=True) + EPS) * g


def _bdot(a, b):
    return jnp.dot(a, b, preferred_element_type=F32)


def _pack_bf16_pair(lo, hi):
    lo_bits = lax.bitcast_convert_type(lo.astype(BF16).astype(F32), U32)
    hi_bits = lax.bitcast_convert_type(hi.astype(BF16).astype(F32), U32)
    return (hi_bits & jnp.uint32(0xFFFF0000)) | (lo_bits >> 16)


def _unpack_bf16_pair(u):
    lo = lax.bitcast_convert_type(u << 16, F32)
    hi = lax.bitcast_convert_type(u & jnp.uint32(0xFFFF0000), F32)
    return lo, hi


def _cparams(sem):
    return pltpu.CompilerParams(dimension_semantics=sem, vmem_limit_bytes=VMEM_LIMIT)


def _cast_kernel(x_ref, o_ref):
    o_ref[...] = x_ref[...].astype(o_ref.dtype)


def _to_bf16(w):
    shape = w.shape
    w2 = w.reshape(-1, shape[-1])
    rows, cols = w2.shape
    blk = rows
    while blk % 2 == 0 and blk > 16 and blk * cols * 4 > CAST_BLOCK_BYTES:
        blk //= 2
    out = pl.pallas_call(
        _cast_kernel,
        grid=(rows // blk,),
        in_specs=[pl.BlockSpec((blk, cols), lambda i: (i, 0))],
        out_specs=pl.BlockSpec((blk, cols), lambda i: (i, 0)),
        out_shape=jax.ShapeDtypeStruct((rows, cols), BF16),
        compiler_params=_cparams(("arbitrary",)),
        name="to_bf16",
    )(w2)
    return out.reshape(shape)


def _ada_kernel(c_ref, w_ref, b_ref, o_ref):
    s = _silu(c_ref[...])
    o_ref[...] = jnp.dot(s, w_ref[...], precision=HIGHEST, preferred_element_type=F32) + b_ref[...]


def _ada(c8, w, b):
    d, n = w.shape
    blk = 1024
    return pl.pallas_call(
        _ada_kernel,
        grid=(n // blk,),
        in_specs=[pl.BlockSpec((8, d), lambda j: (0, 0)),
                  pl.BlockSpec((d, blk), lambda j: (0, j)),
                  pl.BlockSpec((1, blk), lambda j: (0, j))],
        out_specs=pl.BlockSpec((8, blk), lambda j: (0, j)),
        out_shape=jax.ShapeDtypeStruct((8, n), F32),
        compiler_params=_cparams(("arbitrary",)),
        name="ada",
    )(c8, w, b)


def _ctxproj_kernel(x_ref, mod_ref, g_ref, w_ref, b_ref, o_ref, *, row):
    d = x_ref.shape[-1]
    xn = _rms_norm(x_ref[0], g_ref[...])
    sh = mod_ref[row:row + 1, 0:d]
    sc = mod_ref[row:row + 1, d:2 * d]
    hx = (xn * (1.0 + sc) + sh).astype(BF16)
    o_ref[0] = _bdot(hx, w_ref[...]) + b_ref[...]


def _ctxproj(ctx, mod, g, w, b, row):
    bsz, n, d = ctx.shape
    return pl.pallas_call(
        functools.partial(_ctxproj_kernel, row=row),
        grid=(bsz,),
        in_specs=[pl.BlockSpec((1, n, d), lambda i: (i, 0, 0)),
                  pl.BlockSpec(mod.shape, lambda i: (0, 0)),
                  pl.BlockSpec((1, d), lambda i: (0, 0)),
                  pl.BlockSpec(w.shape, lambda i: (0, 0)),
                  pl.BlockSpec((1, w.shape[1]), lambda i: (0, 0))],
        out_specs=pl.BlockSpec((1, n, w.shape[1]), lambda i: (i, 0, 0)),
        out_shape=jax.ShapeDtypeStruct((bsz, n, w.shape[1]), F32),
        compiler_params=_cparams(("arbitrary",)),
        name="ctxproj",
    )(ctx, mod, g, w, b)


def _scan_kernel(*refs, reverse, emit_h, n_chunks):
    refs = list(refs)
    if reverse:
        xc_ref, h0_ref, wg_ref, bg_ref, lam_ref = refs[:5]
        rest = refs[5:]
        bsz, tc, w = xc_ref.shape
    else:
        xr_ref, prev_ref, next_ref, h0_ref, cw_ref, cb_ref, wg_ref, bg_ref, lam_ref = refs[:9]
        rest = refs[9:]
        bsz, tc, w = xr_ref.shape
    h_ref = rest.pop(0) if emit_h else None
    hfin_ref = rest.pop(0)
    xc_out_ref = None if reverse else rest.pop(0)
    xp_ref = None if reverse else rest.pop(0)
    a_ref, b_ref, hs_ref, hst_ref = rest
    nblk = w // LANES
    i = pl.program_id(0)
    chunk = (n_chunks - 1 - i) if reverse else i

    @pl.when(i == 0)
    def _():
        hst_ref[...] = h0_ref[...]

    z = -lam_ref[...]
    softplus = jnp.maximum(z, 0.0) + jnp.log1p(jnp.exp(-jnp.abs(z)))
    nlsh = (-0.5 * LRU_C) * softplus

    if not reverse:
        has_prev = chunk > 0
        has_next = chunk < n_chunks - 1
        for b in range(bsz):
            bp, par = divmod(b, 2)
            for cblk in range(nblk):
                cs = slice(cblk * LANES, (cblk + 1) * LANES)
                xp_ref[bp, cblk, pl.ds(par, 8, stride=2), :] = jnp.where(has_prev, prev_ref[b, :, cs], 0.0)
                xp_ref[bp, cblk, pl.ds(16 + par, tc, stride=2), :] = xr_ref[b, :, cs]
                xp_ref[bp, cblk, pl.ds(16 + 2 * tc + par, 8, stride=2), :] = (
                    jnp.where(has_next, next_ref[b, :, cs], 0.0))

    for b in range(bsz):
        bp, par = divmod(b, 2)
        xc = []
        for cblk in range(nblk):
            cs = slice(cblk * LANES, (cblk + 1) * LANES)
            if reverse:
                xc.append(xc_ref[b, :, cs])
                continue
            acc = cb_ref[:, cs] + cw_ref[0:1, cs] * xp_ref[bp, cblk, pl.ds(2 * 6 + par, tc, stride=2), :]
            for k in range(1, LRU_CONV_K):
                acc = acc + cw_ref[k:k + 1, cs] * xp_ref[bp, cblk, pl.ds(2 * (6 + k) + par, tc, stride=2), :]
            xc_out_ref[b, :, cs] = acc
            xc.append(acc)
        for p in range(LRU_HEADS // 2):
            cs = slice(2 * p * LANES, 2 * (p + 1) * LANES)
            xc_p = jnp.concatenate([xc[2 * p], xc[2 * p + 1]], axis=1)
            th = jnp.tanh(_bdot(xc_p.astype(BF16), wg_ref[p]) + bg_ref[p])
            log_a = nlsh[:, cs] * (th[:, 0:2 * LANES] + 1.0)
            a = jnp.exp(log_a)
            y = jnp.tanh(-log_a) * (1.0 + a * a)
            root = jnp.where(y > 0.0, y * lax.rsqrt(y), 0.0)
            bb = root * ((th[:, 2 * LANES:4 * LANES] + 1.0) * (0.5 * xc_p))
            for jj in range(2):
                j = 2 * p + jj
                a_ref[b, pl.ds(j, tc, stride=nblk), :] = a[:, jj * LANES:(jj + 1) * LANES]
                b_ref[b, pl.ds(j, tc, stride=nblk), :] = bb[:, jj * LANES:(jj + 1) * LANES]

    group = 8
    n_groups = tc // group

    def steps(gidx, hs):
        g = (n_groups - 1 - gidx) if reverse else gidx
        base = pl.multiple_of(g * (group * nblk), group * nblk)
        hs = list(hs)
        for u in range(group):
            row = base + ((group - 1 - u) if reverse else u) * nblk
            for b in range(bsz):
                h = a_ref[b, pl.ds(row, nblk), :] * hs[b] + b_ref[b, pl.ds(row, nblk), :]
                hs_ref[b, pl.ds(row, nblk), :] = h
                hs[b] = h
        return tuple(hs)

    hs = lax.fori_loop(0, n_groups, steps, tuple(hst_ref[b] for b in range(bsz)))
    for b in range(bsz):
        hst_ref[b] = hs[b]
        hfin_ref[b] = hs[b]
    if emit_h:
        for b in range(bsz):
            for j in range(nblk):
                h_ref[b, :, j * LANES:(j + 1) * LANES] = hs_ref[b, pl.ds(j, tc, stride=nblk), :]


def _scan(xin, h0, cw, cb, wg, bg, lam, *, reverse, emit_h):
    bsz, s, w = xin.shape
    tc = min(T_SCAN, s)
    assert s % tc == 0 and w == SUBLANES * LANES and bsz % 2 == 0
    n_chunks = s // tc
    hb = tc // 8

    def cidx(i):
        return (n_chunks - 1 - i) if reverse else i

    tile = lambda: pl.BlockSpec((bsz, tc, w), lambda i: (0, cidx(i), 0))
    state = lambda: pl.BlockSpec((bsz, SUBLANES, LANES), lambda i: (0, 0, 0))
    gate_specs = [pl.BlockSpec(wg.shape, lambda i: (0, 0, 0)),
                  pl.BlockSpec(bg.shape, lambda i: (0, 0, 0)),
                  pl.BlockSpec(lam.shape, lambda i: (0, 0))]
    if reverse:
        args = (xin, h0, wg, bg, lam)
        in_specs = [tile(), state()] + gate_specs
    else:
        args = (xin, xin, xin, h0, cw, cb, wg, bg, lam)
        in_specs = [tile(),
                    pl.BlockSpec((bsz, 8, w), lambda i: (0, jnp.maximum(cidx(i) * hb - 1, 0), 0)),
                    pl.BlockSpec((bsz, 8, w), lambda i: (0, jnp.minimum((cidx(i) + 1) * hb, n_chunks * hb - 1), 0)),
                    state(),
                    pl.BlockSpec(cw.shape, lambda i: (0, 0)),
                    pl.BlockSpec(cb.shape, lambda i: (0, 0))] + gate_specs
    full_seq = jax.ShapeDtypeStruct((bsz, s, w), F32)
    out_specs = [state()]
    out_shape = [jax.ShapeDtypeStruct((bsz, SUBLANES, LANES), F32)]
    if emit_h:
        out_specs = [tile()] + out_specs
        out_shape = [full_seq] + out_shape
    scratch = [pltpu.VMEM((bsz, tc * SUBLANES, LANES), F32),
               pltpu.VMEM((bsz, tc * SUBLANES, LANES), F32),
               pltpu.VMEM((bsz, tc * SUBLANES, LANES), F32),
               pltpu.VMEM((bsz, SUBLANES, LANES), F32)]
    if not reverse:
        out_specs = out_specs + [tile()]
        out_shape = out_shape + [full_seq]
        scratch = [pltpu.VMEM((bsz // 2, w // LANES, 2 * (tc + 16), LANES), F32)] + scratch
    res = pl.pallas_call(
        functools.partial(_scan_kernel, reverse=reverse, emit_h=emit_h, n_chunks=n_chunks),
        grid=(n_chunks,),
        in_specs=in_specs,
        out_specs=out_specs,
        out_shape=out_shape,
        scratch_shapes=scratch,
        compiler_params=_cparams(("arbitrary",)),
        name="scan_bwd" if reverse else "scan_fwd",
    )(*args)
    return tuple(res) if emit_h else (None,) + tuple(res)


CONV_ROWS = 4


def _k1_kernel(x_ref, mod_ref, gmix_ref, win_ref, bin_ref, cw_ref, cb_ref, lng_ref, lnb_ref, wpa_ref,
               ma_ref, gyg_ref, sgb_ref, xr_ref, zp_ref, wb_ref, de_ref, co_ref):
    t, d = x_ref.shape[1], x_ref.shape[2]
    wc = cw_ref.shape[1]
    n_cblk = wc // LANES
    n_quads = t // (CONV_ROWS * GRID_W)
    lo = CONV_PAD
    hi = CONV_PAD + GRID_W
    first = (pl.program_id(0) == 0) & (pl.program_id(1) == 0)

    @pl.when(first)
    def _():
        for k in range(CONV_K):
            for cblk in range(n_cblk):
                row = cw_ref[k:k + 1, cblk * LANES:(cblk + 1) * LANES]
                wb_ref[k, cblk] = jnp.broadcast_to(row, (2 * GRID_W, LANES)).astype(BF16)

    xn = _rms_norm(x_ref[0], gmix_ref[...])
    sh1 = mod_ref[0, :, 0:d]
    sc1 = mod_ref[0, :, d:2 * d]
    hx = (xn * (1.0 + sc1) + sh1).astype(BF16)

    def proj(c0, c1):
        return _bdot(hx, win_ref[:, c0:c1]) + bin_ref[:, c0:c1]

    zglu = proj(0, wc) * _sigmoid(proj(wc, 2 * wc))

    zp_ref[:, :, 0:2 * lo, :] = jnp.zeros((n_quads, n_cblk, 2 * lo, LANES), U32)
    zp_ref[:, :, 2 * hi:2 * (hi + CONV_PAD), :] = jnp.zeros((n_quads, n_cblk, 2 * CONV_PAD, LANES), U32)
    for q in range(n_quads):
        for par in range(2):
            r0 = (CONV_ROWS * q + 2 * par) * GRID_W
            for cblk in range(n_cblk):
                cs = slice(cblk * LANES, (cblk + 1) * LANES)
                zp_ref[q, cblk, pl.ds(2 * lo + par, GRID_W, stride=2), :] = _pack_bf16_pair(
                    zglu[r0:r0 + GRID_W, cs], zglu[r0 + GRID_W:r0 + 2 * GRID_W, cs])

    o = 2 * wc
    gyg_ref[0] = _gelu_tanh(proj(o, o + d)).astype(BF16)
    xr_ref[0] = proj(o + d, o + 2 * d)

    def conv_block(cblk):
        bias = cb_ref[cblk]
        for q in range(n_quads):
            for par in range(2):
                r0 = (CONV_ROWS * q + 2 * par) * GRID_W
                acc = jnp.zeros((2 * GRID_W, LANES), F32)
                for k in range(CONV_K):
                    off = lo - CONV_K // 2 + k
                    words = zp_ref[q, cblk, pl.ds(2 * off + par, GRID_W, stride=2), :]
                    acc = acc + pltpu.bitcast(words, BF16).astype(F32) * wb_ref[k, cblk].astype(F32)
                slot = 2 * q + par
                de_ref[slot] = acc
                co_ref[cblk, r0:r0 + GRID_W, :] = de_ref[slot, pl.ds(0, GRID_W, stride=2), :] + bias
                co_ref[cblk, r0 + GRID_W:r0 + 2 * GRID_W, :] = de_ref[slot, pl.ds(1, GRID_W, stride=2), :] + bias

    def conv_body(cblk, carry):
        conv_block(cblk)
        return carry

    lax.fori_loop(0, n_cblk, conv_body, 0)

    c = jnp.concatenate([co_ref[cblk] for cblk in range(n_cblk)], axis=1)
    mu = jnp.mean(c, axis=-1, keepdims=True)
    cc = c - mu
    y = cc * lax.rsqrt(jnp.mean(cc * cc, axis=-1, keepdims=True) + EPS) * lng_ref[...] + lnb_ref[...]
    act = _silu(y).astype(BF16)
    sga = _sigmoid(proj(o + 2 * d, o + 3 * d))
    sgb_ref[0] = _sigmoid(proj(o + 3 * d, o + 4 * d)).astype(BF16)
    ma_ref[0] = (sga * _bdot(act, wpa_ref[...])).astype(BF16)


def _k1(x, mod, gmix, win, bin_, cw, cb, lng, lnb, wpa):
    bsz, s, d = x.shape
    t = T_TOK
    assert s % t == 0 and t % (CONV_ROWS * GRID_W) == 0
    wc = cw.shape[1]
    n_cblk = wc // LANES
    n_quads = t // (CONV_ROWS * GRID_W)
    cb = cb.reshape(n_cblk, 1, LANES)

    def full(a):
        return pl.BlockSpec(a.shape, lambda b, i: (0,) * a.ndim)

    def const(a):
        return pl.BlockSpec(a.shape, lambda b, i: (0,) * a.ndim, pipeline_mode=pl.Buffered(1))

    tok = lambda: pl.BlockSpec((1, t, d), lambda b, i: (b, i, 0))
    return pl.pallas_call(
        _k1_kernel,
        grid=(bsz, s // t),
        in_specs=[tok(), pl.BlockSpec((1, 1, mod.shape[1]), lambda b, i: (b, 0, 0)), full(gmix), const(win),
                  full(bin_), full(cw), full(cb), full(lng), full(lnb), const(wpa)],
        out_specs=[tok(), tok(), tok(), tok()],
        out_shape=[jax.ShapeDtypeStruct((bsz, s, d), BF16),
                   jax.ShapeDtypeStruct((bsz, s, d), BF16),
                   jax.ShapeDtypeStruct((bsz, s, d), BF16),
                   jax.ShapeDtypeStruct((bsz, s, d), F32)],
        scratch_shapes=[pltpu.VMEM((n_quads, n_cblk, 2 * (GRID_W + 2 * CONV_PAD), LANES), U32),
                        pltpu.VMEM((CONV_K, n_cblk, 2 * GRID_W, LANES), BF16),
                        pltpu.VMEM((2 * n_quads, 2 * GRID_W, LANES), F32),
                        pltpu.VMEM((n_cblk, t, LANES), F32)],
        compiler_params=_cparams(("arbitrary", "arbitrary")),
        name="k1",
    )(x, mod[:, None, :], gmix, win, bin_, cw, cb, lng, lnb, wpa)


def _post_kernel(x_ref, hf_ref, hb_ref, gyg_ref, ma_ref, sgb_ref, mod_ref, wpb_ref, wo_ref, gffn_ref,
                 wrt_ref, brt_ref, x1_ref, hx_ref, rti_ref):
    d = x_ref.shape[2]
    b = pl.program_id(0)
    y_rec = hf_ref[0] + hb_ref[0]
    b_lat = _bdot((gyg_ref[0].astype(F32) * y_rec).astype(BF16), wpb_ref[...])
    merged = ma_ref[0].astype(F32) + sgb_ref[0].astype(F32) * b_lat
    res = _bdot(merged.astype(BF16), wo_ref[...])
    gt1 = mod_ref[pl.ds(b, 1), 2 * d:3 * d]
    x1 = x_ref[0] + gt1 * res
    x1_ref[0] = x1
    sh2 = mod_ref[pl.ds(b, 1), 3 * d:4 * d]
    sc2 = mod_ref[pl.ds(b, 1), 4 * d:5 * d]
    hm = _rms_norm(x1, gffn_ref[...]) * (1.0 + sc2) + sh2
    half = d // 2
    hx_ref[0, :, 0:half] = _pack_bf16_pair(hm[:, 0:half], hm[:, half:d])

    nt = (((1,), (1,)), ((), ()))
    hm_hi = hm.astype(BF16)
    hm_lo = (hm - hm_hi.astype(F32)).astype(BF16)
    lg = (lax.dot_general(wrt_ref[0], hm_hi, nt, preferred_element_type=F32)
          + lax.dot_general(wrt_ref[0], hm_lo, nt, preferred_element_type=F32)
          + lax.dot_general(wrt_ref[1], hm_hi, nt, preferred_element_type=F32)) + brt_ref[...]
    gl = [lg[k:k + 1, :] for k in range(N_GROUPS)]
    gmax = jnp.maximum(jnp.maximum(gl[0], gl[1]), jnp.maximum(gl[2], gl[3]))
    gsel = jnp.where(gl[0] == gmax, 0, jnp.where(gl[1] == gmax, 1, jnp.where(gl[2] == gmax, 2, 3)))
    p_grp = 1.0 / (jnp.exp(gl[0] - gmax) + jnp.exp(gl[1] - gmax) + jnp.exp(gl[2] - gmax) + jnp.exp(gl[3] - gmax))
    el = []
    for j in range(EXPERTS_PER_GROUP):
        rows = [lg[N_GROUPS + g * EXPERTS_PER_GROUP + j:N_GROUPS + g * EXPERTS_PER_GROUP + j + 1, :]
                for g in range(N_GROUPS)]
        el.append(jnp.where(gsel == 0, rows[0], jnp.where(gsel == 1, rows[1], jnp.where(gsel == 2, rows[2], rows[3]))))
    m1 = jnp.maximum(jnp.maximum(el[0], el[1]), jnp.maximum(el[2], el[3]))
    i1 = jnp.where(el[0] == m1, 0, jnp.where(el[1] == m1, 1, jnp.where(el[2] == m1, 2, 3)))
    neg = jnp.float32(-jnp.inf)
    rest = [jnp.where(i1 == j, neg, el[j]) for j in range(EXPERTS_PER_GROUP)]
    m2 = jnp.maximum(jnp.maximum(rest[0], rest[1]), jnp.maximum(rest[2], rest[3]))
    i2 = jnp.where((rest[0] == m2) & (i1 != 0), 0,
                   jnp.where((rest[1] == m2) & (i1 != 1), 1, jnp.where((rest[2] == m2) & (i1 != 2), 2, 3)))
    e21 = jnp.exp(m2 - m1)
    den = 1.0 / (1.0 + e21)
    w1 = p_grp * den
    w2 = p_grp * (e21 * den)
    first_lo = i1 < i2
    lo = jnp.where(first_lo, i1, i2)
    hi = jnp.where(first_lo, i2, i1)
    w_lo = jnp.where(first_lo, w1, w2)
    w_hi = jnp.where(first_lo, w2, w1)
    pair = jnp.where(lo == 0, hi - 1, jnp.where(lo == 1, hi + 1, 5))
    bucket = gsel * N_PAIRS + pair
    t = bucket.shape[1]
    rti_ref[...] = jnp.concatenate([bucket.astype(I32), jnp.zeros((7, t), I32)], axis=0)
    wrows = jnp.concatenate([w_lo, w_hi, jnp.zeros((LANES - 2, t), F32)], axis=0)
    hx_ref[0, :, half:half + LANES] = lax.bitcast_convert_type(wrows.T, U32)


def _post(x, hf, hb, gyg, ma, sgb, mod, wpb, wo, gffn, wrt, brt):
    bsz, s, d = x.shape
    t = brt.shape[1]
    assert s % t == 0
    nt = s // t
    n = bsz * s

    def full(a):
        return pl.BlockSpec(a.shape, lambda b, i: (0,) * a.ndim)

    def const(a):
        return pl.BlockSpec(a.shape, lambda b, i: (0,) * a.ndim, pipeline_mode=pl.Buffered(1))

    tok = lambda: pl.BlockSpec((1, t, d), lambda b, i: (b, i, 0))
    rt = lambda: pl.BlockSpec((8, t), lambda b, i: (0, b * nt + i))
    return pl.pallas_call(
        _post_kernel,
        grid=(bsz, nt),
        in_specs=[tok(), tok(), tok(), tok(), tok(), tok(), full(mod), const(wpb), const(wo), full(gffn), full(wrt),
                  full(brt)],
        out_specs=[tok(), pl.BlockSpec((1, t, d // 2 + LANES), lambda b, i: (b, i, 0)), rt()],
        out_shape=[jax.ShapeDtypeStruct((bsz, s, d), F32),
                   jax.ShapeDtypeStruct((bsz, s, d // 2 + LANES), U32),
                   jax.ShapeDtypeStruct((8, n), I32)],
        compiler_params=_cparams(("arbitrary", "arbitrary")),
        name="post",
    )(x, hf, hb, gyg, ma, sgb, mod, wpb, wo, gffn, wrt, brt)


def _sort_kernel(bk_ref, slot_ref, meta_ref, *, tm):
    bk = bk_ref[...]
    r = bk.shape[0]
    nt_lanes = meta_ref.shape[1]
    shift = tm.bit_length() - 1
    upper = (lax.broadcasted_iota(I32, (LANES, LANES), 0) < lax.broadcasted_iota(I32, (LANES, LANES), 1))
    upper = upper.astype(F32).astype(BF16)
    lower = (lax.broadcasted_iota(I32, (r, r), 1) < lax.broadcasted_iota(I32, (r, r), 0))
    lower = lower.astype(F32).astype(BF16)
    ones = jnp.ones((LANES, LANES), BF16)
    tile_pos = lax.broadcasted_iota(I32, (1, nt_lanes), 1) * tm
    start = jnp.zeros((1, LANES), I32)
    start_t = jnp.zeros((1, nt_lanes), I32)
    slot = jnp.zeros((r, LANES), I32)
    tile_bucket = jnp.zeros((1, nt_lanes), I32)
    for k in range(N_BUCKETS):
        m = bk == k
        mb = m.astype(F32).astype(BF16)
        pre = _bdot(mb, upper)
        tot = _bdot(mb, ones)
        rowpre = _bdot(lower, tot.astype(BF16))
        rank = (pre + rowpre).astype(I32)
        cnt = (rowpre[r - 1:r, :] + tot[r - 1:r, :]).astype(I32)
        slot = jnp.where(m, start + rank, slot)
        padded = ((cnt + (tm - 1)) >> shift) << shift
        start = start + padded
        start_t = start_t + jnp.concatenate([padded] * (nt_lanes // LANES), axis=1)
        tile_bucket = tile_bucket + (tile_pos >= start_t).astype(I32)
    slot_ref[...] = slot
    valid = (tile_pos < start_t).astype(I32)
    tb = jnp.minimum(tile_bucket, N_BUCKETS - 1)
    g = (tb * 43) >> 8
    pair = tb - N_PAIRS * g
    ge3 = (pair >= 3).astype(I32)
    ge5 = (pair >= 5).astype(I32)
    lo = ge3 + ge5
    hi = pair + 1 - 2 * ge3 - ge5
    e_lo = g * EXPERTS_PER_GROUP + lo
    e_hi = g * EXPERTS_PER_GROUP + hi
    meta_ref[...] = jnp.concatenate([e_lo, e_hi, valid, jnp.zeros((5, nt_lanes), I32)], axis=0)


def _sort(bucket2d, tm, nt_lanes):
    r = bucket2d.shape[0]
    return pl.pallas_call(
        functools.partial(_sort_kernel, tm=tm),
        grid=(1,),
        in_specs=[pl.BlockSpec((r, LANES), lambda i: (0, 0))],
        out_specs=[pl.BlockSpec((r, LANES), lambda i: (0, 0)),
                   pl.BlockSpec((8, nt_lanes), lambda i: (0, 0))],
        out_shape=[jax.ShapeDtypeStruct((r, LANES), I32),
                   jax.ShapeDtypeStruct((8, nt_lanes), I32)],
        compiler_params=_cparams(("arbitrary",)),
        name="bucket_sort",
    )(bucket2d)


def _moe_kernel(elo_ref, ehi_ref, valid_ref, h_ref, wg_lo, wg_hi, wu_lo, wu_hi, wd_lo, wd_hi, *rest):
    o_ref = rest[-1]
    i = pl.program_id(0)
    half = o_ref.shape[1]

    @pl.when(valid_ref[i] != 0)
    def _():
        h_a, h_b = _unpack_bf16_pair(h_ref[:, 0:half])
        h = jnp.concatenate([h_a, h_b], axis=1).astype(BF16)
        w_lo = lax.bitcast_convert_type(h_ref[:, half:half + 1], F32)
        w_hi = lax.bitcast_convert_type(h_ref[:, half + 1:half + 2], F32)
        act_lo = _silu(_bdot(h, wg_lo[0])) * _bdot(h, wu_lo[0]) * w_lo
        act_hi = _silu(_bdot(h, wg_hi[0])) * _bdot(h, wu_hi[0]) * w_hi
        o = _bdot(act_lo.astype(BF16), wd_lo[0]) + _bdot(act_hi.astype(BF16), wd_hi[0])
        o_ref[...] = _pack_bf16_pair(o[:, 0:half], o[:, half:2 * half])

    @pl.when(valid_ref[i] == 0)
    def _():
        o_ref[...] = jnp.zeros(o_ref.shape, U32)


def _moe(e_lo, e_hi, valid, hs, wgate, wup, wdown, *, tile0, npad, prev):
    rows, wrow = hs.shape
    tm = T_MOE
    n_tiles = rows // tm
    ne, d, f = wgate.shape
    lo3 = lambda i, elo, ehi, va: (elo[i], 0, 0)
    hi3 = lambda i, elo, ehi, va: (ehi[i], 0, 0)
    row = lambda i, elo, ehi, va: (i, 0)
    in_specs = [pl.BlockSpec((tm, wrow), row),
                pl.BlockSpec((1, d, f), lo3), pl.BlockSpec((1, d, f), hi3),
                pl.BlockSpec((1, d, f), lo3), pl.BlockSpec((1, d, f), hi3),
                pl.BlockSpec((1, f, d), lo3), pl.BlockSpec((1, f, d), hi3)]
    args = [e_lo, e_hi, valid, hs, wgate, wgate, wup, wup, wdown, wdown]
    aliases = {}
    if prev is not None:
        in_specs.append(pl.BlockSpec(memory_space=pl.ANY))
        aliases = {len(args): 0}
        args.append(prev)
    grid_spec = pltpu.PrefetchScalarGridSpec(
        num_scalar_prefetch=3,
        grid=(n_tiles,),
        in_specs=in_specs,
        out_specs=pl.BlockSpec((tm, d // 2), lambda i, elo, ehi, va: (i + tile0, 0)),
    )
    return pl.pallas_call(
        _moe_kernel,
        grid_spec=grid_spec,
        out_shape=jax.ShapeDtypeStruct((npad, d // 2), U32),
        input_output_aliases=aliases,
        compiler_params=_cparams(("arbitrary",)),
        name="moe",
    )(*args)


SC_CORES = 2
SC_SUBCORES = 16
SC_LANES = 16
SC_WORKERS = SC_CORES * SC_SUBCORES
SC_ROWS = 64


def _sc_worker_id():
    return lax.axis_index("s") * SC_CORES + lax.axis_index("c")


def _sc_chunk_rows(rows):
    return next(c for c in (SC_ROWS, SC_ROWS // 2, SC_ROWS // 4) if rows % c == 0)


def _sc_gather_rows(src_hbm, idx_v, out_hbm, out_base, n_chunks, bufs, sems):
    chunk = bufs[0].shape[0]

    def gather(c, b):
        off = pl.multiple_of(c * chunk, chunk)
        return pltpu.make_async_copy(src_hbm.at[idx_v.at[pl.ds(off, chunk)]], bufs[b], sems[b])

    def finish(c, b):
        gather(c, b).wait()
        pltpu.sync_copy(bufs[b], out_hbm.at[pl.ds(out_base + pl.multiple_of(c * chunk, chunk), chunk)])

    gather(0, 0).start()

    @pl.loop(0, n_chunks // 2)
    def _(p):
        c = 2 * p
        gather(c + 1, 1).start()
        finish(c, 0)

        @pl.when(c + 2 < n_chunks)
        def _():
            gather(c + 2, 0).start()

        finish(c + 1, 1)

    if n_chunks % 2:
        finish(n_chunks - 1, 0)


def _sc_dispatch(hx, slot, slot0, n_slots):
    n, wrow = hx.shape
    assert n_slots % (SC_WORKERS * SC_LANES) == 0 and n % SC_LANES == 0
    rows = n_slots // SC_WORKERS
    chunk = _sc_chunk_rows(rows)
    mesh = plsc.VectorSubcoreMesh(core_axis_name="c", subcore_axis_name="s")

    @functools.partial(
        pl.kernel, mesh=mesh,
        out_type=jax.ShapeDtypeStruct((n_slots, wrow), hx.dtype),
        scratch_types=[pltpu.VMEM((n,), I32),
                       pltpu.VMEM((rows,), I32),
                       pltpu.VMEM((chunk, wrow), hx.dtype),
                       pltpu.VMEM((chunk, wrow), hx.dtype),
                       pltpu.SemaphoreType.DMA,
                       pltpu.SemaphoreType.DMA],
        compiler_params=pltpu.CompilerParams(needs_layout_passes=False),
        name="sc_dispatch",
    )
    def k(hx_hbm, slot_hbm, out_hbm, slot_v, tok_v, rows_a, rows_b, sem_a, sem_b):
        out_base = _sc_worker_id() * rows
        base = slot0 + out_base
        pltpu.sync_copy(slot_hbm, slot_v)

        lane = lax.iota(I32, SC_LANES)

        @pl.loop(0, rows // SC_LANES)
        def _(i):
            filler = base + i * SC_LANES + lane
            tok_v[pl.ds(i * SC_LANES, SC_LANES)] = lax.rem(filler, jnp.int32(n))

        @plsc.parallel_loop(0, n // SC_LANES, unroll=8)
        def _(i):
            loc = slot_v[pl.ds(i * SC_LANES, SC_LANES)] - base
            mine = (loc >= 0) & (loc < rows)
            plsc.store_scatter(tok_v, [loc], i * SC_LANES + lane, mask=mine)

        _sc_gather_rows(hx_hbm, tok_v, out_hbm, out_base, rows // chunk, (rows_a, rows_b), (sem_a, sem_b))

    return k(hx, slot)


def _sc_combine(mo_sorted, slot):
    n = slot.shape[0]
    wrow = mo_sorted.shape[1]
    assert n % (SC_WORKERS * SC_LANES) == 0
    toks = n // SC_WORKERS
    chunk = _sc_chunk_rows(toks)
    mesh = plsc.VectorSubcoreMesh(core_axis_name="c", subcore_axis_name="s")

    @functools.partial(
        pl.kernel, mesh=mesh,
        out_type=jax.ShapeDtypeStruct((n, wrow), mo_sorted.dtype),
        scratch_types=[pltpu.VMEM((toks,), I32),
                       pltpu.VMEM((chunk, wrow), mo_sorted.dtype),
                       pltpu.VMEM((chunk, wrow), mo_sorted.dtype),
                       pltpu.SemaphoreType.DMA,
                       pltpu.SemaphoreType.DMA],
        compiler_params=pltpu.CompilerParams(needs_layout_passes=False),
        name="sc_combine",
    )
    def k(mo_hbm, slot_hbm, out_hbm, idx_v, rows_a, rows_b, sem_a, sem_b):
        base = _sc_worker_id() * toks
        pltpu.sync_copy(slot_hbm.at[pl.ds(base, toks)], idx_v)
        _sc_gather_rows(mo_hbm, idx_v, out_hbm, base, toks // chunk, (rows_a, rows_b), (sem_a, sem_b))

    return k(mo_sorted, slot)


def _final_kernel(x1_ref, mo_ref, mod_ref, g_ref, *rest, b0):
    o_ref = rest[-1]
    d = x1_ref.shape[2]
    b = pl.program_id(0) + b0
    gt2 = mod_ref[pl.ds(b, 1), 5 * d:6 * d]
    mo = jnp.concatenate(_unpack_bf16_pair(mo_ref[0]), axis=1)
    o_ref[0] = _rms_norm(x1_ref[0] + gt2 * mo, g_ref[...])


def _final(x1, mo, mod, g, *, b0, prev):
    bsz, s, d = x1.shape
    nb = mo.shape[0]
    t = min(T_FINAL, s)
    assert s % t == 0
    tok = lambda: pl.BlockSpec((1, t, d), lambda b, i: (b + b0, i, 0))
    in_specs = [tok(), pl.BlockSpec((1, t, d // 2), lambda b, i: (b, i, 0)),
                pl.BlockSpec(mod.shape, lambda b, i: (0, 0)), pl.BlockSpec((1, d), lambda b, i: (0, 0))]
    args = [x1, mo, mod, g]
    aliases = {}
    if prev is not None:
        in_specs.append(pl.BlockSpec(memory_space=pl.ANY))
        aliases = {len(args): 0}
        args.append(prev)
    return pl.pallas_call(
        functools.partial(_final_kernel, b0=b0),
        grid=(nb, s // t),
        in_specs=in_specs,
        out_specs=tok(),
        out_shape=jax.ShapeDtypeStruct((bsz, s, d), F32),
        input_output_aliases=aliases,
        compiler_params=_cparams(("arbitrary", "arbitrary")),
        name="final",
    )(*args)


def _pair_blockdiag(w):
    h, hd, _ = w.shape
    w2 = w.reshape(h // 2, 2, hd, hd)
    z = jnp.zeros((h // 2, hd, hd), w.dtype)
    top = jnp.concatenate([w2[:, 0], z], axis=2)
    bot = jnp.concatenate([z, w2[:, 1]], axis=2)
    return jnp.concatenate([top, bot], axis=1)


def _gate_params(w_r, b_r, w_i, b_i):
    wg = (0.5 * jnp.concatenate([_pair_blockdiag(w_r), _pair_blockdiag(w_i)], axis=2)).astype(BF16)
    h, hd = b_r.shape
    bg = 0.5 * jnp.concatenate([b_r.reshape(h // 2, 1, 2 * hd), b_i.reshape(h // 2, 1, 2 * hd)], axis=2)
    return wg, bg


def kernel(x, c, ctx, c_ctx, w_ada, b_ada, g_mix, w_in, b_in, conv_w, conv_b, ln_g, ln_b, w_pa, lru_conv_w, lru_conv_b, w_r_f, b_r_f, w_i_f, b_i_f, lam_f, w_r_b, b_r_b, w_i_b, b_i_b, lam_b, w_pb, w_o, g_ffn, w_grp, b_grp, w_er, b_er, w_gate, w_up, w_down, g_final):
    bsz, s, d = x.shape
    depth = w_ada.shape[0]
    assert depth == 1 and bsz + 1 <= 8
    l = 0
    n = bsz * s
    wc = conv_w.shape[2]
    wl = lru_conv_w.shape[2]
    rec0 = 2 * wc + wl
    ctx_row = bsz

    c8 = jnp.concatenate([c, c_ctx[None, :], jnp.zeros((8 - bsz - 1, d), F32)], axis=0)
    mod = _ada(c8, w_ada[l], b_ada[l][None, :])

    win = _to_bf16(w_in[l])
    bin_ = b_in[l][None, :]
    gmix = g_mix[l][None, :]
    lcw = lru_conv_w[l]
    lcb = lru_conv_b[l][None, :]
    wg_f, bg_f = _gate_params(w_r_f[l], b_r_f[l], w_i_f[l], b_i_f[l])
    wg_b, bg_b = _gate_params(w_r_b[l], b_r_b[l], w_i_b[l], b_i_b[l])
    lamf = lam_f[l][None, :]
    lamb = lam_b[l][None, :]

    xr_c = _ctxproj(ctx, mod, gmix, win[:, rec0:rec0 + wl], bin_[:, rec0:rec0 + wl], ctx_row)
    zero_state = jnp.zeros((bsz, SUBLANES, LANES), F32)
    _, h0f, xc_c = _scan(xr_c, zero_state, lcw, lcb, wg_f, bg_f, lamf, reverse=False, emit_h=False)
    _, h0b = _scan(xc_c, zero_state, lcw, lcb, wg_b, bg_b, lamb, reverse=True, emit_h=False)

    ma, gyg, sgb, xr = _k1(x, mod, gmix, win, bin_, conv_w[l], conv_b[l][None, :], ln_g[l][None, :],
                           ln_b[l][None, :], w_pa[l].astype(BF16))
    hf, _, xc = _scan(xr, h0f, lcw, lcb, wg_f, bg_f, lamf, reverse=False, emit_h=True)
    hb, _ = _scan(xc, h0b, lcw, lcb, wg_b, bg_b, lamb, reverse=True, emit_h=True)

    n_rt = 32
    wrt = jnp.concatenate([w_grp[l].T, w_er[l].T, jnp.zeros((n_rt - N_GROUPS * (1 + EXPERTS_PER_GROUP), d), F32)], axis=0)
    brt = jnp.concatenate([b_grp[l], b_er[l], jnp.zeros((n_rt - N_GROUPS * (1 + EXPERTS_PER_GROUP),), F32)])
    brt = jnp.broadcast_to(brt[:, None], (n_rt, min(T_POST, s)))
    wrt_hi = wrt.astype(BF16)
    wrt = jnp.stack([wrt_hi, (wrt - wrt_hi.astype(F32)).astype(BF16)])
    x1, hx, rti = _post(x, hf, hb, gyg, ma, sgb, mod, w_pb[l].astype(BF16), w_o[l].astype(BF16),
                        g_ffn[l][None, :], wrt, brt)

    tm = T_MOE
    n_tiles = n // tm + N_BUCKETS
    npad = n_tiles * tm
    nt_lanes = ((n_tiles + LANES - 1) // LANES) * LANES
    slot2d, meta = _sort(rti[0].reshape(n // LANES, LANES), tm, nt_lanes)
    slot = slot2d.reshape(n)
    assert n_tiles % MOE_PARTS == 0
    part_tiles = n_tiles // MOE_PARTS
    hx2d = hx.reshape(n, hx.shape[2])
    wgate, wup, wdown = _to_bf16(w_gate[l]), _to_bf16(w_up[l]), _to_bf16(w_down[l])
    hs_parts = [_sc_dispatch(hx2d, slot, p * part_tiles * tm, part_tiles * tm) for p in range(MOE_PARTS)]
    mo_sorted = None
    for p in range(MOE_PARTS):
        t0 = p * part_tiles
        mo_sorted = _moe(meta[0, t0:t0 + part_tiles], meta[1, t0:t0 + part_tiles], meta[2, t0:t0 + part_tiles],
                         hs_parts[p], wgate, wup, wdown, tile0=t0, npad=npad, prev=mo_sorted)
    assert bsz % FINAL_PARTS == 0
    nb = bsz // FINAL_PARTS
    out = None
    for p in range(FINAL_PARTS):
        mo = _sc_combine(mo_sorted, slot[p * nb * s:(p + 1) * nb * s]).reshape(nb, s, d // 2)
        out = _final(x1, mo, mod, g_final[None, :], b0=p * nb, prev=out)
    return out
```

```python
import functools

import jax
import jax.numpy as jnp
from jax import lax
from jax.experimental import pallas as pl
from jax.experimental.pallas import tpu as pltpu
from jax.experimental.pallas import tpu_sc as plsc

F32 = jnp.float32
BF16 = jnp.bfloat16
I32 = jnp.int32
U32 = jnp.uint32
HIGHEST = lax.Precision.HIGHEST

EPS = 1e-6
GRID_W = 64
CONV_K = 31
CONV_PAD = 16
LRU_CONV_K = 4
LRU_C = 8.0
LRU_HEADS = 8
N_GROUPS = 4
EXPERTS_PER_GROUP = 4
N_PAIRS = 6
N_BUCKETS = N_GROUPS * N_PAIRS
LANES = 128
SUBLANES = 8
VMEM_LIMIT = 56 * 1024 * 1024

T_TOK = 512
T_POST = 512
T_FINAL = 1024
T_SCAN = 256
T_MOE = 256
FINAL_PARTS = 2
CAST_BLOCK_BYTES = 4 * 1024 * 1024


def _sigmoid(x):
    return 0.5 * (jnp.tanh(0.5 * x) + 1.0)


def _silu(x):
    return x * _sigmoid(x)


def _gelu_tanh(x):
    return 0.5 * x * (1.0 + jnp.tanh(0.7978845608028654 * (x + 0.044715 * (x * x * x))))


def _rms_norm(x, g):
    return x * lax.rsqrt(jnp.mean(x * x, axis=-1, keepdims=True) + EPS) * g


def _bdot(a, b):
    return jnp.dot(a, b, preferred_element_type=F32)


def _pack_bf16_pair(lo, hi):
    lo_bits = lax.bitcast_convert_type(lo.astype(BF16).astype(F32), U32)
    hi_bits = lax.bitcast_convert_type(hi.astype(BF16).astype(F32), U32)
    return (hi_bits & jnp.uint32(0xFFFF0000)) | (lo_bits >> 16)


def _unpack_bf16_pair(u):
    lo = lax.bitcast_convert_type(u << 16, F32)
    hi = lax.bitcast_convert_type(u & jnp.uint32(0xFFFF0000), F32)
    return lo, hi


def _cparams(sem):
    return pltpu.CompilerParams(dimension_semantics=sem, vmem_limit_bytes=VMEM_LIMIT)


def _cast_kernel(x_ref, o_ref):
    o_ref[...] = x_ref[...].astype(o_ref.dtype)


def _to_bf16(w):
    shape = w.shape
    w2 = w.reshape(-1, shape[-1])
    rows, cols = w2.shape
    blk = rows
    while blk % 2 == 0 and blk > 16 and blk * cols * 4 > CAST_BLOCK_BYTES:
        blk //= 2
    out = pl.pallas_call(
        _cast_kernel,
        grid=(rows // blk,),
        in_specs=[pl.BlockSpec((blk, cols), lambda i: (i, 0))],
        out_specs=pl.BlockSpec((blk, cols), lambda i: (i, 0)),
        out_shape=jax.ShapeDtypeStruct((rows, cols), BF16),
        compiler_params=_cparams(("arbitrary",)),
        name="to_bf16",
    )(w2)
    return out.reshape(shape)


def _ada_kernel(c_ref, w_ref, b_ref, o_ref):
    s = _silu(c_ref[...])
    o_ref[...] = jnp.dot(s, w_ref[...], precision=HIGHEST, preferred_element_type=F32) + b_ref[...]


def _ada(c8, w, b):
    d, n = w.shape
    blk = 1024
    return pl.pallas_call(
        _ada_kernel,
        grid=(n // blk,),
        in_specs=[pl.BlockSpec((8, d), lambda j: (0, 0)),
                  pl.BlockSpec((d, blk), lambda j: (0, j)),
                  pl.BlockSpec((1, blk), lambda j: (0, j))],
        out_specs=pl.BlockSpec((8, blk), lambda j: (0, j)),
        out_shape=jax.ShapeDtypeStruct((8, n), F32),
        compiler_params=_cparams(("arbitrary",)),
        name="ada",
    )(c8, w, b)


def _ctxproj_kernel(x_ref, mod_ref, g_ref, w_ref, b_ref, o_ref, *, row):
    d = x_ref.shape[-1]
    xn = _rms_norm(x_ref[0], g_ref[...])
    sh = mod_ref[row:row + 1, 0:d]
    sc = mod_ref[row:row + 1, d:2 * d]
    hx = (xn * (1.0 + sc) + sh).astype(BF16)
    o_ref[0] = _bdot(hx, w_ref[...]) + b_ref[...]


def _ctxproj(ctx, mod, g, w, b, row):
    bsz, n, d = ctx.shape
    return pl.pallas_call(
        functools.partial(_ctxproj_kernel, row=row),
        grid=(bsz,),
        in_specs=[pl.BlockSpec((1, n, d), lambda i: (i, 0, 0)),
                  pl.BlockSpec(mod.shape, lambda i: (0, 0)),
                  pl.BlockSpec((1, d), lambda i: (0, 0)),
                  pl.BlockSpec(w.shape, lambda i: (0, 0)),
                  pl.BlockSpec((1, w.shape[1]), lambda i: (0, 0))],
        out_specs=pl.BlockSpec((1, n, w.shape[1]), lambda i: (i, 0, 0)),
        out_shape=jax.ShapeDtypeStruct((bsz, n, w.shape[1]), F32),
        compiler_params=_cparams(("arbitrary",)),
        name="ctxproj",
    )(ctx, mod, g, w, b)


def _scan_kernel(*refs, reverse, emit_h, n_chunks):
    refs = list(refs)
    if reverse:
        xc_ref, h0_ref, wg_ref, bg_ref, lam_ref = refs[:5]
        rest = refs[5:]
        bsz, tc, w = xc_ref.shape
    else:
        xr_ref, prev_ref, next_ref, h0_ref, cw_ref, cb_ref, wg_ref, bg_ref, lam_ref = refs[:9]
        rest = refs[9:]
        bsz, tc, w = xr_ref.shape
    h_ref = rest.pop(0) if emit_h else None
    hfin_ref = rest.pop(0)
    xc_out_ref = None if reverse else rest.pop(0)
    xp_ref = None if reverse else rest.pop(0)
    a_ref, b_ref, hs_ref, hst_ref = rest
    nblk = w // LANES
    i = pl.program_id(0)
    chunk = (n_chunks - 1 - i) if reverse else i

    @pl.when(i == 0)
    def _():
        hst_ref[...] = h0_ref[...]

    z = -lam_ref[...]
    softplus = jnp.maximum(z, 0.0) + jnp.log1p(jnp.exp(-jnp.abs(z)))
    nlsh = (-0.5 * LRU_C) * softplus

    if not reverse:
        has_prev = chunk > 0
        has_next = chunk < n_chunks - 1
        for b in range(bsz):
            bp, par = divmod(b, 2)
            for cblk in range(nblk):
                cs = slice(cblk * LANES, (cblk + 1) * LANES)
                xp_ref[bp, cblk, pl.ds(par, 8, stride=2), :] = jnp.where(has_prev, prev_ref[b, :, cs], 0.0)
                xp_ref[bp, cblk, pl.ds(16 + par, tc, stride=2), :] = xr_ref[b, :, cs]
                xp_ref[bp, cblk, pl.ds(16 + 2 * tc + par, 8, stride=2), :] = (
                    jnp.where(has_next, next_ref[b, :, cs], 0.0))

    for b in range(bsz):
        bp, par = divmod(b, 2)
        xc = []
        for cblk in range(nblk):
            cs = slice(cblk * LANES, (cblk + 1) * LANES)
            if reverse:
                xc.append(xc_ref[b, :, cs])
                continue
            acc = cb_ref[:, cs] + cw_ref[0:1, cs] * xp_ref[bp, cblk, pl.ds(2 * 6 + par, tc, stride=2), :]
            for k in range(1, LRU_CONV_K):
                acc = acc + cw_ref[k:k + 1, cs] * xp_ref[bp, cblk, pl.ds(2 * (6 + k) + par, tc, stride=2), :]
            xc_out_ref[b, :, cs] = acc
            xc.append(acc)
        for p in range(LRU_HEADS // 2):
            cs = slice(2 * p * LANES, 2 * (p + 1) * LANES)
            xc_p = jnp.concatenate([xc[2 * p], xc[2 * p + 1]], axis=1)
            th = jnp.tanh(_bdot(xc_p.astype(BF16), wg_ref[p]) + bg_ref[p])
            log_a = nlsh[:, cs] * (th[:, 0:2 * LANES] + 1.0)
            a = jnp.exp(log_a)
            y = jnp.tanh(-log_a) * (1.0 + a * a)
            root = jnp.where(y > 0.0, y * lax.rsqrt(y), 0.0)
            bb = root * ((th[:, 2 * LANES:4 * LANES] + 1.0) * (0.5 * xc_p))
            for jj in range(2):
                j = 2 * p + jj
                a_ref[b, pl.ds(j, tc, stride=nblk), :] = a[:, jj * LANES:(jj + 1) * LANES]
                b_ref[b, pl.ds(j, tc, stride=nblk), :] = bb[:, jj * LANES:(jj + 1) * LANES]

    group = 8
    n_groups = tc // group

    def steps(gidx, hs):
        g = (n_groups - 1 - gidx) if reverse else gidx
        base = pl.multiple_of(g * (group * nblk), group * nblk)
        hs = list(hs)
        for u in range(group):
            row = base + ((group - 1 - u) if reverse else u) * nblk
            for b in range(bsz):
                h = a_ref[b, pl.ds(row, nblk), :] * hs[b] + b_ref[b, pl.ds(row, nblk), :]
                hs_ref[b, pl.ds(row, nblk), :] = h
                hs[b] = h
        return tuple(hs)

    hs = lax.fori_loop(0, n_groups, steps, tuple(hst_ref[b] for b in range(bsz)))
    for b in range(bsz):
        hst_ref[b] = hs[b]
        hfin_ref[b] = hs[b]
    if emit_h:
        for b in range(bsz):
            for j in range(nblk):
                h_ref[b, :, j * LANES:(j + 1) * LANES] = hs_ref[b, pl.ds(j, tc, stride=nblk), :]


def _scan(xin, h0, cw, cb, wg, bg, lam, *, reverse, emit_h):
    bsz, s, w = xin.shape
    tc = min(T_SCAN, s)
    assert s % tc == 0 and w == SUBLANES * LANES and bsz % 2 == 0
    n_chunks = s // tc
    hb = tc // 8

    def cidx(i):
        return (n_chunks - 1 - i) if reverse else i

    tile = lambda: pl.BlockSpec((bsz, tc, w), lambda i: (0, cidx(i), 0))
    state = lambda: pl.BlockSpec((bsz, SUBLANES, LANES), lambda i: (0, 0, 0))
    gate_specs = [pl.BlockSpec(wg.shape, lambda i: (0, 0, 0)),
                  pl.BlockSpec(bg.shape, lambda i: (0, 0, 0)),
                  pl.BlockSpec(lam.shape, lambda i: (0, 0))]
    if reverse:
        args = (xin, h0, wg, bg, lam)
        in_specs = [tile(), state()] + gate_specs
    else:
        args = (xin, xin, xin, h0, cw, cb, wg, bg, lam)
        in_specs = [tile(),
                    pl.BlockSpec((bsz, 8, w), lambda i: (0, jnp.maximum(cidx(i) * hb - 1, 0), 0)),
                    pl.BlockSpec((bsz, 8, w), lambda i: (0, jnp.minimum((cidx(i) + 1) * hb, n_chunks * hb - 1), 0)),
                    state(),
                    pl.BlockSpec(cw.shape, lambda i: (0, 0)),
                    pl.BlockSpec(cb.shape, lambda i: (0, 0))] + gate_specs
    full_seq = jax.ShapeDtypeStruct((bsz, s, w), F32)
    out_specs = [state()]
    out_shape = [jax.ShapeDtypeStruct((bsz, SUBLANES, LANES), F32)]
    if emit_h:
        out_specs = [tile()] + out_specs
        out_shape = [full_seq] + out_shape
    scratch = [pltpu.VMEM((bsz, tc * SUBLANES, LANES), F32),
               pltpu.VMEM((bsz, tc * SUBLANES, LANES), F32),
               pltpu.VMEM((bsz, tc * SUBLANES, LANES), F32),
               pltpu.VMEM((bsz, SUBLANES, LANES), F32)]
    if not reverse:
        out_specs = out_specs + [tile()]
        out_shape = out_shape + [full_seq]
        scratch = [pltpu.VMEM((bsz // 2, w // LANES, 2 * (tc + 16), LANES), F32)] + scratch
    res = pl.pallas_call(
        functools.partial(_scan_kernel, reverse=reverse, emit_h=emit_h, n_chunks=n_chunks),
        grid=(n_chunks,),
        in_specs=in_specs,
        out_specs=out_specs,
        out_shape=out_shape,
        scratch_shapes=scratch,
        compiler_params=_cparams(("arbitrary",)),
        name="scan_bwd" if reverse else "scan_fwd",
    )(*args)
    return tuple(res) if emit_h else (None,) + tuple(res)


CONV_ROWS = 4


def _k1_kernel(x_ref, mod_ref, gmix_ref, win_ref, bin_ref, cw_ref, cb_ref, lng_ref, lnb_ref, wpa_ref,
               ma_ref, gyg_ref, sgb_ref, xr_ref, zp_ref, wb_ref, de_ref, co_ref):
    t, d = x_ref.shape[1], x_ref.shape[2]
    wc = cw_ref.shape[1]
    n_cblk = wc // LANES
    n_quads = t // (CONV_ROWS * GRID_W)
    lo = CONV_PAD
    hi = CONV_PAD + GRID_W
    first = (pl.program_id(0) == 0) & (pl.program_id(1) == 0)

    @pl.when(first)
    def _():
        for k in range(CONV_K):
            for cblk in range(n_cblk):
                row = cw_ref[k:k + 1, cblk * LANES:(cblk + 1) * LANES]
                wb_ref[k, cblk] = jnp.broadcast_to(row, (2 * GRID_W, LANES)).astype(BF16)

    xn = _rms_norm(x_ref[0], gmix_ref[...])
    sh1 = mod_ref[0, :, 0:d]
    sc1 = mod_ref[0, :, d:2 * d]
    hx = (xn * (1.0 + sc1) + sh1).astype(BF16)

    def proj(c0, c1):
        return _bdot(hx, win_ref[:, c0:c1]) + bin_ref[:, c0:c1]

    zglu = proj(0, wc) * _sigmoid(proj(wc, 2 * wc))

    zp_ref[:, :, 0:2 * lo, :] = jnp.zeros((n_quads, n_cblk, 2 * lo, LANES), U32)
    zp_ref[:, :, 2 * hi:2 * (hi + CONV_PAD), :] = jnp.zeros((n_quads, n_cblk, 2 * CONV_PAD, LANES), U32)
    for q in range(n_quads):
        for par in range(2):
            r0 = (CONV_ROWS * q + 2 * par) * GRID_W
            for cblk in range(n_cblk):
                cs = slice(cblk * LANES, (cblk + 1) * LANES)
                zp_ref[q, cblk, pl.ds(2 * lo + par, GRID_W, stride=2), :] = _pack_bf16_pair(
                    zglu[r0:r0 + GRID_W, cs], zglu[r0 + GRID_W:r0 + 2 * GRID_W, cs])

    o = 2 * wc
    gyg_ref[0] = _gelu_tanh(proj(o, o + d)).astype(BF16)
    xr_ref[0] = proj(o + d, o + 2 * d)

    def conv_block(cblk):
        bias = cb_ref[cblk]
        for q in range(n_quads):
            for par in range(2):
                r0 = (CONV_ROWS * q + 2 * par) * GRID_W
                acc = jnp.zeros((2 * GRID_W, LANES), F32)
                for k in range(CONV_K):
                    off = lo - CONV_K // 2 + k
                    words = zp_ref[q, cblk, pl.ds(2 * off + par, GRID_W, stride=2), :]
                    acc = acc + pltpu.bitcast(words, BF16).astype(F32) * wb_ref[k, cblk].astype(F32)
                slot = 2 * q + par
                de_ref[slot] = acc
                co_ref[cblk, r0:r0 + GRID_W, :] = de_ref[slot, pl.ds(0, GRID_W, stride=2), :] + bias
                co_ref[cblk, r0 + GRID_W:r0 + 2 * GRID_W, :] = de_ref[slot, pl.ds(1, GRID_W, stride=2), :] + bias

    def conv_body(cblk, carry):
        conv_block(cblk)
        return carry

    lax.fori_loop(0, n_cblk, conv_body, 0)

    c = jnp.concatenate([co_ref[cblk] for cblk in range(n_cblk)], axis=1)
    mu = jnp.mean(c, axis=-1, keepdims=True)
    cc = c - mu
    y = cc * lax.rsqrt(jnp.mean(cc * cc, axis=-1, keepdims=True) + EPS) * lng_ref[...] + lnb_ref[...]
    act = _silu(y).astype(BF16)
    sga = _sigmoid(proj(o + 2 * d, o + 3 * d))
    sgb_ref[0] = _sigmoid(proj(o + 3 * d, o + 4 * d)).astype(BF16)
    ma_ref[0] = (sga * _bdot(act, wpa_ref[...])).astype(BF16)


def _k1(x, mod, gmix, win, bin_, cw, cb, lng, lnb, wpa):
    bsz, s, d = x.shape
    t = T_TOK
    assert s % t == 0 and t % (CONV_ROWS * GRID_W) == 0
    wc = cw.shape[1]
    n_cblk = wc // LANES
    n_quads = t // (CONV_ROWS * GRID_W)
    cb = cb.reshape(n_cblk, 1, LANES)

    def full(a):
        return pl.BlockSpec(a.shape, lambda b, i: (0,) * a.ndim)

    def const(a):
        return pl.BlockSpec(a.shape, lambda b, i: (0,) * a.ndim, pipeline_mode=pl.Buffered(1))

    tok = lambda: pl.BlockSpec((1, t, d), lambda b, i: (b, i, 0))
    return pl.pallas_call(
        _k1_kernel,
        grid=(bsz, s // t),
        in_specs=[tok(), pl.BlockSpec((1, 1, mod.shape[1]), lambda b, i: (b, 0, 0)), full(gmix), const(win),
                  full(bin_), full(cw), full(cb), full(lng), full(lnb), const(wpa)],
        out_specs=[tok(), tok(), tok(), tok()],
        out_shape=[jax.ShapeDtypeStruct((bsz, s, d), BF16),
                   jax.ShapeDtypeStruct((bsz, s, d), BF16),
                   jax.ShapeDtypeStruct((bsz, s, d), BF16),
                   jax.ShapeDtypeStruct((bsz, s, d), F32)],
        scratch_shapes=[pltpu.VMEM((n_quads, n_cblk, 2 * (GRID_W + 2 * CONV_PAD), LANES), U32),
                        pltpu.VMEM((CONV_K, n_cblk, 2 * GRID_W, LANES), BF16),
                        pltpu.VMEM((2 * n_quads, 2 * GRID_W, LANES), F32),
                        pltpu.VMEM((n_cblk, t, LANES), F32)],
        compiler_params=_cparams(("arbitrary", "arbitrary")),
        name="k1",
    )(x, mod[:, None, :], gmix, win, bin_, cw, cb, lng, lnb, wpa)


def _post_kernel(x_ref, hf_ref, hb_ref, gyg_ref, ma_ref, sgb_ref, mod_ref, wpb_ref, wo_ref, gffn_ref,
                 wrt_ref, brt_ref, x1_ref, hx_ref, rti_ref):
    d = x_ref.shape[2]
    b = pl.program_id(0)
    y_rec = hf_ref[0] + hb_ref[0]
    b_lat = _bdot((gyg_ref[0].astype(F32) * y_rec).astype(BF16), wpb_ref[...])
    merged = ma_ref[0].astype(F32) + sgb_ref[0].astype(F32) * b_lat
    res = _bdot(merged.astype(BF16), wo_ref[...])
    gt1 = mod_ref[pl.ds(b, 1), 2 * d:3 * d]
    x1 = x_ref[0] + gt1 * res
    x1_ref[0] = x1
    sh2 = mod_ref[pl.ds(b, 1), 3 * d:4 * d]
    sc2 = mod_ref[pl.ds(b, 1), 4 * d:5 * d]
    hm = _rms_norm(x1, gffn_ref[...]) * (1.0 + sc2) + sh2
    half = d // 2
    hx_ref[0, :, 0:half] = _pack_bf16_pair(hm[:, 0:half], hm[:, half:d])

    nt = (((1,), (1,)), ((), ()))
    hm_hi = hm.astype(BF16)
    hm_lo = (hm - hm_hi.astype(F32)).astype(BF16)
    lg = (lax.dot_general(wrt_ref[0], hm_hi, nt, preferred_element_type=F32)
          + lax.dot_general(wrt_ref[0], hm_lo, nt, preferred_element_type=F32)
          + lax.dot_general(wrt_ref[1], hm_hi, nt, preferred_element_type=F32)) + brt_ref[...]
    gl = [lg[k:k + 1, :] for k in range(N_GROUPS)]
    gmax = jnp.maximum(jnp.maximum(gl[0], gl[1]), jnp.maximum(gl[2], gl[3]))
    gsel = jnp.where(gl[0] == gmax, 0, jnp.where(gl[1] == gmax, 1, jnp.where(gl[2] == gmax, 2, 3)))
    p_grp = 1.0 / (jnp.exp(gl[0] - gmax) + jnp.exp(gl[1] - gmax) + jnp.exp(gl[2] - gmax) + jnp.exp(gl[3] - gmax))
    el = []
    for j in range(EXPERTS_PER_GROUP):
        rows = [lg[N_GROUPS + g * EXPERTS_PER_GROUP + j:N_GROUPS + g * EXPERTS_PER_GROUP + j + 1, :]
                for g in range(N_GROUPS)]
        el.append(jnp.where(gsel == 0, rows[0], jnp.where(gsel == 1, rows[1], jnp.where(gsel == 2, rows[2], rows[3]))))
    m1 = jnp.maximum(jnp.maximum(el[0], el[1]), jnp.maximum(el[2], el[3]))
    i1 = jnp.where(el[0] == m1, 0, jnp.where(el[1] == m1, 1, jnp.where(el[2] == m1, 2, 3)))
    neg = jnp.float32(-jnp.inf)
    rest = [jnp.where(i1 == j, neg, el[j]) for j in range(EXPERTS_PER_GROUP)]
    m2 = jnp.maximum(jnp.maximum(rest[0], rest[1]), jnp.maximum(rest[2], rest[3]))
    i2 = jnp.where((rest[0] == m2) & (i1 != 0), 0,
                   jnp.where((rest[1] == m2) & (i1 != 1), 1, jnp.where((rest[2] == m2) & (i1 != 2), 2, 3)))
    e21 = jnp.exp(m2 - m1)
    den = 1.0 / (1.0 + e21)
    w1 = p_grp * den
    w2 = p_grp * (e21 * den)
    first_lo = i1 < i2
    lo = jnp.where(first_lo, i1, i2)
    hi = jnp.where(first_lo, i2, i1)
    w_lo = jnp.where(first_lo, w1, w2)
    w_hi = jnp.where(first_lo, w2, w1)
    pair = jnp.where(lo == 0, hi - 1, jnp.where(lo == 1, hi + 1, 5))
    bucket = gsel * N_PAIRS + pair
    t = bucket.shape[1]
    rti_ref[...] = jnp.concatenate([bucket.astype(I32), jnp.zeros((7, t), I32)], axis=0)
    wrows = jnp.concatenate([w_lo, w_hi, jnp.zeros((LANES - 2, t), F32)], axis=0)
    hx_ref[0, :, half:half + LANES] = lax.bitcast_convert_type(wrows.T, U32)


def _post(x, hf, hb, gyg, ma, sgb, mod, wpb, wo, gffn, wrt, brt):
    bsz, s, d = x.shape
    t = brt.shape[1]
    assert s % t == 0
    nt = s // t
    n = bsz * s

    def full(a):
        return pl.BlockSpec(a.shape, lambda b, i: (0,) * a.ndim)

    def const(a):
        return pl.BlockSpec(a.shape, lambda b, i: (0,) * a.ndim, pipeline_mode=pl.Buffered(1))

    tok = lambda: pl.BlockSpec((1, t, d), lambda b, i: (b, i, 0))
    rt = lambda: pl.BlockSpec((8, t), lambda b, i: (0, b * nt + i))
    return pl.pallas_call(
        _post_kernel,
        grid=(bsz, nt),
        in_specs=[tok(), tok(), tok(), tok(), tok(), tok(), full(mod), const(wpb), const(wo), full(gffn), full(wrt),
                  full(brt)],
        out_specs=[tok(), pl.BlockSpec((1, t, d // 2 + LANES), lambda b, i: (b, i, 0)), rt()],
        out_shape=[jax.ShapeDtypeStruct((bsz, s, d), F32),
                   jax.ShapeDtypeStruct((bsz, s, d // 2 + LANES), U32),
                   jax.ShapeDtypeStruct((8, n), I32)],
        compiler_params=_cparams(("arbitrary", "arbitrary")),
        name="post",
    )(x, hf, hb, gyg, ma, sgb, mod, wpb, wo, gffn, wrt, brt)


def _sort_kernel(bk_ref, slot_ref, meta_ref, *, tm):
    bk = bk_ref[...]
    r = bk.shape[0]
    nt_lanes = meta_ref.shape[1]
    shift = tm.bit_length() - 1
    upper = (lax.broadcasted_iota(I32, (LANES, LANES), 0) < lax.broadcasted_iota(I32, (LANES, LANES), 1))
    upper = upper.astype(F32).astype(BF16)
    lower = (lax.broadcasted_iota(I32, (r, r), 1) < lax.broadcasted_iota(I32, (r, r), 0))
    lower = lower.astype(F32).astype(BF16)
    ones = jnp.ones((LANES, LANES), BF16)
    tile_pos = lax.broadcasted_iota(I32, (1, nt_lanes), 1) * tm
    start = jnp.zeros((1, LANES), I32)
    start_t = jnp.zeros((1, nt_lanes), I32)
    slot = jnp.zeros((r, LANES), I32)
    tile_bucket = jnp.zeros((1, nt_lanes), I32)
    for k in range(N_BUCKETS):
        m = bk == k
        mb = m.astype(F32).astype(BF16)
        pre = _bdot(mb, upper)
        tot = _bdot(mb, ones)
        rowpre = _bdot(lower, tot.astype(BF16))
        rank = (pre + rowpre).astype(I32)
        cnt = (rowpre[r - 1:r, :] + tot[r - 1:r, :]).astype(I32)
        slot = jnp.where(m, start + rank, slot)
        padded = ((cnt + (tm - 1)) >> shift) << shift
        start = start + padded
        start_t = start_t + jnp.concatenate([padded] * (nt_lanes // LANES), axis=1)
        tile_bucket = tile_bucket + (tile_pos >= start_t).astype(I32)
    slot_ref[...] = slot
    valid = (tile_pos < start_t).astype(I32)
    tb = jnp.minimum(tile_bucket, N_BUCKETS - 1)
    g = (tb * 43) >> 8
    pair = tb - N_PAIRS * g
    ge3 = (pair >= 3).astype(I32)
    ge5 = (pair >= 5).astype(I32)
    lo = ge3 + ge5
    hi = pair + 1 - 2 * ge3 - ge5
    e_lo = g * EXPERTS_PER_GROUP + lo
    e_hi = g * EXPERTS_PER_GROUP + hi
    meta_ref[...] = jnp.concatenate([e_lo, e_hi, valid, jnp.zeros((5, nt_lanes), I32)], axis=0)


def _sort(bucket2d, tm, nt_lanes):
    r = bucket2d.shape[0]
    return pl.pallas_call(
        functools.partial(_sort_kernel, tm=tm),
        grid=(1,),
        in_specs=[pl.BlockSpec((r, LANES), lambda i: (0, 0))],
        out_specs=[pl.BlockSpec((r, LANES), lambda i: (0, 0)),
                   pl.BlockSpec((8, nt_lanes), lambda i: (0, 0))],
        out_shape=[jax.ShapeDtypeStruct((r, LANES), I32),
                   jax.ShapeDtypeStruct((8, nt_lanes), I32)],
        compiler_params=_cparams(("arbitrary",)),
        name="bucket_sort",
    )(bucket2d)


def _moe_kernel(elo_ref, ehi_ref, valid_ref, h_ref, wg_lo, wg_hi, wu_lo, wu_hi, wd_lo, wd_hi, o_ref):
    i = pl.program_id(0)
    half = o_ref.shape[1]

    @pl.when(valid_ref[i] != 0)
    def _():
        h_a, h_b = _unpack_bf16_pair(h_ref[:, 0:half])
        h = jnp.concatenate([h_a, h_b], axis=1).astype(BF16)
        w_lo = lax.bitcast_convert_type(h_ref[:, half:half + 1], F32)
        w_hi = lax.bitcast_convert_type(h_ref[:, half + 1:half + 2], F32)
        act_lo = _silu(_bdot(h, wg_lo[0])) * _bdot(h, wu_lo[0]) * w_lo
        act_hi = _silu(_bdot(h, wg_hi[0])) * _bdot(h, wu_hi[0]) * w_hi
        o = _bdot(act_lo.astype(BF16), wd_lo[0]) + _bdot(act_hi.astype(BF16), wd_hi[0])
        o_ref[...] = _pack_bf16_pair(o[:, 0:half], o[:, half:2 * half])

    @pl.when(valid_ref[i] == 0)
    def _():
        o_ref[...] = jnp.zeros(o_ref.shape, U32)


def _moe(e_lo, e_hi, valid, hs, wgate, wup, wdown):
    npad, wrow = hs.shape
    tm = T_MOE
    n_tiles = npad // tm
    ne, d, f = wgate.shape
    lo3 = lambda i, elo, ehi, va: (elo[i], 0, 0)
    hi3 = lambda i, elo, ehi, va: (ehi[i], 0, 0)
    row = lambda i, elo, ehi, va: (i, 0)
    grid_spec = pltpu.PrefetchScalarGridSpec(
        num_scalar_prefetch=3,
        grid=(n_tiles,),
        in_specs=[pl.BlockSpec((tm, wrow), row),
                  pl.BlockSpec((1, d, f), lo3), pl.BlockSpec((1, d, f), hi3),
                  pl.BlockSpec((1, d, f), lo3), pl.BlockSpec((1, d, f), hi3),
                  pl.BlockSpec((1, f, d), lo3), pl.BlockSpec((1, f, d), hi3)],
        out_specs=pl.BlockSpec((tm, d // 2), row),
    )
    return pl.pallas_call(
        _moe_kernel,
        grid_spec=grid_spec,
        out_shape=jax.ShapeDtypeStruct((npad, d // 2), U32),
        compiler_params=_cparams(("arbitrary",)),
        name="moe",
    )(e_lo, e_hi, valid, hs, wgate, wgate, wup, wup, wdown, wdown)


SC_CORES = 2
SC_SUBCORES = 16
SC_LANES = 16
SC_WORKERS = SC_CORES * SC_SUBCORES
SC_ROWS = 64


def _sc_worker_id():
    return lax.axis_index("s") * SC_CORES + lax.axis_index("c")


def _sc_chunk_rows(rows):
    return next(c for c in (SC_ROWS, SC_ROWS // 2, SC_ROWS // 4) if rows % c == 0)


def _sc_gather_rows(src_hbm, idx_v, out_hbm, out_base, n_chunks, bufs, sems):
    chunk = bufs[0].shape[0]

    def gather(c, b):
        off = pl.multiple_of(c * chunk, chunk)
        return pltpu.make_async_copy(src_hbm.at[idx_v.at[pl.ds(off, chunk)]], bufs[b], sems[b])

    def finish(c, b):
        gather(c, b).wait()
        pltpu.sync_copy(bufs[b], out_hbm.at[pl.ds(out_base + pl.multiple_of(c * chunk, chunk), chunk)])

    gather(0, 0).start()

    @pl.loop(0, n_chunks // 2)
    def _(p):
        c = 2 * p
        gather(c + 1, 1).start()
        finish(c, 0)

        @pl.when(c + 2 < n_chunks)
        def _():
            gather(c + 2, 0).start()

        finish(c + 1, 1)

    if n_chunks % 2:
        finish(n_chunks - 1, 0)


def _sc_dispatch(hx, slot, npad):
    n, wrow = hx.shape
    assert npad % (SC_WORKERS * SC_LANES) == 0 and n % SC_LANES == 0
    rows = npad // SC_WORKERS
    chunk = _sc_chunk_rows(rows)
    mesh = plsc.VectorSubcoreMesh(core_axis_name="c", subcore_axis_name="s")

    @functools.partial(
        pl.kernel, mesh=mesh,
        out_type=jax.ShapeDtypeStruct((npad, wrow), hx.dtype),
        scratch_types=[pltpu.VMEM((n,), I32),
                       pltpu.VMEM((rows,), I32),
                       pltpu.VMEM((chunk, wrow), hx.dtype),
                       pltpu.VMEM((chunk, wrow), hx.dtype),
                       pltpu.SemaphoreType.DMA,
                       pltpu.SemaphoreType.DMA],
        compiler_params=pltpu.CompilerParams(needs_layout_passes=False),
        name="sc_dispatch",
    )
    def k(hx_hbm, slot_hbm, out_hbm, slot_v, tok_v, rows_a, rows_b, sem_a, sem_b):
        base = _sc_worker_id() * rows
        pltpu.sync_copy(slot_hbm, slot_v)

        lane = lax.iota(I32, SC_LANES)

        @pl.loop(0, rows // SC_LANES)
        def _(i):
            filler = base + i * SC_LANES + lane
            tok_v[pl.ds(i * SC_LANES, SC_LANES)] = lax.rem(filler, jnp.int32(n))

        @plsc.parallel_loop(0, n // SC_LANES, unroll=8)
        def _(i):
            loc = slot_v[pl.ds(i * SC_LANES, SC_LANES)] - base
            mine = (loc >= 0) & (loc < rows)
            plsc.store_scatter(tok_v, [loc], i * SC_LANES + lane, mask=mine)

        _sc_gather_rows(hx_hbm, tok_v, out_hbm, base, rows // chunk, (rows_a, rows_b), (sem_a, sem_b))

    return k(hx, slot)


def _sc_combine(mo_sorted, slot):
    n = slot.shape[0]
    wrow = mo_sorted.shape[1]
    assert n % (SC_WORKERS * SC_LANES) == 0
    toks = n // SC_WORKERS
    chunk = _sc_chunk_rows(toks)
    mesh = plsc.VectorSubcoreMesh(core_axis_name="c", subcore_axis_name="s")

    @functools.partial(
        pl.kernel, mesh=mesh,
        out_type=jax.ShapeDtypeStruct((n, wrow), mo_sorted.dtype),
        scratch_types=[pltpu.VMEM((toks,), I32),
                       pltpu.VMEM((chunk, wrow), mo_sorted.dtype),
                       pltpu.VMEM((chunk, wrow), mo_sorted.dtype),
                       pltpu.SemaphoreType.DMA,
                       pltpu.SemaphoreType.DMA],
        compiler_params=pltpu.CompilerParams(needs_layout_passes=False),
        name="sc_combine",
    )
    def k(mo_hbm, slot_hbm, out_hbm, idx_v, rows_a, rows_b, sem_a, sem_b):
        base = _sc_worker_id() * toks
        pltpu.sync_copy(slot_hbm.at[pl.ds(base, toks)], idx_v)
        _sc_gather_rows(mo_hbm, idx_v, out_hbm, base, toks // chunk, (rows_a, rows_b), (sem_a, sem_b))

    return k(mo_sorted, slot)


def _final_kernel(x1_ref, mo_ref, mod_ref, g_ref, *rest, b0):
    o_ref = rest[-1]
    d = x1_ref.shape[2]
    b = pl.program_id(0) + b0
    gt2 = mod_ref[pl.ds(b, 1), 5 * d:6 * d]
    mo = jnp.concatenate(_unpack_bf16_pair(mo_ref[0]), axis=1)
    o_ref[0] = _rms_norm(x1_ref[0] + gt2 * mo, g_ref[...])


def _final(x1, mo, mod, g, *, b0, prev):
    bsz, s, d = x1.shape
    nb = mo.shape[0]
    t = min(T_FINAL, s)
    assert s % t == 0
    tok = lambda: pl.BlockSpec((1, t, d), lambda b, i: (b + b0, i, 0))
    in_specs = [tok(), pl.BlockSpec((1, t, d // 2), lambda b, i: (b, i, 0)),
                pl.BlockSpec(mod.shape, lambda b, i: (0, 0)), pl.BlockSpec((1, d), lambda b, i: (0, 0))]
    args = [x1, mo, mod, g]
    aliases = {}
    if prev is not None:
        in_specs.append(pl.BlockSpec(memory_space=pl.ANY))
        aliases = {len(args): 0}
        args.append(prev)
    return pl.pallas_call(
        functools.partial(_final_kernel, b0=b0),
        grid=(nb, s // t),
        in_specs=in_specs,
        out_specs=tok(),
        out_shape=jax.ShapeDtypeStruct((bsz, s, d), F32),
        input_output_aliases=aliases,
        compiler_params=_cparams(("arbitrary", "arbitrary")),
        name="final",
    )(*args)


def _pair_blockdiag(w):
    h, hd, _ = w.shape
    w2 = w.reshape(h // 2, 2, hd, hd)
    z = jnp.zeros((h // 2, hd, hd), w.dtype)
    top = jnp.concatenate([w2[:, 0], z], axis=2)
    bot = jnp.concatenate([z, w2[:, 1]], axis=2)
    return jnp.concatenate([top, bot], axis=1)


def _gate_params(w_r, b_r, w_i, b_i):
    wg = (0.5 * jnp.concatenate([_pair_blockdiag(w_r), _pair_blockdiag(w_i)], axis=2)).astype(BF16)
    h, hd = b_r.shape
    bg = 0.5 * jnp.concatenate([b_r.reshape(h // 2, 1, 2 * hd), b_i.reshape(h // 2, 1, 2 * hd)], axis=2)
    return wg, bg


def kernel(x, c, ctx, c_ctx, w_ada, b_ada, g_mix, w_in, b_in, conv_w, conv_b, ln_g, ln_b, w_pa, lru_conv_w, lru_conv_b, w_r_f, b_r_f, w_i_f, b_i_f, lam_f, w_r_b, b_r_b, w_i_b, b_i_b, lam_b, w_pb, w_o, g_ffn, w_grp, b_grp, w_er, b_er, w_gate, w_up, w_down, g_final):
    bsz, s, d = x.shape
    depth = w_ada.shape[0]
    assert depth == 1 and bsz + 1 <= 8
    l = 0
    n = bsz * s
    wc = conv_w.shape[2]
    wl = lru_conv_w.shape[2]
    rec0 = 2 * wc + wl
    ctx_row = bsz

    c8 = jnp.concatenate([c, c_ctx[None, :], jnp.zeros((8 - bsz - 1, d), F32)], axis=0)
    mod = _ada(c8, w_ada[l], b_ada[l][None, :])

    win = _to_bf16(w_in[l])
    bin_ = b_in[l][None, :]
    gmix = g_mix[l][None, :]
    lcw = lru_conv_w[l]
    lcb = lru_conv_b[l][None, :]
    wg_f, bg_f = _gate_params(w_r_f[l], b_r_f[l], w_i_f[l], b_i_f[l])
    wg_b, bg_b = _gate_params(w_r_b[l], b_r_b[l], w_i_b[l], b_i_b[l])
    lamf = lam_f[l][None, :]
    lamb = lam_b[l][None, :]

    xr_c = _ctxproj(ctx, mod, gmix, win[:, rec0:rec0 + wl], bin_[:, rec0:rec0 + wl], ctx_row)
    zero_state = jnp.zeros((bsz, SUBLANES, LANES), F32)
    _, h0f, xc_c = _scan(xr_c, zero_state, lcw, lcb, wg_f, bg_f, lamf, reverse=False, emit_h=False)
    _, h0b = _scan(xc_c, zero_state, lcw, lcb, wg_b, bg_b, lamb, reverse=True, emit_h=False)

    ma, gyg, sgb, xr = _k1(x, mod, gmix, win, bin_, conv_w[l], conv_b[l][None, :], ln_g[l][None, :],
                           ln_b[l][None, :], w_pa[l].astype(BF16))
    hf, _, xc = _scan(xr, h0f, lcw, lcb, wg_f, bg_f, lamf, reverse=False, emit_h=True)
    hb, _ = _scan(xc, h0b, lcw, lcb, wg_b, bg_b, lamb, reverse=True, emit_h=True)

    n_rt = 32
    wrt = jnp.concatenate([w_grp[l].T, w_er[l].T, jnp.zeros((n_rt - N_GROUPS * (1 + EXPERTS_PER_GROUP), d), F32)], axis=0)
    brt = jnp.concatenate([b_grp[l], b_er[l], jnp.zeros((n_rt - N_GROUPS * (1 + EXPERTS_PER_GROUP),), F32)])
    brt = jnp.broadcast_to(brt[:, None], (n_rt, min(T_POST, s)))
    wrt_hi = wrt.astype(BF16)
    wrt = jnp.stack([wrt_hi, (wrt - wrt_hi.astype(F32)).astype(BF16)])
    x1, hx, rti = _post(x, hf, hb, gyg, ma, sgb, mod, w_pb[l].astype(BF16), w_o[l].astype(BF16),
                        g_ffn[l][None, :], wrt, brt)

    tm = T_MOE
    n_tiles = n // tm + N_BUCKETS
    npad = n_tiles * tm
    nt_lanes = ((n_tiles + LANES - 1) // LANES) * LANES
    slot2d, meta = _sort(rti[0].reshape(n // LANES, LANES), tm, nt_lanes)
    slot = slot2d.reshape(n)
    hs = _sc_dispatch(hx.reshape(n, hx.shape[2]), slot, npad)
    mo_sorted = _moe(meta[0, :n_tiles], meta[1, :n_tiles], meta[2, :n_tiles], hs,
                     _to_bf16(w_gate[l]), _to_bf16(w_up[l]), _to_bf16(w_down[l]))
    assert bsz % FINAL_PARTS == 0
    nb = bsz // FINAL_PARTS
    out = None
    for p in range(FINAL_PARTS):
        mo = _sc_combine(mo_sorted, slot[p * nb * s:(p + 1) * nb * s]).reshape(nb, s, d // 2)
        out = _final(x1, mo, mod, g_final[None, :], b0=p * nb, prev=out)
    return out
```

```python
import functools

import jax
import jax.numpy as jnp
from jax import lax
from jax.experimental import pallas as pl
from jax.experimental.pallas import tpu as pltpu
from jax.experimental.pallas import tpu_sc as plsc

F32 = jnp.float32
BF16 = jnp.bfloat16
I32 = jnp.int32
U32 = jnp.uint32
HIGHEST = lax.Precision.HIGHEST

EPS = 1e-6
GRID_W = 64
CONV_K = 31
CONV_PAD = 16
LRU_CONV_K = 4
LRU_C = 8.0
LRU_HEADS = 8
N_GROUPS = 4
EXPERTS_PER_GROUP = 4
N_PAIRS = 6
N_BUCKETS = N_GROUPS * N_PAIRS
LANES = 128
SUBLANES = 8
VMEM_LIMIT = 56 * 1024 * 1024

T_TOK = 512
T_POST = 512
T_FINAL = 1024
T_SCAN = 256
T_MOE = 256
FINAL_PARTS = 2
CAST_BLOCK_BYTES = 8 * 1024 * 1024


def _sigmoid(x):
    return 0.5 * (jnp.tanh(0.5 * x) + 1.0)


def _silu(x):
    return x * _sigmoid(x)


def _gelu_tanh(x):
    return 0.5 * x * (1.0 + jnp.tanh(0.7978845608028654 * (x + 0.044715 * (x * x * x))))


def _rms_norm(x, g):
    return x * lax.rsqrt(jnp.mean(x * x, axis=-1, keepdims=True) + EPS) * g


def _bdot(a, b):
    return jnp.dot(a, b, preferred_element_type=F32)


def _pack_bf16_pair(lo, hi):
    lo_bits = lax.bitcast_convert_type(lo.astype(BF16).astype(F32), U32)
    hi_bits = lax.bitcast_convert_type(hi.astype(BF16).astype(F32), U32)
    return (hi_bits & jnp.uint32(0xFFFF0000)) | (lo_bits >> 16)


def _unpack_bf16_pair(u):
    lo = lax.bitcast_convert_type(u << 16, F32)
    hi = lax.bitcast_convert_type(u & jnp.uint32(0xFFFF0000), F32)
    return lo, hi


def _cparams(sem):
    return pltpu.CompilerParams(dimension_semantics=sem, vmem_limit_bytes=VMEM_LIMIT)


def _cast_kernel(x_ref, o_ref):
    o_ref[...] = x_ref[...].astype(o_ref.dtype)


def _to_bf16(w):
    shape = w.shape
    w2 = w.reshape(-1, shape[-1])
    rows, cols = w2.shape
    blk = rows
    while blk % 2 == 0 and blk > 16 and blk * cols * 4 > CAST_BLOCK_BYTES:
        blk //= 2
    out = pl.pallas_call(
        _cast_kernel,
        grid=(rows // blk,),
        in_specs=[pl.BlockSpec((blk, cols), lambda i: (i, 0))],
        out_specs=pl.BlockSpec((blk, cols), lambda i: (i, 0)),
        out_shape=jax.ShapeDtypeStruct((rows, cols), BF16),
        compiler_params=_cparams(("arbitrary",)),
        name="to_bf16",
    )(w2)
    return out.reshape(shape)


def _ada_kernel(c_ref, w_ref, b_ref, o_ref):
    s = _silu(c_ref[...])
    o_ref[...] = jnp.dot(s, w_ref[...], precision=HIGHEST, preferred_element_type=F32) + b_ref[...]


def _ada(c8, w, b):
    d, n = w.shape
    blk = 1024
    return pl.pallas_call(
        _ada_kernel,
        grid=(n // blk,),
        in_specs=[pl.BlockSpec((8, d), lambda j: (0, 0)),
                  pl.BlockSpec((d, blk), lambda j: (0, j)),
                  pl.BlockSpec((1, blk), lambda j: (0, j))],
        out_specs=pl.BlockSpec((8, blk), lambda j: (0, j)),
        out_shape=jax.ShapeDtypeStruct((8, n), F32),
        compiler_params=_cparams(("arbitrary",)),
        name="ada",
    )(c8, w, b)


def _ctxproj_kernel(x_ref, mod_ref, g_ref, w_ref, b_ref, o_ref, *, row):
    d = x_ref.shape[-1]
    xn = _rms_norm(x_ref[0], g_ref[...])
    sh = mod_ref[row:row + 1, 0:d]
    sc = mod_ref[row:row + 1, d:2 * d]
    hx = (xn * (1.0 + sc) + sh).astype(BF16)
    o_ref[0] = _bdot(hx, w_ref[...]) + b_ref[...]


def _ctxproj(ctx, mod, g, w, b, row):
    bsz, n, d = ctx.shape
    return pl.pallas_call(
        functools.partial(_ctxproj_kernel, row=row),
        grid=(bsz,),
        in_specs=[pl.BlockSpec((1, n, d), lambda i: (i, 0, 0)),
                  pl.BlockSpec(mod.shape, lambda i: (0, 0)),
                  pl.BlockSpec((1, d), lambda i: (0, 0)),
                  pl.BlockSpec(w.shape, lambda i: (0, 0)),
                  pl.BlockSpec((1, w.shape[1]), lambda i: (0, 0))],
        out_specs=pl.BlockSpec((1, n, w.shape[1]), lambda i: (i, 0, 0)),
        out_shape=jax.ShapeDtypeStruct((bsz, n, w.shape[1]), F32),
        compiler_params=_cparams(("arbitrary",)),
        name="ctxproj",
    )(ctx, mod, g, w, b)


def _scan_kernel(*refs, reverse, emit_h, n_chunks):
    refs = list(refs)
    if reverse:
        xc_ref, h0_ref, wg_ref, bg_ref, lam_ref = refs[:5]
        rest = refs[5:]
        bsz, tc, w = xc_ref.shape
    else:
        xr_ref, prev_ref, next_ref, h0_ref, cw_ref, cb_ref, wg_ref, bg_ref, lam_ref = refs[:9]
        rest = refs[9:]
        bsz, tc, w = xr_ref.shape
    h_ref = rest.pop(0) if emit_h else None
    hfin_ref = rest.pop(0)
    xc_out_ref = None if reverse else rest.pop(0)
    xp_ref = None if reverse else rest.pop(0)
    a_ref, b_ref, hs_ref, hst_ref = rest
    nblk = w // LANES
    i = pl.program_id(0)
    chunk = (n_chunks - 1 - i) if reverse else i

    @pl.when(i == 0)
    def _():
        hst_ref[...] = h0_ref[...]

    z = -lam_ref[...]
    softplus = jnp.maximum(z, 0.0) + jnp.log1p(jnp.exp(-jnp.abs(z)))
    nlsh = (-0.5 * LRU_C) * softplus

    if not reverse:
        has_prev = chunk > 0
        has_next = chunk < n_chunks - 1
        for b in range(bsz):
            bp, par = divmod(b, 2)
            for cblk in range(nblk):
                cs = slice(cblk * LANES, (cblk + 1) * LANES)
                xp_ref[bp, cblk, pl.ds(par, 8, stride=2), :] = jnp.where(has_prev, prev_ref[b, :, cs], 0.0)
                xp_ref[bp, cblk, pl.ds(16 + par, tc, stride=2), :] = xr_ref[b, :, cs]
                xp_ref[bp, cblk, pl.ds(16 + 2 * tc + par, 8, stride=2), :] = (
                    jnp.where(has_next, next_ref[b, :, cs], 0.0))

    for b in range(bsz):
        bp, par = divmod(b, 2)
        xc = []
        for cblk in range(nblk):
            cs = slice(cblk * LANES, (cblk + 1) * LANES)
            if reverse:
                xc.append(xc_ref[b, :, cs])
                continue
            acc = cb_ref[:, cs] + cw_ref[0:1, cs] * xp_ref[bp, cblk, pl.ds(2 * 6 + par, tc, stride=2), :]
            for k in range(1, LRU_CONV_K):
                acc = acc + cw_ref[k:k + 1, cs] * xp_ref[bp, cblk, pl.ds(2 * (6 + k) + par, tc, stride=2), :]
            xc_out_ref[b, :, cs] = acc
            xc.append(acc)
        for p in range(LRU_HEADS // 2):
            cs = slice(2 * p * LANES, 2 * (p + 1) * LANES)
            xc_p = jnp.concatenate([xc[2 * p], xc[2 * p + 1]], axis=1)
            th = jnp.tanh(_bdot(xc_p.astype(BF16), wg_ref[p]) + bg_ref[p])
            log_a = nlsh[:, cs] * (th[:, 0:2 * LANES] + 1.0)
            a = jnp.exp(log_a)
            y = jnp.tanh(log_a) * (-1.0 - a * a)
            root = jnp.where(y > 0.0, y * lax.rsqrt(y), 0.0)
            bb = root * ((th[:, 2 * LANES:4 * LANES] + 1.0) * xc_p)
            for jj in range(2):
                j = 2 * p + jj
                a_ref[b, pl.ds(j, tc, stride=nblk), :] = a[:, jj * LANES:(jj + 1) * LANES]
                b_ref[b, pl.ds(j, tc, stride=nblk), :] = bb[:, jj * LANES:(jj + 1) * LANES]

    group = 8
    n_groups = tc // group

    def steps(gidx, hs):
        g = (n_groups - 1 - gidx) if reverse else gidx
        base = pl.multiple_of(g * (group * nblk), group * nblk)
        hs = list(hs)
        for u in range(group):
            row = base + ((group - 1 - u) if reverse else u) * nblk
            for b in range(bsz):
                h = a_ref[b, pl.ds(row, nblk), :] * hs[b] + b_ref[b, pl.ds(row, nblk), :]
                hs_ref[b, pl.ds(row, nblk), :] = h
                hs[b] = h
        return tuple(hs)

    hs = lax.fori_loop(0, n_groups, steps, tuple(hst_ref[b] for b in range(bsz)))
    for b in range(bsz):
        hst_ref[b] = hs[b]
        hfin_ref[b] = hs[b]
    if emit_h:
        for b in range(bsz):
            for j in range(nblk):
                h_ref[b, :, j * LANES:(j + 1) * LANES] = hs_ref[b, pl.ds(j, tc, stride=nblk), :]


def _scan(xin, h0, cw, cb, wg, bg, lam, *, reverse, emit_h):
    bsz, s, w = xin.shape
    tc = min(T_SCAN, s)
    assert s % tc == 0 and w == SUBLANES * LANES and bsz % 2 == 0
    n_chunks = s // tc
    hb = tc // 8

    def cidx(i):
        return (n_chunks - 1 - i) if reverse else i

    tile = lambda: pl.BlockSpec((bsz, tc, w), lambda i: (0, cidx(i), 0))
    state = lambda: pl.BlockSpec((bsz, SUBLANES, LANES), lambda i: (0, 0, 0))
    gate_specs = [pl.BlockSpec(wg.shape, lambda i: (0, 0, 0)),
                  pl.BlockSpec(bg.shape, lambda i: (0, 0, 0)),
                  pl.BlockSpec(lam.shape, lambda i: (0, 0))]
    if reverse:
        args = (xin, h0, wg, bg, lam)
        in_specs = [tile(), state()] + gate_specs
    else:
        args = (xin, xin, xin, h0, cw, cb, wg, bg, lam)
        in_specs = [tile(),
                    pl.BlockSpec((bsz, 8, w), lambda i: (0, jnp.maximum(cidx(i) * hb - 1, 0), 0)),
                    pl.BlockSpec((bsz, 8, w), lambda i: (0, jnp.minimum((cidx(i) + 1) * hb, n_chunks * hb - 1), 0)),
                    state(),
                    pl.BlockSpec(cw.shape, lambda i: (0, 0)),
                    pl.BlockSpec(cb.shape, lambda i: (0, 0))] + gate_specs
    full_seq = jax.ShapeDtypeStruct((bsz, s, w), F32)
    out_specs = [state()]
    out_shape = [jax.ShapeDtypeStruct((bsz, SUBLANES, LANES), F32)]
    if emit_h:
        out_specs = [tile()] + out_specs
        out_shape = [full_seq] + out_shape
    scratch = [pltpu.VMEM((bsz, tc * SUBLANES, LANES), F32),
               pltpu.VMEM((bsz, tc * SUBLANES, LANES), F32),
               pltpu.VMEM((bsz, tc * SUBLANES, LANES), F32),
               pltpu.VMEM((bsz, SUBLANES, LANES), F32)]
    if not reverse:
        out_specs = out_specs + [tile()]
        out_shape = out_shape + [full_seq]
        scratch = [pltpu.VMEM((bsz // 2, w // LANES, 2 * (tc + 16), LANES), F32)] + scratch
    res = pl.pallas_call(
        functools.partial(_scan_kernel, reverse=reverse, emit_h=emit_h, n_chunks=n_chunks),
        grid=(n_chunks,),
        in_specs=in_specs,
        out_specs=out_specs,
        out_shape=out_shape,
        scratch_shapes=scratch,
        compiler_params=_cparams(("arbitrary",)),
        name="scan_bwd" if reverse else "scan_fwd",
    )(*args)
    return tuple(res) if emit_h else (None,) + tuple(res)


CONV_ROWS = 4


def _k1_kernel(x_ref, mod_ref, gmix_ref, win_ref, bin_ref, cw_ref, cb_ref, lng_ref, lnb_ref, wpa_ref,
               ma_ref, gyg_ref, sgb_ref, xr_ref, zp_ref, wb_ref, de_ref, co_ref):
    t, d = x_ref.shape[1], x_ref.shape[2]
    wc = cw_ref.shape[1]
    n_cblk = wc // LANES
    n_quads = t // (CONV_ROWS * GRID_W)
    lo = CONV_PAD
    hi = CONV_PAD + GRID_W
    first = (pl.program_id(0) == 0) & (pl.program_id(1) == 0)

    @pl.when(first)
    def _():
        for k in range(CONV_K):
            for cblk in range(n_cblk):
                row = cw_ref[k:k + 1, cblk * LANES:(cblk + 1) * LANES]
                wb_ref[k, cblk] = jnp.broadcast_to(row, (2 * GRID_W, LANES)).astype(BF16)

    xn = _rms_norm(x_ref[0], gmix_ref[...])
    sh1 = mod_ref[0, :, 0:d]
    sc1 = mod_ref[0, :, d:2 * d]
    hx = (xn * (1.0 + sc1) + sh1).astype(BF16)

    def proj(c0, c1):
        return _bdot(hx, win_ref[:, c0:c1]) + bin_ref[:, c0:c1]

    zglu = proj(0, wc) * _sigmoid(proj(wc, 2 * wc))

    zp_ref[:, :, 0:2 * lo, :] = jnp.zeros((n_quads, n_cblk, 2 * lo, LANES), U32)
    zp_ref[:, :, 2 * hi:2 * (hi + CONV_PAD), :] = jnp.zeros((n_quads, n_cblk, 2 * CONV_PAD, LANES), U32)
    for q in range(n_quads):
        for par in range(2):
            r0 = (CONV_ROWS * q + 2 * par) * GRID_W
            for cblk in range(n_cblk):
                cs = slice(cblk * LANES, (cblk + 1) * LANES)
                zp_ref[q, cblk, pl.ds(2 * lo + par, GRID_W, stride=2), :] = _pack_bf16_pair(
                    zglu[r0:r0 + GRID_W, cs], zglu[r0 + GRID_W:r0 + 2 * GRID_W, cs])

    o = 2 * wc
    gyg_ref[0] = _gelu_tanh(proj(o, o + d)).astype(BF16)
    xr_ref[0] = proj(o + d, o + 2 * d)

    def conv_block(cblk):
        bias = cb_ref[cblk]
        for q in range(n_quads):
            for par in range(2):
                r0 = (CONV_ROWS * q + 2 * par) * GRID_W
                acc = jnp.zeros((2 * GRID_W, LANES), F32)
                for k in range(CONV_K):
                    off = lo - CONV_K // 2 + k
                    words = zp_ref[q, cblk, pl.ds(2 * off + par, GRID_W, stride=2), :]
                    acc = acc + pltpu.bitcast(words, BF16).astype(F32) * wb_ref[k, cblk].astype(F32)
                slot = 2 * q + par
                de_ref[slot] = acc
                co_ref[cblk, r0:r0 + GRID_W, :] = de_ref[slot, pl.ds(0, GRID_W, stride=2), :] + bias
                co_ref[cblk, r0 + GRID_W:r0 + 2 * GRID_W, :] = de_ref[slot, pl.ds(1, GRID_W, stride=2), :] + bias

    def conv_body(cblk, carry):
        conv_block(cblk)
        return carry

    lax.fori_loop(0, n_cblk, conv_body, 0)

    c = jnp.concatenate([co_ref[cblk] for cblk in range(n_cblk)], axis=1)
    mu = jnp.mean(c, axis=-1, keepdims=True)
    cc = c - mu
    y = cc * lax.rsqrt(jnp.mean(cc * cc, axis=-1, keepdims=True) + EPS) * lng_ref[...] + lnb_ref[...]
    act = _silu(y).astype(BF16)
    sga = _sigmoid(proj(o + 2 * d, o + 3 * d))
    sgb_ref[0] = _sigmoid(proj(o + 3 * d, o + 4 * d)).astype(BF16)
    ma_ref[0] = (sga * _bdot(act, wpa_ref[...])).astype(BF16)


def _k1(x, mod, gmix, win, bin_, cw, cb, lng, lnb, wpa):
    bsz, s, d = x.shape
    t = T_TOK
    assert s % t == 0 and t % (CONV_ROWS * GRID_W) == 0
    wc = cw.shape[1]
    n_cblk = wc // LANES
    n_quads = t // (CONV_ROWS * GRID_W)
    cb = cb.reshape(n_cblk, 1, LANES)

    def full(a):
        return pl.BlockSpec(a.shape, lambda b, i: (0,) * a.ndim)

    def const(a):
        return pl.BlockSpec(a.shape, lambda b, i: (0,) * a.ndim, pipeline_mode=pl.Buffered(1))

    tok = lambda: pl.BlockSpec((1, t, d), lambda b, i: (b, i, 0))
    return pl.pallas_call(
        _k1_kernel,
        grid=(bsz, s // t),
        in_specs=[tok(), pl.BlockSpec((1, 1, mod.shape[1]), lambda b, i: (b, 0, 0)), full(gmix), const(win),
                  full(bin_), full(cw), full(cb), full(lng), full(lnb), const(wpa)],
        out_specs=[tok(), tok(), tok(), tok()],
        out_shape=[jax.ShapeDtypeStruct((bsz, s, d), BF16),
                   jax.ShapeDtypeStruct((bsz, s, d), BF16),
                   jax.ShapeDtypeStruct((bsz, s, d), BF16),
                   jax.ShapeDtypeStruct((bsz, s, d), F32)],
        scratch_shapes=[pltpu.VMEM((n_quads, n_cblk, 2 * (GRID_W + 2 * CONV_PAD), LANES), U32),
                        pltpu.VMEM((CONV_K, n_cblk, 2 * GRID_W, LANES), BF16),
                        pltpu.VMEM((2 * n_quads, 2 * GRID_W, LANES), F32),
                        pltpu.VMEM((n_cblk, t, LANES), F32)],
        compiler_params=_cparams(("arbitrary", "arbitrary")),
        name="k1",
    )(x, mod[:, None, :], gmix, win, bin_, cw, cb, lng, lnb, wpa)


def _post_kernel(x_ref, hf_ref, hb_ref, gyg_ref, ma_ref, sgb_ref, mod_ref, wpb_ref, wo_ref, gffn_ref,
                 wrt_ref, brt_ref, x1_ref, hx_ref, rti_ref):
    d = x_ref.shape[2]
    b = pl.program_id(0)
    y_rec = hf_ref[0] + hb_ref[0]
    b_lat = _bdot((gyg_ref[0].astype(F32) * y_rec).astype(BF16), wpb_ref[...])
    merged = ma_ref[0].astype(F32) + sgb_ref[0].astype(F32) * b_lat
    res = _bdot(merged.astype(BF16), wo_ref[...])
    gt1 = mod_ref[pl.ds(b, 1), 2 * d:3 * d]
    x1 = x_ref[0] + gt1 * res
    x1_ref[0] = x1
    sh2 = mod_ref[pl.ds(b, 1), 3 * d:4 * d]
    sc2 = mod_ref[pl.ds(b, 1), 4 * d:5 * d]
    hm = _rms_norm(x1, gffn_ref[...]) * (1.0 + sc2) + sh2
    half = d // 2
    hx_ref[0, :, 0:half] = _pack_bf16_pair(hm[:, 0:half], hm[:, half:d])

    nt = (((1,), (1,)), ((), ()))
    hm_hi = hm.astype(BF16)
    hm_lo = (hm - hm_hi.astype(F32)).astype(BF16)
    lg = (lax.dot_general(wrt_ref[0], hm_hi, nt, preferred_element_type=F32)
          + lax.dot_general(wrt_ref[0], hm_lo, nt, preferred_element_type=F32)
          + lax.dot_general(wrt_ref[1], hm_hi, nt, preferred_element_type=F32)) + brt_ref[...]
    gl = [lg[k:k + 1, :] for k in range(N_GROUPS)]
    gmax = jnp.maximum(jnp.maximum(gl[0], gl[1]), jnp.maximum(gl[2], gl[3]))
    gsel = jnp.where(gl[0] == gmax, 0, jnp.where(gl[1] == gmax, 1, jnp.where(gl[2] == gmax, 2, 3)))
    p_grp = 1.0 / (jnp.exp(gl[0] - gmax) + jnp.exp(gl[1] - gmax) + jnp.exp(gl[2] - gmax) + jnp.exp(gl[3] - gmax))
    el = []
    for j in range(EXPERTS_PER_GROUP):
        rows = [lg[N_GROUPS + g * EXPERTS_PER_GROUP + j:N_GROUPS + g * EXPERTS_PER_GROUP + j + 1, :]
                for g in range(N_GROUPS)]
        el.append(jnp.where(gsel == 0, rows[0], jnp.where(gsel == 1, rows[1], jnp.where(gsel == 2, rows[2], rows[3]))))
    m1 = jnp.maximum(jnp.maximum(el[0], el[1]), jnp.maximum(el[2], el[3]))
    i1 = jnp.where(el[0] == m1, 0, jnp.where(el[1] == m1, 1, jnp.where(el[2] == m1, 2, 3)))
    neg = jnp.float32(-jnp.inf)
    rest = [jnp.where(i1 == j, neg, el[j]) for j in range(EXPERTS_PER_GROUP)]
    m2 = jnp.maximum(jnp.maximum(rest[0], rest[1]), jnp.maximum(rest[2], rest[3]))
    i2 = jnp.where((rest[0] == m2) & (i1 != 0), 0,
                   jnp.where((rest[1] == m2) & (i1 != 1), 1, jnp.where((rest[2] == m2) & (i1 != 2), 2, 3)))
    e21 = jnp.exp(m2 - m1)
    den = 1.0 / (1.0 + e21)
    w1 = p_grp * den
    w2 = p_grp * (e21 * den)
    first_lo = i1 < i2
    lo = jnp.where(first_lo, i1, i2)
    hi = jnp.where(first_lo, i2, i1)
    w_lo = jnp.where(first_lo, w1, w2)
    w_hi = jnp.where(first_lo, w2, w1)
    pair = jnp.where(lo == 0, hi - 1, jnp.where(lo == 1, hi + 1, 5))
    bucket = gsel * N_PAIRS + pair
    t = bucket.shape[1]
    rti_ref[...] = jnp.concatenate([bucket.astype(I32), jnp.zeros((7, t), I32)], axis=0)
    wrows = jnp.concatenate([w_lo, w_hi, jnp.zeros((LANES - 2, t), F32)], axis=0)
    hx_ref[0, :, half:half + LANES] = lax.bitcast_convert_type(wrows.T, U32)


def _post(x, hf, hb, gyg, ma, sgb, mod, wpb, wo, gffn, wrt, brt):
    bsz, s, d = x.shape
    t = brt.shape[1]
    assert s % t == 0
    nt = s // t
    n = bsz * s

    def full(a):
        return pl.BlockSpec(a.shape, lambda b, i: (0,) * a.ndim)

    def const(a):
        return pl.BlockSpec(a.shape, lambda b, i: (0,) * a.ndim, pipeline_mode=pl.Buffered(1))

    tok = lambda: pl.BlockSpec((1, t, d), lambda b, i: (b, i, 0))
    rt = lambda: pl.BlockSpec((8, t), lambda b, i: (0, b * nt + i))
    return pl.pallas_call(
        _post_kernel,
        grid=(bsz, nt),
        in_specs=[tok(), tok(), tok(), tok(), tok(), tok(), full(mod), const(wpb), const(wo), full(gffn), full(wrt),
                  full(brt)],
        out_specs=[tok(), pl.BlockSpec((1, t, d // 2 + LANES), lambda b, i: (b, i, 0)), rt()],
        out_shape=[jax.ShapeDtypeStruct((bsz, s, d), F32),
                   jax.ShapeDtypeStruct((bsz, s, d // 2 + LANES), U32),
                   jax.ShapeDtypeStruct((8, n), I32)],
        compiler_params=_cparams(("arbitrary", "arbitrary")),
        name="post",
    )(x, hf, hb, gyg, ma, sgb, mod, wpb, wo, gffn, wrt, brt)


def _sort_kernel(bk_ref, slot_ref, meta_ref, *, tm):
    bk = bk_ref[...]
    r = bk.shape[0]
    nt_lanes = meta_ref.shape[1]
    shift = tm.bit_length() - 1
    upper = (lax.broadcasted_iota(I32, (LANES, LANES), 0) < lax.broadcasted_iota(I32, (LANES, LANES), 1))
    upper = upper.astype(F32).astype(BF16)
    lower = (lax.broadcasted_iota(I32, (r, r), 1) < lax.broadcasted_iota(I32, (r, r), 0))
    lower = lower.astype(F32).astype(BF16)
    ones = jnp.ones((LANES, LANES), BF16)
    tile_pos = lax.broadcasted_iota(I32, (1, nt_lanes), 1) * tm
    start = jnp.zeros((1, LANES), I32)
    start_t = jnp.zeros((1, nt_lanes), I32)
    slot = jnp.zeros((r, LANES), I32)
    tile_bucket = jnp.zeros((1, nt_lanes), I32)
    for k in range(N_BUCKETS):
        m = bk == k
        mb = m.astype(F32).astype(BF16)
        pre = _bdot(mb, upper)
        tot = _bdot(mb, ones)
        rowpre = _bdot(lower, tot.astype(BF16))
        rank = (pre + rowpre).astype(I32)
        cnt = (rowpre[r - 1:r, :] + tot[r - 1:r, :]).astype(I32)
        slot = jnp.where(m, start + rank, slot)
        padded = ((cnt + (tm - 1)) >> shift) << shift
        start = start + padded
        start_t = start_t + jnp.concatenate([padded] * (nt_lanes // LANES), axis=1)
        tile_bucket = tile_bucket + (tile_pos >= start_t).astype(I32)
    slot_ref[...] = slot
    valid = (tile_pos < start_t).astype(I32)
    tb = jnp.minimum(tile_bucket, N_BUCKETS - 1)
    g = (tb * 43) >> 8
    pair = tb - N_PAIRS * g
    ge3 = (pair >= 3).astype(I32)
    ge5 = (pair >= 5).astype(I32)
    lo = ge3 + ge5
    hi = pair + 1 - 2 * ge3 - ge5
    e_lo = g * EXPERTS_PER_GROUP + lo
    e_hi = g * EXPERTS_PER_GROUP + hi
    meta_ref[...] = jnp.concatenate([e_lo, e_hi, valid, jnp.zeros((5, nt_lanes), I32)], axis=0)


def _sort(bucket2d, tm, nt_lanes):
    r = bucket2d.shape[0]
    return pl.pallas_call(
        functools.partial(_sort_kernel, tm=tm),
        grid=(1,),
        in_specs=[pl.BlockSpec((r, LANES), lambda i: (0, 0))],
        out_specs=[pl.BlockSpec((r, LANES), lambda i: (0, 0)),
                   pl.BlockSpec((8, nt_lanes), lambda i: (0, 0))],
        out_shape=[jax.ShapeDtypeStruct((r, LANES), I32),
                   jax.ShapeDtypeStruct((8, nt_lanes), I32)],
        compiler_params=_cparams(("arbitrary",)),
        name="bucket_sort",
    )(bucket2d)


def _moe_kernel(elo_ref, ehi_ref, valid_ref, h_ref, wg_lo, wg_hi, wu_lo, wu_hi, wd_lo, wd_hi, o_ref):
    i = pl.program_id(0)
    half = o_ref.shape[1]

    @pl.when(valid_ref[i] != 0)
    def _():
        h_a, h_b = _unpack_bf16_pair(h_ref[:, 0:half])
        h = jnp.concatenate([h_a, h_b], axis=1).astype(BF16)
        w_lo = lax.bitcast_convert_type(h_ref[:, half:half + 1], F32)
        w_hi = lax.bitcast_convert_type(h_ref[:, half + 1:half + 2], F32)
        act_lo = _silu(_bdot(h, wg_lo[0])) * _bdot(h, wu_lo[0]) * w_lo
        act_hi = _silu(_bdot(h, wg_hi[0])) * _bdot(h, wu_hi[0]) * w_hi
        o = _bdot(act_lo.astype(BF16), wd_lo[0]) + _bdot(act_hi.astype(BF16), wd_hi[0])
        o_ref[...] = _pack_bf16_pair(o[:, 0:half], o[:, half:2 * half])

    @pl.when(valid_ref[i] == 0)
    def _():
        o_ref[...] = jnp.zeros(o_ref.shape, U32)


def _moe(e_lo, e_hi, valid, hs, wgate, wup, wdown):
    npad, wrow = hs.shape
    tm = T_MOE
    n_tiles = npad // tm
    ne, d, f = wgate.shape
    lo3 = lambda i, elo, ehi, va: (elo[i], 0, 0)
    hi3 = lambda i, elo, ehi, va: (ehi[i], 0, 0)
    row = lambda i, elo, ehi, va: (i, 0)
    grid_spec = pltpu.PrefetchScalarGridSpec(
        num_scalar_prefetch=3,
        grid=(n_tiles,),
        in_specs=[pl.BlockSpec((tm, wrow), row),
                  pl.BlockSpec((1, d, f), lo3), pl.BlockSpec((1, d, f), hi3),
                  pl.BlockSpec((1, d, f), lo3), pl.BlockSpec((1, d, f), hi3),
                  pl.BlockSpec((1, f, d), lo3), pl.BlockSpec((1, f, d), hi3)],
        out_specs=pl.BlockSpec((tm, d // 2), row),
    )
    return pl.pallas_call(
        _moe_kernel,
        grid_spec=grid_spec,
        out_shape=jax.ShapeDtypeStruct((npad, d // 2), U32),
        compiler_params=_cparams(("arbitrary",)),
        name="moe",
    )(e_lo, e_hi, valid, hs, wgate, wgate, wup, wup, wdown, wdown)


SC_CORES = 2
SC_SUBCORES = 16
SC_LANES = 16
SC_WORKERS = SC_CORES * SC_SUBCORES
SC_ROWS = 64


def _sc_worker_id():
    return lax.axis_index("s") * SC_CORES + lax.axis_index("c")


def _sc_chunk_rows(rows):
    return next(c for c in (SC_ROWS, SC_ROWS // 2, SC_ROWS // 4) if rows % c == 0)


def _sc_gather_rows(src_hbm, idx_v, out_hbm, out_base, n_chunks, bufs, sems):
    chunk = bufs[0].shape[0]

    def gather(c, b):
        off = pl.multiple_of(c * chunk, chunk)
        return pltpu.make_async_copy(src_hbm.at[idx_v.at[pl.ds(off, chunk)]], bufs[b], sems[b])

    def finish(c, b):
        gather(c, b).wait()
        pltpu.sync_copy(bufs[b], out_hbm.at[pl.ds(out_base + pl.multiple_of(c * chunk, chunk), chunk)])

    gather(0, 0).start()

    @pl.loop(0, n_chunks // 2)
    def _(p):
        c = 2 * p
        gather(c + 1, 1).start()
        finish(c, 0)

        @pl.when(c + 2 < n_chunks)
        def _():
            gather(c + 2, 0).start()

        finish(c + 1, 1)

    if n_chunks % 2:
        finish(n_chunks - 1, 0)


def _sc_dispatch(hx, slot, npad):
    n, wrow = hx.shape
    assert npad % (SC_WORKERS * SC_LANES) == 0 and n % SC_LANES == 0
    rows = npad // SC_WORKERS
    chunk = _sc_chunk_rows(rows)
    mesh = plsc.VectorSubcoreMesh(core_axis_name="c", subcore_axis_name="s")

    @functools.partial(
        pl.kernel, mesh=mesh,
        out_type=jax.ShapeDtypeStruct((npad, wrow), hx.dtype),
        scratch_types=[pltpu.VMEM((n,), I32),
                       pltpu.VMEM((rows,), I32),
                       pltpu.VMEM((chunk, wrow), hx.dtype),
                       pltpu.VMEM((chunk, wrow), hx.dtype),
                       pltpu.SemaphoreType.DMA,
                       pltpu.SemaphoreType.DMA],
        compiler_params=pltpu.CompilerParams(needs_layout_passes=False),
        name="sc_dispatch",
    )
    def k(hx_hbm, slot_hbm, out_hbm, slot_v, tok_v, rows_a, rows_b, sem_a, sem_b):
        base = _sc_worker_id() * rows
        pltpu.sync_copy(slot_hbm, slot_v)

        lane = lax.iota(I32, SC_LANES)

        @pl.loop(0, rows // SC_LANES)
        def _(i):
            filler = base + i * SC_LANES + lane
            tok_v[pl.ds(i * SC_LANES, SC_LANES)] = lax.rem(filler, jnp.int32(n))

        @plsc.parallel_loop(0, n // SC_LANES, unroll=8)
        def _(i):
            loc = slot_v[pl.ds(i * SC_LANES, SC_LANES)] - base
            mine = (loc >= 0) & (loc < rows)
            plsc.store_scatter(tok_v, [loc], i * SC_LANES + lane, mask=mine)

        _sc_gather_rows(hx_hbm, tok_v, out_hbm, base, rows // chunk, (rows_a, rows_b), (sem_a, sem_b))

    return k(hx, slot)


def _sc_combine(mo_sorted, slot):
    n = slot.shape[0]
    wrow = mo_sorted.shape[1]
    assert n % (SC_WORKERS * SC_LANES) == 0
    toks = n // SC_WORKERS
    chunk = _sc_chunk_rows(toks)
    mesh = plsc.VectorSubcoreMesh(core_axis_name="c", subcore_axis_name="s")

    @functools.partial(
        pl.kernel, mesh=mesh,
        out_type=jax.ShapeDtypeStruct((n, wrow), mo_sorted.dtype),
        scratch_types=[pltpu.VMEM((toks,), I32),
                       pltpu.VMEM((chunk, wrow), mo_sorted.dtype),
                       pltpu.VMEM((chunk, wrow), mo_sorted.dtype),
                       pltpu.SemaphoreType.DMA,
                       pltpu.SemaphoreType.DMA],
        compiler_params=pltpu.CompilerParams(needs_layout_passes=False),
        name="sc_combine",
    )
    def k(mo_hbm, slot_hbm, out_hbm, idx_v, rows_a, rows_b, sem_a, sem_b):
        base = _sc_worker_id() * toks
        pltpu.sync_copy(slot_hbm.at[pl.ds(base, toks)], idx_v)
        _sc_gather_rows(mo_hbm, idx_v, out_hbm, base, toks // chunk, (rows_a, rows_b), (sem_a, sem_b))

    return k(mo_sorted, slot)


def _final_kernel(x1_ref, mo_ref, mod_ref, g_ref, *rest, b0):
    o_ref = rest[-1]
    d = x1_ref.shape[2]
    b = pl.program_id(0) + b0
    gt2 = mod_ref[pl.ds(b, 1), 5 * d:6 * d]
    mo = jnp.concatenate(_unpack_bf16_pair(mo_ref[0]), axis=1)
    o_ref[0] = _rms_norm(x1_ref[0] + gt2 * mo, g_ref[...])


def _final(x1, mo, mod, g, *, b0, prev):
    bsz, s, d = x1.shape
    nb = mo.shape[0]
    t = min(T_FINAL, s)
    assert s % t == 0
    tok = lambda: pl.BlockSpec((1, t, d), lambda b, i: (b + b0, i, 0))
    in_specs = [tok(), pl.BlockSpec((1, t, d // 2), lambda b, i: (b, i, 0)),
                pl.BlockSpec(mod.shape, lambda b, i: (0, 0)), pl.BlockSpec((1, d), lambda b, i: (0, 0))]
    args = [x1, mo, mod, g]
    aliases = {}
    if prev is not None:
        in_specs.append(pl.BlockSpec(memory_space=pl.ANY))
        aliases = {len(args): 0}
        args.append(prev)
    return pl.pallas_call(
        functools.partial(_final_kernel, b0=b0),
        grid=(nb, s // t),
        in_specs=in_specs,
        out_specs=tok(),
        out_shape=jax.ShapeDtypeStruct((bsz, s, d), F32),
        input_output_aliases=aliases,
        compiler_params=_cparams(("arbitrary", "arbitrary")),
        name="final",
    )(*args)


def _pair_blockdiag(w):
    h, hd, _ = w.shape
    w2 = w.reshape(h // 2, 2, hd, hd)
    z = jnp.zeros((h // 2, hd, hd), w.dtype)
    top = jnp.concatenate([w2[:, 0], z], axis=2)
    bot = jnp.concatenate([z, w2[:, 1]], axis=2)
    return jnp.concatenate([top, bot], axis=1)


def _gate_params(w_r, b_r, w_i, b_i):
    wg = jnp.concatenate([_pair_blockdiag(w_r), _pair_blockdiag(w_i)], axis=2).astype(BF16)
    h, hd = b_r.shape
    bg = 0.5 * jnp.concatenate([b_r.reshape(h // 2, 1, 2 * hd), b_i.reshape(h // 2, 1, 2 * hd)], axis=2)
    return wg, bg


def kernel(x, c, ctx, c_ctx, w_ada, b_ada, g_mix, w_in, b_in, conv_w, conv_b, ln_g, ln_b, w_pa, lru_conv_w, lru_conv_b, w_r_f, b_r_f, w_i_f, b_i_f, lam_f, w_r_b, b_r_b, w_i_b, b_i_b, lam_b, w_pb, w_o, g_ffn, w_grp, b_grp, w_er, b_er, w_gate, w_up, w_down, g_final):
    bsz, s, d = x.shape
    depth = w_ada.shape[0]
    assert depth == 1 and bsz + 1 <= 8
    l = 0
    n = bsz * s
    wc = conv_w.shape[2]
    wl = lru_conv_w.shape[2]
    rec0 = 2 * wc + wl
    ctx_row = bsz

    c8 = jnp.concatenate([c, c_ctx[None, :], jnp.zeros((8 - bsz - 1, d), F32)], axis=0)
    mod = _ada(c8, w_ada[l], b_ada[l][None, :])

    win = _to_bf16(w_in[l])
    bin_ = b_in[l][None, :]
    gmix = g_mix[l][None, :]
    lcw = 0.5 * lru_conv_w[l]
    lcb = 0.5 * lru_conv_b[l][None, :]
    wg_f, bg_f = _gate_params(w_r_f[l], b_r_f[l], w_i_f[l], b_i_f[l])
    wg_b, bg_b = _gate_params(w_r_b[l], b_r_b[l], w_i_b[l], b_i_b[l])
    lamf = lam_f[l][None, :]
    lamb = lam_b[l][None, :]

    xr_c = _ctxproj(ctx, mod, gmix, win[:, rec0:rec0 + wl], bin_[:, rec0:rec0 + wl], ctx_row)
    zero_state = jnp.zeros((bsz, SUBLANES, LANES), F32)
    _, h0f, xc_c = _scan(xr_c, zero_state, lcw, lcb, wg_f, bg_f, lamf, reverse=False, emit_h=False)
    _, h0b = _scan(xc_c, zero_state, lcw, lcb, wg_b, bg_b, lamb, reverse=True, emit_h=False)

    ma, gyg, sgb, xr = _k1(x, mod, gmix, win, bin_, conv_w[l], conv_b[l][None, :], ln_g[l][None, :],
                           ln_b[l][None, :], w_pa[l].astype(BF16))
    hf, _, xc = _scan(xr, h0f, lcw, lcb, wg_f, bg_f, lamf, reverse=False, emit_h=True)
    hb, _ = _scan(xc, h0b, lcw, lcb, wg_b, bg_b, lamb, reverse=True, emit_h=True)

    n_rt = 32
    wrt = jnp.concatenate([w_grp[l].T, w_er[l].T, jnp.zeros((n_rt - N_GROUPS * (1 + EXPERTS_PER_GROUP), d), F32)], axis=0)
    brt = jnp.concatenate([b_grp[l], b_er[l], jnp.zeros((n_rt - N_GROUPS * (1 + EXPERTS_PER_GROUP),), F32)])
    brt = jnp.broadcast_to(brt[:, None], (n_rt, min(T_POST, s)))
    wrt_hi = wrt.astype(BF16)
    wrt = jnp.stack([wrt_hi, (wrt - wrt_hi.astype(F32)).astype(BF16)])
    x1, hx, rti = _post(x, hf, hb, gyg, ma, sgb, mod, w_pb[l].astype(BF16), w_o[l].astype(BF16),
                        g_ffn[l][None, :], wrt, brt)

    tm = T_MOE
    n_tiles = n // tm + N_BUCKETS
    npad = n_tiles * tm
    nt_lanes = ((n_tiles + LANES - 1) // LANES) * LANES
    slot2d, meta = _sort(rti[0].reshape(n // LANES, LANES), tm, nt_lanes)
    slot = slot2d.reshape(n)
    hs = _sc_dispatch(hx.reshape(n, hx.shape[2]), slot, npad)
    mo_sorted = _moe(meta[0, :n_tiles], meta[1, :n_tiles], meta[2, :n_tiles], hs,
                     _to_bf16(w_gate[l]), _to_bf16(w_up[l]), _to_bf16(w_down[l]))
    assert bsz % FINAL_PARTS == 0
    nb = bsz // FINAL_PARTS
    out = None
    for p in range(FINAL_PARTS):
        mo = _sc_combine(mo_sorted, slot[p * nb * s:(p + 1) * nb * s]).reshape(nb, s, d // 2)
        out = _final(x1, mo, mod, g_final[None, :], b0=p * nb, prev=out)
    return out
```

```python
import functools

import jax
import jax.numpy as jnp
from jax import lax
from jax.experimental import pallas as pl
from jax.experimental.pallas import tpu as pltpu
from jax.experimental.pallas import tpu_sc as plsc

F32 = jnp.float32
BF16 = jnp.bfloat16
I32 = jnp.int32
U32 = jnp.uint32
HIGHEST = lax.Precision.HIGHEST

EPS = 1e-6
GRID_W = 64
CONV_K = 31
CONV_PAD = 16
LRU_CONV_K = 4
LRU_C = 8.0
LRU_HEADS = 8
N_GROUPS = 4
EXPERTS_PER_GROUP = 4
N_PAIRS = 6
N_BUCKETS = N_GROUPS * N_PAIRS
LANES = 128
SUBLANES = 8
VMEM_LIMIT = 56 * 1024 * 1024

T_TOK = 512
T_POST = 512
T_FINAL = 1024
T_SCAN = 256
T_MOE = 256
FINAL_PARTS = 2
CAST_BLOCK_BYTES = 8 * 1024 * 1024


def _sigmoid(x):
    return 0.5 * (jnp.tanh(0.5 * x) + 1.0)


def _silu(x):
    return x * _sigmoid(x)


def _tanh1(x):
    return jnp.tanh(x) + 1.0


def _gelu_tanh_x2(x):
    return x * (1.0 + jnp.tanh(0.7978845608028654 * (x + 0.044715 * (x * x * x))))


def _rms_norm(x, g):
    return x * lax.rsqrt(jnp.mean(x * x, axis=-1, keepdims=True) + EPS) * g


def _bdot(a, b):
    return jnp.dot(a, b, preferred_element_type=F32)


def _pack_bf16_pair(lo, hi):
    lo_bits = lax.bitcast_convert_type(lo.astype(BF16).astype(F32), U32)
    hi_bits = lax.bitcast_convert_type(hi.astype(BF16).astype(F32), U32)
    return (hi_bits & jnp.uint32(0xFFFF0000)) | (lo_bits >> 16)


def _unpack_bf16_pair(u):
    lo = lax.bitcast_convert_type(u << 16, F32)
    hi = lax.bitcast_convert_type(u & jnp.uint32(0xFFFF0000), F32)
    return lo, hi


def _cparams(sem):
    return pltpu.CompilerParams(dimension_semantics=sem, vmem_limit_bytes=VMEM_LIMIT)


def _cast_kernel(x_ref, s_ref, o_ref):
    o_ref[...] = (x_ref[...] * s_ref[...]).astype(o_ref.dtype)


def _to_bf16(w, col_scale=None):
    shape = w.shape
    w2 = w.reshape(-1, shape[-1])
    rows, cols = w2.shape
    if col_scale is None:
        col_scale = jnp.ones((1, cols), F32)
    blk = rows
    while blk % 2 == 0 and blk > 16 and blk * cols * 4 > CAST_BLOCK_BYTES:
        blk //= 2
    out = pl.pallas_call(
        _cast_kernel,
        grid=(rows // blk,),
        in_specs=[pl.BlockSpec((blk, cols), lambda i: (i, 0)), pl.BlockSpec((1, cols), lambda i: (0, 0))],
        out_specs=pl.BlockSpec((blk, cols), lambda i: (i, 0)),
        out_shape=jax.ShapeDtypeStruct((rows, cols), BF16),
        compiler_params=_cparams(("arbitrary",)),
        name="to_bf16",
    )(w2, col_scale)
    return out.reshape(shape)


def _ada_kernel(c_ref, w_ref, b_ref, o_ref):
    s = _silu(c_ref[...])
    o_ref[...] = jnp.dot(s, w_ref[...], precision=HIGHEST, preferred_element_type=F32) + b_ref[...]


def _ada(c8, w, b):
    d, n = w.shape
    blk = 1024
    return pl.pallas_call(
        _ada_kernel,
        grid=(n // blk,),
        in_specs=[pl.BlockSpec((8, d), lambda j: (0, 0)),
                  pl.BlockSpec((d, blk), lambda j: (0, j)),
                  pl.BlockSpec((1, blk), lambda j: (0, j))],
        out_specs=pl.BlockSpec((8, blk), lambda j: (0, j)),
        out_shape=jax.ShapeDtypeStruct((8, n), F32),
        compiler_params=_cparams(("arbitrary",)),
        name="ada",
    )(c8, w, b)


def _ctxproj_kernel(x_ref, mod_ref, g_ref, w_ref, b_ref, o_ref, *, row):
    d = x_ref.shape[-1]
    xn = _rms_norm(x_ref[0], g_ref[...])
    sh = mod_ref[row:row + 1, 0:d]
    sc = mod_ref[row:row + 1, d:2 * d]
    hx = (xn * (1.0 + sc) + sh).astype(BF16)
    o_ref[0] = _bdot(hx, w_ref[...]) + b_ref[...]


def _ctxproj(ctx, mod, g, w, b, row):
    bsz, n, d = ctx.shape
    return pl.pallas_call(
        functools.partial(_ctxproj_kernel, row=row),
        grid=(bsz,),
        in_specs=[pl.BlockSpec((1, n, d), lambda i: (i, 0, 0)),
                  pl.BlockSpec(mod.shape, lambda i: (0, 0)),
                  pl.BlockSpec((1, d), lambda i: (0, 0)),
                  pl.BlockSpec(w.shape, lambda i: (0, 0)),
                  pl.BlockSpec((1, w.shape[1]), lambda i: (0, 0))],
        out_specs=pl.BlockSpec((1, n, w.shape[1]), lambda i: (i, 0, 0)),
        out_shape=jax.ShapeDtypeStruct((bsz, n, w.shape[1]), F32),
        compiler_params=_cparams(("arbitrary",)),
        name="ctxproj",
    )(ctx, mod, g, w, b)


def _scan_kernel(*refs, reverse, emit_h, n_chunks):
    refs = list(refs)
    if reverse:
        xc_ref, h0_ref, wg_ref, bg_ref, lam_ref = refs[:5]
        rest = refs[5:]
        bsz, tc, w = xc_ref.shape
    else:
        xr_ref, prev_ref, next_ref, h0_ref, cw_ref, cb_ref, wg_ref, bg_ref, lam_ref = refs[:9]
        rest = refs[9:]
        bsz, tc, w = xr_ref.shape
    h_ref = rest.pop(0) if emit_h else None
    hfin_ref = rest.pop(0)
    xc_out_ref = None if reverse else rest.pop(0)
    xp_ref = None if reverse else rest.pop(0)
    a_ref, b_ref, hs_ref, hst_ref = rest
    nblk = w // LANES
    i = pl.program_id(0)
    chunk = (n_chunks - 1 - i) if reverse else i

    @pl.when(i == 0)
    def _():
        hst_ref[...] = h0_ref[...]

    z = -lam_ref[...]
    softplus = jnp.maximum(z, 0.0) + jnp.log1p(jnp.exp(-jnp.abs(z)))
    nlsh = (-0.5 * LRU_C) * softplus

    if not reverse:
        has_prev = chunk > 0
        has_next = chunk < n_chunks - 1
        for b in range(bsz):
            bp, par = divmod(b, 2)
            for cblk in range(nblk):
                cs = slice(cblk * LANES, (cblk + 1) * LANES)
                xp_ref[bp, cblk, pl.ds(par, 8, stride=2), :] = jnp.where(has_prev, prev_ref[b, :, cs], 0.0)
                xp_ref[bp, cblk, pl.ds(16 + par, tc, stride=2), :] = xr_ref[b, :, cs]
                xp_ref[bp, cblk, pl.ds(16 + 2 * tc + par, 8, stride=2), :] = (
                    jnp.where(has_next, next_ref[b, :, cs], 0.0))

    for b in range(bsz):
        bp, par = divmod(b, 2)
        xc = []
        for cblk in range(nblk):
            cs = slice(cblk * LANES, (cblk + 1) * LANES)
            if reverse:
                xc.append(xc_ref[b, :, cs])
                continue
            acc = cb_ref[:, cs] + cw_ref[0:1, cs] * xp_ref[bp, cblk, pl.ds(2 * 6 + par, tc, stride=2), :]
            for k in range(1, LRU_CONV_K):
                acc = acc + cw_ref[k:k + 1, cs] * xp_ref[bp, cblk, pl.ds(2 * (6 + k) + par, tc, stride=2), :]
            xc_out_ref[b, :, cs] = acc
            xc.append(acc)
        for p in range(LRU_HEADS // 2):
            cs = slice(2 * p * LANES, 2 * (p + 1) * LANES)
            xc_p = jnp.concatenate([xc[2 * p], xc[2 * p + 1]], axis=1)
            th = jnp.tanh(_bdot(xc_p.astype(BF16), wg_ref[p]) + bg_ref[p])
            log_a = nlsh[:, cs] * (th[:, 0:2 * LANES] + 1.0)
            a = jnp.exp(log_a)
            y = jnp.tanh(log_a) * (-1.0 - a * a)
            root = jnp.where(y > 0.0, y * lax.rsqrt(y), 0.0)
            bb = root * ((th[:, 2 * LANES:4 * LANES] + 1.0) * xc_p)
            for jj in range(2):
                j = 2 * p + jj
                a_ref[b, pl.ds(j, tc, stride=nblk), :] = a[:, jj * LANES:(jj + 1) * LANES]
                b_ref[b, pl.ds(j, tc, stride=nblk), :] = bb[:, jj * LANES:(jj + 1) * LANES]

    group = 8
    n_groups = tc // group

    def steps(gidx, hs):
        g = (n_groups - 1 - gidx) if reverse else gidx
        base = pl.multiple_of(g * (group * nblk), group * nblk)
        hs = list(hs)
        for u in range(group):
            row = base + ((group - 1 - u) if reverse else u) * nblk
            for b in range(bsz):
                h = a_ref[b, pl.ds(row, nblk), :] * hs[b] + b_ref[b, pl.ds(row, nblk), :]
                hs_ref[b, pl.ds(row, nblk), :] = h
                hs[b] = h
        return tuple(hs)

    hs = lax.fori_loop(0, n_groups, steps, tuple(hst_ref[b] for b in range(bsz)))
    for b in range(bsz):
        hst_ref[b] = hs[b]
        hfin_ref[b] = hs[b]
    if emit_h:
        for b in range(bsz):
            for j in range(nblk):
                h_ref[b, :, j * LANES:(j + 1) * LANES] = hs_ref[b, pl.ds(j, tc, stride=nblk), :]


def _scan(xin, h0, cw, cb, wg, bg, lam, *, reverse, emit_h):
    bsz, s, w = xin.shape
    tc = min(T_SCAN, s)
    assert s % tc == 0 and w == SUBLANES * LANES and bsz % 2 == 0
    n_chunks = s // tc
    hb = tc // 8

    def cidx(i):
        return (n_chunks - 1 - i) if reverse else i

    tile = lambda: pl.BlockSpec((bsz, tc, w), lambda i: (0, cidx(i), 0))
    state = lambda: pl.BlockSpec((bsz, SUBLANES, LANES), lambda i: (0, 0, 0))
    gate_specs = [pl.BlockSpec(wg.shape, lambda i: (0, 0, 0)),
                  pl.BlockSpec(bg.shape, lambda i: (0, 0, 0)),
                  pl.BlockSpec(lam.shape, lambda i: (0, 0))]
    if reverse:
        args = (xin, h0, wg, bg, lam)
        in_specs = [tile(), state()] + gate_specs
    else:
        args = (xin, xin, xin, h0, cw, cb, wg, bg, lam)
        in_specs = [tile(),
                    pl.BlockSpec((bsz, 8, w), lambda i: (0, jnp.maximum(cidx(i) * hb - 1, 0), 0)),
                    pl.BlockSpec((bsz, 8, w), lambda i: (0, jnp.minimum((cidx(i) + 1) * hb, n_chunks * hb - 1), 0)),
                    state(),
                    pl.BlockSpec(cw.shape, lambda i: (0, 0)),
                    pl.BlockSpec(cb.shape, lambda i: (0, 0))] + gate_specs
    full_seq = jax.ShapeDtypeStruct((bsz, s, w), F32)
    out_specs = [state()]
    out_shape = [jax.ShapeDtypeStruct((bsz, SUBLANES, LANES), F32)]
    if emit_h:
        out_specs = [tile()] + out_specs
        out_shape = [full_seq] + out_shape
    scratch = [pltpu.VMEM((bsz, tc * SUBLANES, LANES), F32),
               pltpu.VMEM((bsz, tc * SUBLANES, LANES), F32),
               pltpu.VMEM((bsz, tc * SUBLANES, LANES), F32),
               pltpu.VMEM((bsz, SUBLANES, LANES), F32)]
    if not reverse:
        out_specs = out_specs + [tile()]
        out_shape = out_shape + [full_seq]
        scratch = [pltpu.VMEM((bsz // 2, w // LANES, 2 * (tc + 16), LANES), F32)] + scratch
    res = pl.pallas_call(
        functools.partial(_scan_kernel, reverse=reverse, emit_h=emit_h, n_chunks=n_chunks),
        grid=(n_chunks,),
        in_specs=in_specs,
        out_specs=out_specs,
        out_shape=out_shape,
        scratch_shapes=scratch,
        compiler_params=_cparams(("arbitrary",)),
        name="scan_bwd" if reverse else "scan_fwd",
    )(*args)
    return tuple(res) if emit_h else (None,) + tuple(res)


CONV_ROWS = 4


def _k1_kernel(x_ref, mod_ref, gmix_ref, win_ref, bin_ref, cw_ref, cb_ref, lng_ref, lnb_ref, wpa_ref,
               ma_ref, gyg_ref, sgb_ref, xr_ref, zp_ref, wb_ref, de_ref, co_ref):
    t, d = x_ref.shape[1], x_ref.shape[2]
    wc = cw_ref.shape[1]
    n_cblk = wc // LANES
    n_quads = t // (CONV_ROWS * GRID_W)
    lo = CONV_PAD
    hi = CONV_PAD + GRID_W
    first = (pl.program_id(0) == 0) & (pl.program_id(1) == 0)

    @pl.when(first)
    def _():
        for k in range(CONV_K):
            for cblk in range(n_cblk):
                row = cw_ref[k:k + 1, cblk * LANES:(cblk + 1) * LANES]
                wb_ref[k, cblk] = jnp.broadcast_to(row, (2 * GRID_W, LANES)).astype(BF16)

    xn = _rms_norm(x_ref[0], gmix_ref[...])
    sh1 = mod_ref[0, :, 0:d]
    sc1 = mod_ref[0, :, d:2 * d]
    hx = (xn * (1.0 + sc1) + sh1).astype(BF16)

    def proj(c0, c1):
        return _bdot(hx, win_ref[:, c0:c1]) + bin_ref[:, c0:c1]

    zglu = proj(0, wc) * _tanh1(proj(wc, 2 * wc))

    zp_ref[:, :, 0:2 * lo, :] = jnp.zeros((n_quads, n_cblk, 2 * lo, LANES), U32)
    zp_ref[:, :, 2 * hi:2 * (hi + CONV_PAD), :] = jnp.zeros((n_quads, n_cblk, 2 * CONV_PAD, LANES), U32)
    for q in range(n_quads):
        for par in range(2):
            r0 = (CONV_ROWS * q + 2 * par) * GRID_W
            for cblk in range(n_cblk):
                cs = slice(cblk * LANES, (cblk + 1) * LANES)
                zp_ref[q, cblk, pl.ds(2 * lo + par, GRID_W, stride=2), :] = _pack_bf16_pair(
                    zglu[r0:r0 + GRID_W, cs], zglu[r0 + GRID_W:r0 + 2 * GRID_W, cs])

    o = 2 * wc
    gyg_ref[0] = _gelu_tanh_x2(proj(o, o + d)).astype(BF16)
    xr_ref[0] = proj(o + d, o + 2 * d)

    def conv_block(cblk):
        bias = cb_ref[cblk]
        for q in range(n_quads):
            for par in range(2):
                r0 = (CONV_ROWS * q + 2 * par) * GRID_W
                acc = jnp.zeros((2 * GRID_W, LANES), F32)
                for k in range(CONV_K):
                    off = lo - CONV_K // 2 + k
                    words = zp_ref[q, cblk, pl.ds(2 * off + par, GRID_W, stride=2), :]
                    acc = acc + pltpu.bitcast(words, BF16).astype(F32) * wb_ref[k, cblk].astype(F32)
                slot = 2 * q + par
                de_ref[slot] = acc
                co_ref[cblk, r0:r0 + GRID_W, :] = de_ref[slot, pl.ds(0, GRID_W, stride=2), :] + bias
                co_ref[cblk, r0 + GRID_W:r0 + 2 * GRID_W, :] = de_ref[slot, pl.ds(1, GRID_W, stride=2), :] + bias

    def conv_body(cblk, carry):
        conv_block(cblk)
        return carry

    lax.fori_loop(0, n_cblk, conv_body, 0)

    c = jnp.concatenate([co_ref[cblk] for cblk in range(n_cblk)], axis=1)
    mu = jnp.mean(c, axis=-1, keepdims=True)
    cc = c - mu
    y = cc * lax.rsqrt(jnp.mean(cc * cc, axis=-1, keepdims=True) + EPS) * lng_ref[...] + lnb_ref[...]
    act = _silu(y).astype(BF16)
    sga = _tanh1(proj(o + 2 * d, o + 3 * d))
    sgb_ref[0] = _tanh1(proj(o + 3 * d, o + 4 * d)).astype(BF16)
    ma_ref[0] = (sga * _bdot(act, wpa_ref[...])).astype(BF16)


def _k1(x, mod, gmix, win, bin_, cw, cb, lng, lnb, wpa):
    bsz, s, d = x.shape
    t = T_TOK
    assert s % t == 0 and t % (CONV_ROWS * GRID_W) == 0
    wc = cw.shape[1]
    n_cblk = wc // LANES
    n_quads = t // (CONV_ROWS * GRID_W)
    cb = cb.reshape(n_cblk, 1, LANES)

    def full(a):
        return pl.BlockSpec(a.shape, lambda b, i: (0,) * a.ndim)

    def const(a):
        return pl.BlockSpec(a.shape, lambda b, i: (0,) * a.ndim, pipeline_mode=pl.Buffered(1))

    tok = lambda: pl.BlockSpec((1, t, d), lambda b, i: (b, i, 0))
    return pl.pallas_call(
        _k1_kernel,
        grid=(bsz, s // t),
        in_specs=[tok(), pl.BlockSpec((1, 1, mod.shape[1]), lambda b, i: (b, 0, 0)), full(gmix), const(win),
                  full(bin_), full(cw), full(cb), full(lng), full(lnb), const(wpa)],
        out_specs=[tok(), tok(), tok(), tok()],
        out_shape=[jax.ShapeDtypeStruct((bsz, s, d), BF16),
                   jax.ShapeDtypeStruct((bsz, s, d), BF16),
                   jax.ShapeDtypeStruct((bsz, s, d), BF16),
                   jax.ShapeDtypeStruct((bsz, s, d), F32)],
        scratch_shapes=[pltpu.VMEM((n_quads, n_cblk, 2 * (GRID_W + 2 * CONV_PAD), LANES), U32),
                        pltpu.VMEM((CONV_K, n_cblk, 2 * GRID_W, LANES), BF16),
                        pltpu.VMEM((2 * n_quads, 2 * GRID_W, LANES), F32),
                        pltpu.VMEM((n_cblk, t, LANES), F32)],
        compiler_params=_cparams(("arbitrary", "arbitrary")),
        name="k1",
    )(x, mod[:, None, :], gmix, win, bin_, cw, cb, lng, lnb, wpa)


def _post_kernel(x_ref, hf_ref, hb_ref, gyg_ref, ma_ref, sgb_ref, mod_ref, wpb_ref, wo_ref, gffn_ref,
                 wrt_ref, brt_ref, x1_ref, hx_ref, rti_ref):
    d = x_ref.shape[2]
    b = pl.program_id(0)
    y_rec = hf_ref[0] + hb_ref[0]
    b_lat = _bdot((gyg_ref[0].astype(F32) * y_rec).astype(BF16), wpb_ref[...])
    merged = ma_ref[0].astype(F32) + sgb_ref[0].astype(F32) * b_lat
    res = _bdot(merged.astype(BF16), wo_ref[...])
    gt1 = mod_ref[pl.ds(b, 1), 2 * d:3 * d]
    x1 = x_ref[0] + gt1 * res
    x1_ref[0] = x1
    sh2 = mod_ref[pl.ds(b, 1), 3 * d:4 * d]
    sc2 = mod_ref[pl.ds(b, 1), 4 * d:5 * d]
    hm = _rms_norm(x1, gffn_ref[...]) * (1.0 + sc2) + sh2
    half = d // 2
    hx_ref[0, :, 0:half] = _pack_bf16_pair(hm[:, 0:half], hm[:, half:d])

    nt = (((1,), (1,)), ((), ()))
    hm_hi = hm.astype(BF16)
    hm_lo = (hm - hm_hi.astype(F32)).astype(BF16)
    lg = (lax.dot_general(wrt_ref[0], hm_hi, nt, preferred_element_type=F32)
          + lax.dot_general(wrt_ref[0], hm_lo, nt, preferred_element_type=F32)
          + lax.dot_general(wrt_ref[1], hm_hi, nt, preferred_element_type=F32)) + brt_ref[...]
    gl = [lg[k:k + 1, :] for k in range(N_GROUPS)]
    gmax = jnp.maximum(jnp.maximum(gl[0], gl[1]), jnp.maximum(gl[2], gl[3]))
    gsel = jnp.where(gl[0] == gmax, 0, jnp.where(gl[1] == gmax, 1, jnp.where(gl[2] == gmax, 2, 3)))
    p_grp = 1.0 / (jnp.exp(gl[0] - gmax) + jnp.exp(gl[1] - gmax) + jnp.exp(gl[2] - gmax) + jnp.exp(gl[3] - gmax))
    el = []
    for j in range(EXPERTS_PER_GROUP):
        rows = [lg[N_GROUPS + g * EXPERTS_PER_GROUP + j:N_GROUPS + g * EXPERTS_PER_GROUP + j + 1, :]
                for g in range(N_GROUPS)]
        el.append(jnp.where(gsel == 0, rows[0], jnp.where(gsel == 1, rows[1], jnp.where(gsel == 2, rows[2], rows[3]))))
    m1 = jnp.maximum(jnp.maximum(el[0], el[1]), jnp.maximum(el[2], el[3]))
    i1 = jnp.where(el[0] == m1, 0, jnp.where(el[1] == m1, 1, jnp.where(el[2] == m1, 2, 3)))
    neg = jnp.float32(-jnp.inf)
    rest = [jnp.where(i1 == j, neg, el[j]) for j in range(EXPERTS_PER_GROUP)]
    m2 = jnp.maximum(jnp.maximum(rest[0], rest[1]), jnp.maximum(rest[2], rest[3]))
    i2 = jnp.where((rest[0] == m2) & (i1 != 0), 0,
                   jnp.where((rest[1] == m2) & (i1 != 1), 1, jnp.where((rest[2] == m2) & (i1 != 2), 2, 3)))
    e21 = jnp.exp(m2 - m1)
    den = 1.0 / (1.0 + e21)
    w1 = p_grp * den
    w2 = p_grp * (e21 * den)
    first_lo = i1 < i2
    lo = jnp.where(first_lo, i1, i2)
    hi = jnp.where(first_lo, i2, i1)
    w_lo = jnp.where(first_lo, w1, w2)
    w_hi = jnp.where(first_lo, w2, w1)
    pair = jnp.where(lo == 0, hi - 1, jnp.where(lo == 1, hi + 1, 5))
    bucket = gsel * N_PAIRS + pair
    t = bucket.shape[1]
    rti_ref[...] = jnp.concatenate([bucket.astype(I32), jnp.zeros((7, t), I32)], axis=0)
    wrows = jnp.concatenate([w_lo, w_hi, jnp.zeros((LANES - 2, t), F32)], axis=0)
    hx_ref[0, :, half:half + LANES] = lax.bitcast_convert_type(wrows.T, U32)


def _post(x, hf, hb, gyg, ma, sgb, mod, wpb, wo, gffn, wrt, brt):
    bsz, s, d = x.shape
    t = brt.shape[1]
    assert s % t == 0
    nt = s // t
    n = bsz * s

    def full(a):
        return pl.BlockSpec(a.shape, lambda b, i: (0,) * a.ndim)

    def const(a):
        return pl.BlockSpec(a.shape, lambda b, i: (0,) * a.ndim, pipeline_mode=pl.Buffered(1))

    tok = lambda: pl.BlockSpec((1, t, d), lambda b, i: (b, i, 0))
    rt = lambda: pl.BlockSpec((8, t), lambda b, i: (0, b * nt + i))
    return pl.pallas_call(
        _post_kernel,
        grid=(bsz, nt),
        in_specs=[tok(), tok(), tok(), tok(), tok(), tok(), full(mod), const(wpb), const(wo), full(gffn), full(wrt),
                  full(brt)],
        out_specs=[tok(), pl.BlockSpec((1, t, d // 2 + LANES), lambda b, i: (b, i, 0)), rt()],
        out_shape=[jax.ShapeDtypeStruct((bsz, s, d), F32),
                   jax.ShapeDtypeStruct((bsz, s, d // 2 + LANES), U32),
                   jax.ShapeDtypeStruct((8, n), I32)],
        compiler_params=_cparams(("arbitrary", "arbitrary")),
        name="post",
    )(x, hf, hb, gyg, ma, sgb, mod, wpb, wo, gffn, wrt, brt)


def _sort_kernel(bk_ref, slot_ref, meta_ref, *, tm):
    bk = bk_ref[...]
    r = bk.shape[0]
    nt_lanes = meta_ref.shape[1]
    shift = tm.bit_length() - 1
    upper = (lax.broadcasted_iota(I32, (LANES, LANES), 0) < lax.broadcasted_iota(I32, (LANES, LANES), 1))
    upper = upper.astype(F32).astype(BF16)
    lower = (lax.broadcasted_iota(I32, (r, r), 1) < lax.broadcasted_iota(I32, (r, r), 0))
    lower = lower.astype(F32).astype(BF16)
    ones = jnp.ones((LANES, LANES), BF16)
    tile_pos = lax.broadcasted_iota(I32, (1, nt_lanes), 1) * tm
    start = jnp.zeros((1, LANES), I32)
    start_t = jnp.zeros((1, nt_lanes), I32)
    slot = jnp.zeros((r, LANES), I32)
    tile_bucket = jnp.zeros((1, nt_lanes), I32)
    for k in range(N_BUCKETS):
        m = bk == k
        mb = m.astype(F32).astype(BF16)
        pre = _bdot(mb, upper)
        tot = _bdot(mb, ones)
        rowpre = _bdot(lower, tot.astype(BF16))
        rank = (pre + rowpre).astype(I32)
        cnt = (rowpre[r - 1:r, :] + tot[r - 1:r, :]).astype(I32)
        slot = jnp.where(m, start + rank, slot)
        padded = ((cnt + (tm - 1)) >> shift) << shift
        start = start + padded
        start_t = start_t + jnp.concatenate([padded] * (nt_lanes // LANES), axis=1)
        tile_bucket = tile_bucket + (tile_pos >= start_t).astype(I32)
    slot_ref[...] = slot
    valid = (tile_pos < start_t).astype(I32)
    tb = jnp.minimum(tile_bucket, N_BUCKETS - 1)
    g = (tb * 43) >> 8
    pair = tb - N_PAIRS * g
    ge3 = (pair >= 3).astype(I32)
    ge5 = (pair >= 5).astype(I32)
    lo = ge3 + ge5
    hi = pair + 1 - 2 * ge3 - ge5
    e_lo = g * EXPERTS_PER_GROUP + lo
    e_hi = g * EXPERTS_PER_GROUP + hi
    meta_ref[...] = jnp.concatenate([e_lo, e_hi, valid, jnp.zeros((5, nt_lanes), I32)], axis=0)


def _sort(bucket2d, tm, nt_lanes):
    r = bucket2d.shape[0]
    return pl.pallas_call(
        functools.partial(_sort_kernel, tm=tm),
        grid=(1,),
        in_specs=[pl.BlockSpec((r, LANES), lambda i: (0, 0))],
        out_specs=[pl.BlockSpec((r, LANES), lambda i: (0, 0)),
                   pl.BlockSpec((8, nt_lanes), lambda i: (0, 0))],
        out_shape=[jax.ShapeDtypeStruct((r, LANES), I32),
                   jax.ShapeDtypeStruct((8, nt_lanes), I32)],
        compiler_params=_cparams(("arbitrary",)),
        name="bucket_sort",
    )(bucket2d)


def _moe_kernel(elo_ref, ehi_ref, valid_ref, h_ref, wg_lo, wg_hi, wu_lo, wu_hi, wd_lo, wd_hi, o_ref):
    i = pl.program_id(0)
    half = o_ref.shape[1]

    @pl.when(valid_ref[i] != 0)
    def _():
        h_a, h_b = _unpack_bf16_pair(h_ref[:, 0:half])
        h = jnp.concatenate([h_a, h_b], axis=1).astype(BF16)
        w_lo = lax.bitcast_convert_type(h_ref[:, half:half + 1], F32)
        w_hi = lax.bitcast_convert_type(h_ref[:, half + 1:half + 2], F32)
        act_lo = _silu(_bdot(h, wg_lo[0])) * _bdot(h, wu_lo[0]) * w_lo
        act_hi = _silu(_bdot(h, wg_hi[0])) * _bdot(h, wu_hi[0]) * w_hi
        o = _bdot(act_lo.astype(BF16), wd_lo[0]) + _bdot(act_hi.astype(BF16), wd_hi[0])
        o_ref[...] = _pack_bf16_pair(o[:, 0:half], o[:, half:2 * half])

    @pl.when(valid_ref[i] == 0)
    def _():
        o_ref[...] = jnp.zeros(o_ref.shape, U32)


def _moe(e_lo, e_hi, valid, hs, wgate, wup, wdown):
    npad, wrow = hs.shape
    tm = T_MOE
    n_tiles = npad // tm
    ne, d, f = wgate.shape
    lo3 = lambda i, elo, ehi, va: (elo[i], 0, 0)
    hi3 = lambda i, elo, ehi, va: (ehi[i], 0, 0)
    row = lambda i, elo, ehi, va: (i, 0)
    grid_spec = pltpu.PrefetchScalarGridSpec(
        num_scalar_prefetch=3,
        grid=(n_tiles,),
        in_specs=[pl.BlockSpec((tm, wrow), row),
                  pl.BlockSpec((1, d, f), lo3), pl.BlockSpec((1, d, f), hi3),
                  pl.BlockSpec((1, d, f), lo3), pl.BlockSpec((1, d, f), hi3),
                  pl.BlockSpec((1, f, d), lo3), pl.BlockSpec((1, f, d), hi3)],
        out_specs=pl.BlockSpec((tm, d // 2), row),
    )
    return pl.pallas_call(
        _moe_kernel,
        grid_spec=grid_spec,
        out_shape=jax.ShapeDtypeStruct((npad, d // 2), U32),
        compiler_params=_cparams(("arbitrary",)),
        name="moe",
    )(e_lo, e_hi, valid, hs, wgate, wgate, wup, wup, wdown, wdown)


SC_CORES = 2
SC_SUBCORES = 16
SC_LANES = 16
SC_WORKERS = SC_CORES * SC_SUBCORES
SC_ROWS = 64


def _sc_worker_id():
    return lax.axis_index("s") * SC_CORES + lax.axis_index("c")


def _sc_chunk_rows(rows):
    return next(c for c in (SC_ROWS, SC_ROWS // 2, SC_ROWS // 4) if rows % c == 0)


def _sc_gather_rows(src_hbm, idx_v, out_hbm, out_base, n_chunks, bufs, sems):
    chunk = bufs[0].shape[0]

    def gather(c, b):
        off = pl.multiple_of(c * chunk, chunk)
        return pltpu.make_async_copy(src_hbm.at[idx_v.at[pl.ds(off, chunk)]], bufs[b], sems[b])

    def finish(c, b):
        gather(c, b).wait()
        pltpu.sync_copy(bufs[b], out_hbm.at[pl.ds(out_base + pl.multiple_of(c * chunk, chunk), chunk)])

    gather(0, 0).start()

    @pl.loop(0, n_chunks // 2)
    def _(p):
        c = 2 * p
        gather(c + 1, 1).start()
        finish(c, 0)

        @pl.when(c + 2 < n_chunks)
        def _():
            gather(c + 2, 0).start()

        finish(c + 1, 1)

    if n_chunks % 2:
        finish(n_chunks - 1, 0)


def _sc_dispatch(hx, slot, npad):
    n, wrow = hx.shape
    assert npad % (SC_WORKERS * SC_LANES) == 0 and n % SC_LANES == 0
    rows = npad // SC_WORKERS
    chunk = _sc_chunk_rows(rows)
    mesh = plsc.VectorSubcoreMesh(core_axis_name="c", subcore_axis_name="s")

    @functools.partial(
        pl.kernel, mesh=mesh,
        out_type=jax.ShapeDtypeStruct((npad, wrow), hx.dtype),
        scratch_types=[pltpu.VMEM((n,), I32),
                       pltpu.VMEM((rows,), I32),
                       pltpu.VMEM((chunk, wrow), hx.dtype),
                       pltpu.VMEM((chunk, wrow), hx.dtype),
                       pltpu.SemaphoreType.DMA,
                       pltpu.SemaphoreType.DMA],
        compiler_params=pltpu.CompilerParams(needs_layout_passes=False),
        name="sc_dispatch",
    )
    def k(hx_hbm, slot_hbm, out_hbm, slot_v, tok_v, rows_a, rows_b, sem_a, sem_b):
        base = _sc_worker_id() * rows
        pltpu.sync_copy(slot_hbm, slot_v)

        lane = lax.iota(I32, SC_LANES)

        @pl.loop(0, rows // SC_LANES)
        def _(i):
            filler = base + i * SC_LANES + lane
            tok_v[pl.ds(i * SC_LANES, SC_LANES)] = lax.rem(filler, jnp.int32(n))

        @plsc.parallel_loop(0, n // SC_LANES, unroll=8)
        def _(i):
            loc = slot_v[pl.ds(i * SC_LANES, SC_LANES)] - base
            mine = (loc >= 0) & (loc < rows)
            plsc.store_scatter(tok_v, [loc], i * SC_LANES + lane, mask=mine)

        _sc_gather_rows(hx_hbm, tok_v, out_hbm, base, rows // chunk, (rows_a, rows_b), (sem_a, sem_b))

    return k(hx, slot)


def _sc_combine(mo_sorted, slot):
    n = slot.shape[0]
    wrow = mo_sorted.shape[1]
    assert n % (SC_WORKERS * SC_LANES) == 0
    toks = n // SC_WORKERS
    chunk = _sc_chunk_rows(toks)
    mesh = plsc.VectorSubcoreMesh(core_axis_name="c", subcore_axis_name="s")

    @functools.partial(
        pl.kernel, mesh=mesh,
        out_type=jax.ShapeDtypeStruct((n, wrow), mo_sorted.dtype),
        scratch_types=[pltpu.VMEM((toks,), I32),
                       pltpu.VMEM((chunk, wrow), mo_sorted.dtype),
                       pltpu.VMEM((chunk, wrow), mo_sorted.dtype),
                       pltpu.SemaphoreType.DMA,
                       pltpu.SemaphoreType.DMA],
        compiler_params=pltpu.CompilerParams(needs_layout_passes=False),
        name="sc_combine",
    )
    def k(mo_hbm, slot_hbm, out_hbm, idx_v, rows_a, rows_b, sem_a, sem_b):
        base = _sc_worker_id() * toks
        pltpu.sync_copy(slot_hbm.at[pl.ds(base, toks)], idx_v)
        _sc_gather_rows(mo_hbm, idx_v, out_hbm, base, toks // chunk, (rows_a, rows_b), (sem_a, sem_b))

    return k(mo_sorted, slot)


def _final_kernel(x1_ref, mo_ref, mod_ref, g_ref, *rest, b0):
    o_ref = rest[-1]
    d = x1_ref.shape[2]
    b = pl.program_id(0) + b0
    gt2 = mod_ref[pl.ds(b, 1), 5 * d:6 * d]
    mo = jnp.concatenate(_unpack_bf16_pair(mo_ref[0]), axis=1)
    o_ref[0] = _rms_norm(x1_ref[0] + gt2 * mo, g_ref[...])


def _final(x1, mo, mod, g, *, b0, prev):
    bsz, s, d = x1.shape
    nb = mo.shape[0]
    t = min(T_FINAL, s)
    assert s % t == 0
    tok = lambda: pl.BlockSpec((1, t, d), lambda b, i: (b + b0, i, 0))
    in_specs = [tok(), pl.BlockSpec((1, t, d // 2), lambda b, i: (b, i, 0)),
                pl.BlockSpec(mod.shape, lambda b, i: (0, 0)), pl.BlockSpec((1, d), lambda b, i: (0, 0))]
    args = [x1, mo, mod, g]
    aliases = {}
    if prev is not None:
        in_specs.append(pl.BlockSpec(memory_space=pl.ANY))
        aliases = {len(args): 0}
        args.append(prev)
    return pl.pallas_call(
        functools.partial(_final_kernel, b0=b0),
        grid=(nb, s // t),
        in_specs=in_specs,
        out_specs=tok(),
        out_shape=jax.ShapeDtypeStruct((bsz, s, d), F32),
        input_output_aliases=aliases,
        compiler_params=_cparams(("arbitrary", "arbitrary")),
        name="final",
    )(*args)


def _pair_blockdiag(w):
    h, hd, _ = w.shape
    w2 = w.reshape(h // 2, 2, hd, hd)
    z = jnp.zeros((h // 2, hd, hd), w.dtype)
    top = jnp.concatenate([w2[:, 0], z], axis=2)
    bot = jnp.concatenate([z, w2[:, 1]], axis=2)
    return jnp.concatenate([top, bot], axis=1)


def _gate_params(w_r, b_r, w_i, b_i):
    wg = jnp.concatenate([_pair_blockdiag(w_r), _pair_blockdiag(w_i)], axis=2).astype(BF16)
    h, hd = b_r.shape
    bg = 0.5 * jnp.concatenate([b_r.reshape(h // 2, 1, 2 * hd), b_i.reshape(h // 2, 1, 2 * hd)], axis=2)
    return wg, bg


def kernel(x, c, ctx, c_ctx, w_ada, b_ada, g_mix, w_in, b_in, conv_w, conv_b, ln_g, ln_b, w_pa, lru_conv_w, lru_conv_b, w_r_f, b_r_f, w_i_f, b_i_f, lam_f, w_r_b, b_r_b, w_i_b, b_i_b, lam_b, w_pb, w_o, g_ffn, w_grp, b_grp, w_er, b_er, w_gate, w_up, w_down, g_final):
    bsz, s, d = x.shape
    depth = w_ada.shape[0]
    assert depth == 1 and bsz + 1 <= 8
    l = 0
    n = bsz * s
    wc = conv_w.shape[2]
    wl = lru_conv_w.shape[2]
    rec0 = 2 * wc + wl
    ctx_row = bsz

    c8 = jnp.concatenate([c, c_ctx[None, :], jnp.zeros((8 - bsz - 1, d), F32)], axis=0)
    mod = _ada(c8, w_ada[l], b_ada[l][None, :])

    half_cols = lambda k: jnp.full((1, k), 0.5, F32)
    one_cols = lambda k: jnp.ones((1, k), F32)
    in_scale = jnp.concatenate([half_cols(2 * wc), one_cols(d + wl), half_cols(2 * d)], axis=1)
    win = _to_bf16(w_in[l], in_scale)
    bin_ = b_in[l][None, :] * in_scale
    gmix = g_mix[l][None, :]
    lcw = 0.5 * lru_conv_w[l]
    lcb = 0.5 * lru_conv_b[l][None, :]
    wg_f, bg_f = _gate_params(w_r_f[l], b_r_f[l], w_i_f[l], b_i_f[l])
    wg_b, bg_b = _gate_params(w_r_b[l], b_r_b[l], w_i_b[l], b_i_b[l])
    lamf = lam_f[l][None, :]
    lamb = lam_b[l][None, :]

    xr_c = _ctxproj(ctx, mod, gmix, win[:, rec0:rec0 + wl], bin_[:, rec0:rec0 + wl], ctx_row)
    zero_state = jnp.zeros((bsz, SUBLANES, LANES), F32)
    _, h0f, xc_c = _scan(xr_c, zero_state, lcw, lcb, wg_f, bg_f, lamf, reverse=False, emit_h=False)
    _, h0b = _scan(xc_c, zero_state, lcw, lcb, wg_b, bg_b, lamb, reverse=True, emit_h=False)

    ma, gyg, sgb, xr = _k1(x, mod, gmix, win, bin_, conv_w[l], conv_b[l][None, :], ln_g[l][None, :],
                           ln_b[l][None, :], _to_bf16(w_pa[l], half_cols(d)))
    hf, _, xc = _scan(xr, h0f, lcw, lcb, wg_f, bg_f, lamf, reverse=False, emit_h=True)
    hb, _ = _scan(xc, h0b, lcw, lcb, wg_b, bg_b, lamb, reverse=True, emit_h=True)

    n_rt = 32
    wrt = jnp.concatenate([w_grp[l].T, w_er[l].T, jnp.zeros((n_rt - N_GROUPS * (1 + EXPERTS_PER_GROUP), d), F32)], axis=0)
    brt = jnp.concatenate([b_grp[l], b_er[l], jnp.zeros((n_rt - N_GROUPS * (1 + EXPERTS_PER_GROUP),), F32)])
    brt = jnp.broadcast_to(brt[:, None], (n_rt, min(T_POST, s)))
    wrt_hi = wrt.astype(BF16)
    wrt = jnp.stack([wrt_hi, (wrt - wrt_hi.astype(F32)).astype(BF16)])
    x1, hx, rti = _post(x, hf, hb, gyg, ma, sgb, mod, _to_bf16(w_pb[l], 0.5 * half_cols(d)), w_o[l].astype(BF16),
                        g_ffn[l][None, :], wrt, brt)

    tm = T_MOE
    n_tiles = n // tm + N_BUCKETS
    npad = n_tiles * tm
    nt_lanes = ((n_tiles + LANES - 1) // LANES) * LANES
    slot2d, meta = _sort(rti[0].reshape(n // LANES, LANES), tm, nt_lanes)
    slot = slot2d.reshape(n)
    hs = _sc_dispatch(hx.reshape(n, hx.shape[2]), slot, npad)
    mo_sorted = _moe(meta[0, :n_tiles], meta[1, :n_tiles], meta[2, :n_tiles], hs,
                     _to_bf16(w_gate[l]), _to_bf16(w_up[l]), _to_bf16(w_down[l]))
    assert bsz % FINAL_PARTS == 0
    nb = bsz // FINAL_PARTS
    out = None
    for p in range(FINAL_PARTS):
        mo = _sc_combine(mo_sorted, slot[p * nb * s:(p + 1) * nb * s]).reshape(nb, s, d // 2)
        out = _final(x1, mo, mod, g_final[None, :], b0=p * nb, prev=out)
    return out
```
